```python
import jax
import jax.numpy as jnp
from jax import lax
import numpy as np

D_MODEL = 2048
BATCH = 2
SEQ = 8192
DEPTH = 1

GRID_W = 64
CTX_LEN = 256
D_MIX = D_MODEL
D_ATTN = D_MIX // 2
D_RWKV = D_MIX - D_ATTN
HEAD_DIM = 64
N_HEADS_A = D_ATTN // HEAD_DIM
N_KV_HEADS = 4
GQA_GROUP = N_HEADS_A // N_KV_HEADS
WINDOW = 128
BLOCK = 128
BAND = BLOCK + 2 * WINDOW
ROPE_BASE = 10000.0
RWKV_HEAD = 64
N_HEADS_R = D_RWKV // RWKV_HEAD
D_DECAY_LORA = max(32, int(round(1.8 * D_RWKV ** 0.5 / 32)) * 32)
D_AAA_LORA = max(32, int(round(1.8 * D_RWKV ** 0.5 / 32)) * 32)
D_GATE_LORA = max(32, int(round(0.6 * D_RWKV ** 0.8 / 32)) * 32)
D_FF = -(-(8 * D_MODEL) // (3 * 256)) * 256
NORM_EPS = 1e-6
LNX_EPS = 64e-5
MASK_VALUE = -1e30
ATTN_SPLITS = (D_ATTN, N_KV_HEADS * HEAD_DIM, N_KV_HEADS * HEAD_DIM)
RWKV_SPLITS = (D_RWKV, D_RWKV, D_RWKV, D_DECAY_LORA, D_DECAY_LORA, D_AAA_LORA, D_AAA_LORA, D_GATE_LORA)
D_IN_ATTN = sum(ATTN_SPLITS)
D_IN_RWKV = sum(RWKV_SPLITS)
D_IN = D_IN_ATTN + D_IN_RWKV
F32 = jnp.float32

kernel_name = 'hybrid_swa_rwkv7_dit_block'


def _split(z, sizes):
    return jnp.split(z, [int(s) for s in np.cumsum(sizes)[:-1]], axis=-1)


def rmsnorm(x, g):
    xf = x.astype(F32)
    y = xf * lax.rsqrt(jnp.mean(xf * xf, axis=-1, keepdims=True) + NORM_EPS)
    return (y * g.astype(F32)).astype(x.dtype)


def modulate(h, shift, scale):
    return h * (1 + scale) + shift


def swiglu(h, w1, w3, w2):
    return (jax.nn.silu(h @ w1) * (h @ w3)) @ w2


def axial_rope(x, row, col):
    half = HEAD_DIM // 2
    nf = half // 2
    freqs = 1.0 / (ROPE_BASE ** (jnp.arange(nf, dtype=F32) / nf))

    def rot(xa, pos):
        ang = pos.astype(F32)[:, None] * freqs[None, :]
        cos = jnp.cos(ang)[None, :, None, :]
        sin = jnp.sin(ang)[None, :, None, :]
        x1, x2 = xa[..., :nf], xa[..., nf:]
        return jnp.concatenate([x1 * cos - x2 * sin, x2 * cos + x1 * sin], axis=-1)

    xf = x.astype(F32)
    return jnp.concatenate([rot(xf[..., :half], row), rot(xf[..., half:], col)], axis=-1).astype(x.dtype)


def sink_softmax(s, sink_hg):
    sk = sink_hg[None, :, :, None, None]
    m = jnp.maximum(jnp.max(s, axis=-1, keepdims=True), sk)
    p = jnp.exp(s - m)
    return p / (jnp.sum(p, axis=-1, keepdims=True) + jnp.exp(sk - m))


def windowed_gqa(q, k, v, k_ctx, v_ctx, sink):
    B, T = q.shape[0], q.shape[1]
    nb = T // BLOCK
    scale = HEAD_DIM ** -0.5
    qb = (q.astype(F32) * scale).reshape(B, nb, BLOCK, N_KV_HEADS, GQA_GROUP, HEAD_DIM)
    qb = qb.transpose(1, 0, 2, 3, 4, 5)
    pad = ((0, 0), (WINDOW, WINDOW), (0, 0), (0, 0))
    kp = jnp.pad(k.astype(F32), pad)
    vp = jnp.pad(v.astype(F32), pad)
    kc = k_ctx.astype(F32)
    vc = v_ctx.astype(F32)
    sink_hg = sink.astype(F32).reshape(N_KV_HEADS, GQA_GROUP)
    offs_q = jnp.arange(BLOCK)
    offs_k = jnp.arange(BAND) - WINDOW
    in_window = jnp.abs(offs_q[:, None] - offs_k[None, :]) <= WINDOW

    def one_block(args):
        qi, bi = args
        start = bi * BLOCK
        kb = lax.dynamic_slice_in_dim(kp, start, BAND, axis=1)
        vb = lax.dynamic_slice_in_dim(vp, start, BAND, axis=1)
        kpos = start + offs_k
        valid = in_window & ((kpos >= 0) & (kpos < T))[None, :]
        s_band = jnp.where(valid, jnp.einsum('bqhgd,bkhd->bhgqk', qi, kb), MASK_VALUE)
        s_ctx = jnp.einsum('bqhgd,bkhd->bhgqk', qi, kc)
        p = sink_softmax(jnp.concatenate([s_band, s_ctx], axis=-1), sink_hg)
        return jnp.einsum('bhgqk,bkhd->bqhgd', p, jnp.concatenate([vb, vc], axis=1))

    out = lax.map(one_block, (qb, jnp.arange(nb)))
    return out.transpose(1, 0, 2, 3, 4, 5).reshape(B, T, D_ATTN).astype(q.dtype)


def context_gqa(q, k, v, sink):
    B, C = q.shape[0], q.shape[1]
    qf = (q.astype(F32) * HEAD_DIM ** -0.5).reshape(B, C, N_KV_HEADS, GQA_GROUP, HEAD_DIM)
    s = jnp.einsum('bqhgd,bkhd->bhgqk', qf, k.astype(F32))
    p = sink_softmax(s, sink.astype(F32).reshape(N_KV_HEADS, GQA_GROUP))
    return jnp.einsum('bhgqk,bkhd->bqhgd', p, v.astype(F32)).reshape(B, C, D_ATTN).astype(q.dtype)


def token_shift(h, mu_prev, mu_next):
    prev = jnp.pad(h, ((0, 0), (1, 0), (0, 0)))[:, :-1]
    nxt = jnp.pad(h, ((0, 0), (0, 1), (0, 0)))[:, 1:]
    return h + mu_prev * (prev - h) + mu_next * (nxt - h)


def _heads(t):
    return t.reshape(t.shape[0], t.shape[1], N_HEADS_R, RWKV_HEAD)


def rwkv_prepare(z, mu_prev, mu_next, w0, w_up, a0, a_up, k_k, k_a, g_up):
    zs = token_shift(z.astype(F32), mu_prev, mu_next)
    r, k, v, wdf, wdb, adf, adb, gd = _split(zs, RWKV_SPLITS)
    kk = _heads(k * k_k)
    kk = kk * lax.rsqrt(jnp.maximum(jnp.sum(kk * kk, axis=-1, keepdims=True), 1e-12))
    dirs = []
    for d, (wd, ad) in enumerate(((wdf, adf), (wdb, adb))):
        wlog = -jax.nn.softplus(-(w0[d] + jnp.tanh(wd) @ w_up[d])) - 0.5
        decay = jnp.exp(-jnp.exp(wlog))
        a = jax.nn.sigmoid(a0[d] + ad @ a_up[d])
        kd = k * (1 + (a - 1) * k_a)
        dirs.append((_heads(decay), _heads(kd), _heads(a)))
    gate = jax.nn.sigmoid(gd) @ g_up
    return _heads(r), _heads(v), kk, dirs, gate


def wkv_scan(state0, r, w, k, v, kk, a, reverse):
    tm = lambda t: jnp.swapaxes(t, 0, 1)
    xs = (tm(r), tm(w), tm(k), tm(v), tm(-kk), tm(kk * a))

    def step(S, inp):
        r_t, w_t, k_t, v_t, nkk_t, b_t = inp
        sa = jnp.einsum('bhvk,bhk->bhv', S, nkk_t)
        S = S * w_t[:, :, None, :] + sa[..., None] * b_t[:, :, None, :] + v_t[..., None] * k_t[:, :, None, :]
        return S, jnp.einsum('bhvk,bhk->bhv', S, r_t)

    S_final, ys = lax.scan(step, state0, xs, reverse=reverse)
    return S_final, jnp.swapaxes(ys, 0, 1)


def rwkv_output(y, r, v, dirs, gate, r_k, lnx_g, lnx_b):
    B, T = y.shape[0], y.shape[1]
    mean = jnp.mean(y, axis=-1, keepdims=True)
    var = jnp.mean(jnp.square(y - mean), axis=-1, keepdims=True)
    yn = ((y - mean) * lax.rsqrt(var + LNX_EPS)).reshape(B, T, D_RWKV) * lnx_g + lnx_b
    rk = r_k.reshape(N_HEADS_R, RWKV_HEAD)
    bonus = sum(jnp.sum(r * kd * rk, axis=-1, keepdims=True) * v for _, kd, _ in dirs)
    return (yn + bonus.reshape(B, T, D_RWKV)) * gate


def hybrid_layer(x, xc, c, c_ctx, row, col, ada_w, ada_b, norm1_g, norm2_g, w_in, attn_sink,
                 ts_prev, ts_next, w0, w_up, a0, a_up, k_k, k_a, r_k, g_up, lnx_g, lnx_b,
                 w_out, ffn_w1, ffn_w3, ffn_w2, update_ctx):
    B, T = x.shape[0], x.shape[1]
    C = xc.shape[1]
    mod = jax.nn.silu(c) @ ada_w + ada_b
    mod_c = jax.nn.silu(c_ctx) @ ada_w + ada_b
    sh1, sc1, g1, sh2, sc2, g2 = jnp.split(mod[:, None, :], 6, axis=-1)
    sh1c, sc1c, g1c, sh2c, sc2c, g2c = jnp.split(mod_c, 6, axis=-1)

    z = modulate(rmsnorm(x, norm1_g), sh1, sc1) @ w_in
    zc = modulate(rmsnorm(xc, norm1_g), sh1c, sc1c) @ w_in
    za, zr = z[..., :D_IN_ATTN], z[..., D_IN_ATTN:]
    zca, zcr = zc[..., :D_IN_ATTN], zc[..., D_IN_ATTN:]

    q, k, v = _split(za, ATTN_SPLITS)
    q = axial_rope(q.reshape(B, T, N_HEADS_A, HEAD_DIM), row, col)
    k = axial_rope(k.reshape(B, T, N_KV_HEADS, HEAD_DIM), row, col)
    v = v.reshape(B, T, N_KV_HEADS, HEAD_DIM)
    qc, kc, vc = _split(zca, ATTN_SPLITS)
    kc = kc.reshape(B, C, N_KV_HEADS, HEAD_DIM)
    vc = vc.reshape(B, C, N_KV_HEADS, HEAD_DIM)
    attn = windowed_gqa(q, k, v, kc, vc, attn_sink)

    rw_args = (ts_prev, ts_next, w0, w_up, a0, a_up, k_k, k_a, g_up)
    r, vr, kk, dirs, gate = rwkv_prepare(zr, *rw_args)
    rc, vrc, kkc, dirsc, gatec = rwkv_prepare(zcr, *rw_args)
    S0 = jnp.zeros((B, N_HEADS_R, RWKV_HEAD, RWKV_HEAD), F32)
    y_lat = jnp.zeros_like(r)
    yc_dirs = []
    for d in range(2):
        rev = d == 1
        wd, kd, ad = dirs[d]
        wdc, kdc, adc = dirsc[d]
        Sc, yc_d = wkv_scan(S0, rc, wdc, kdc, vrc, kkc, adc, rev)
        _, y_d = wkv_scan(Sc, r, wd, kd, vr, kk, ad, rev)
        y_lat = y_lat + y_d
        yc_dirs.append(yc_d)
    rw = rwkv_output(y_lat, r, vr, dirs, gate, r_k, lnx_g, lnx_b).astype(x.dtype)

    x = x + g1 * (jnp.concatenate([attn, rw], axis=-1) @ w_out)
    x = x + g2 * swiglu(modulate(rmsnorm(x, norm2_g), sh2, sc2), ffn_w1, ffn_w3, ffn_w2)

    if update_ctx:
        attn_c = context_gqa(qc.reshape(B, C, N_HEADS_A, HEAD_DIM), kc, vc, attn_sink)
        rw_c = rwkv_output(yc_dirs[0] + yc_dirs[1], rc, vrc, dirsc, gatec, r_k, lnx_g, lnx_b).astype(xc.dtype)
        xc = xc + g1c * (jnp.concatenate([attn_c, rw_c], axis=-1) @ w_out)
        xc = xc + g2c * swiglu(modulate(rmsnorm(xc, norm2_g), sh2c, sc2c), ffn_w1, ffn_w3, ffn_w2)
    return x, xc


def setup_inputs(seed: int = 0) -> dict:
    key = jax.random.key(seed)
    ks = jax.random.split(key, 27)
    nrm = lambda k, shape, s: jax.random.normal(k, shape, jnp.float32) * s
    uni = lambda k, shape, lo, hi: jax.random.uniform(k, shape, jnp.float32, lo, hi)
    L = DEPTH
    return {
        'x': nrm(ks[0], (BATCH, SEQ, D_MODEL), 1.0),
        'c': nrm(ks[1], (BATCH, D_MODEL), 1.0),
        'ctx': nrm(ks[2], (BATCH, CTX_LEN, D_MODEL), 1.0),
        'c_ctx': nrm(ks[3], (D_MODEL,), 1.0),
        'ada_w': nrm(ks[4], (L, D_MODEL, 6 * D_MODEL), 0.5 * D_MODEL ** -0.5),
        'ada_b': nrm(ks[5], (L, 6 * D_MODEL), 0.02),
        'norm1_g': 1.0 + nrm(ks[6], (L, D_MODEL), 0.05),
        'norm2_g': 1.0 + nrm(ks[7], (L, D_MODEL), 0.05),
        'w_in': nrm(ks[8], (L, D_MODEL, D_IN), D_MODEL ** -0.5),
        'attn_sink': nrm(ks[9], (L, N_HEADS_A), 0.5),
        'ts_prev': uni(ks[10], (L, D_IN_RWKV), 0.0, 0.5),
        'ts_next': uni(ks[11], (L, D_IN_RWKV), 0.0, 0.5),
        'w0': uni(ks[12], (L, 2, D_RWKV), -4.0, 1.0),
        'w_up': nrm(ks[13], (L, 2, D_DECAY_LORA, D_RWKV), 0.1),
        'a0': nrm(ks[14], (L, 2, D_RWKV), 0.5),
        'a_up': nrm(ks[15], (L, 2, D_AAA_LORA, D_RWKV), 0.1),
        'k_k': 0.85 + nrm(ks[16], (L, D_RWKV), 0.05),
        'k_a': 1.0 + nrm(ks[17], (L, D_RWKV), 0.05),
        'r_k': nrm(ks[18], (L, D_RWKV), 0.1),
        'g_up': nrm(ks[19], (L, D_GATE_LORA, D_RWKV), D_GATE_LORA ** -0.5),
        'lnx_g': 1.0 + nrm(ks[20], (L, D_RWKV), 0.05),
        'lnx_b': nrm(ks[21], (L, D_RWKV), 0.02),
        'w_out': nrm(ks[22], (L, D_MIX, D_MODEL), D_MIX ** -0.5),
        'ffn_w1': nrm(ks[23], (L, D_MODEL, D_FF), D_MODEL ** -0.5),
        'ffn_w3': nrm(ks[24], (L, D_MODEL, D_FF), D_MODEL ** -0.5),
        'ffn_w2': nrm(ks[25], (L, D_FF, D_MODEL), D_FF ** -0.5),
        'final_norm_g': 1.0 + nrm(ks[26], (D_MODEL,), 0.05),
    }


def reference(x, c, ctx, c_ctx, ada_w, ada_b, norm1_g, norm2_g, w_in, attn_sink, ts_prev, ts_next,
              w0, w_up, a0, a_up, k_k, k_a, r_k, g_up, lnx_g, lnx_b, w_out, ffn_w1, ffn_w3, ffn_w2,
              final_norm_g):
    seq_len = x.shape[1]
    rows = seq_len // GRID_W
    row = jnp.repeat(jnp.arange(rows, dtype=jnp.int32), GRID_W)
    col = jnp.tile(jnp.arange(GRID_W, dtype=jnp.int32), rows)
    xc = ctx
    for i in range(DEPTH):
        x, xc = hybrid_layer(x, xc, c, c_ctx, row, col, ada_w[i], ada_b[i], norm1_g[i], norm2_g[i],
                             w_in[i], attn_sink[i], ts_prev[i], ts_next[i], w0[i], w_up[i], a0[i],
                             a_up[i], k_k[i], k_a[i], r_k[i], g_up[i], lnx_g[i], lnx_b[i], w_out[i],
                             ffn_w1[i], ffn_w3[i], ffn_w2[i], update_ctx=(i < DEPTH - 1))
    return rmsnorm(x, final_norm_g)
```

```python
import functools

import numpy as np
import jax
import jax.numpy as jnp
from jax import lax
from jax.experimental import pallas as pl
from jax.experimental.pallas import tpu as pltpu

F32 = jnp.float32
BF16 = jnp.bfloat16
HI = lax.Precision.HIGHEST

D_MODEL = 2048
HEAD_DIM = 64
N_HEADS_A = 16
N_KV_HEADS = 4
GQA_GROUP = 4
D_ATTN = 1024
D_RWKV = 1024
D_LORA = 64
D_GATE_LORA = 160
D_FF = 5632
GRID_W = 64
WINDOW = 128
BLOCK = 128
ROPE_BASE = 10000.0
NORM_EPS = 1e-6
LNX_EPS = 64e-5
MASK_VALUE = -1e30

LANES = 128
ROWS = 256
CHUNK = 64
VMEM_LIMIT = 56 * 1024 * 1024

Z_R, Z_K, Z_V, Z_LORA = 0, 1024, 2048, 3072
Z_RWKV_W = 3584
Z_Q, Z_KD, Z_VD = 3584, 4608, 5120
NZ = 5632
PROJ_TN = 512


def _cparams(sem):
    return pltpu.CompilerParams(dimension_semantics=sem, vmem_limit_bytes=VMEM_LIMIT)


def _sigmoid(x):
    return 1.0 / (1.0 + jnp.exp(-x))


def _dot(a, b, precision=None):
    return jnp.dot(a, b, preferred_element_type=F32, precision=precision)


def _dot_nt(a, b, precision=None):
    return lax.dot_general(a, b, (((1,), (1,)), ((), ())), preferred_element_type=F32, precision=precision)


def _adaln_kernel(c_ref, w_ref, b_ref, o_ref):
    c = c_ref[...]
    s = c * _sigmoid(c)
    o_ref[...] = _dot(s, w_ref[...], HI) + b_ref[...]


def _adaln(cc, w, b):
    n = w.shape[1]
    tn = 1024
    return pl.pallas_call(
        _adaln_kernel,
        grid=(n // tn,),
        in_specs=[
            pl.BlockSpec((8, D_MODEL), lambda j: (0, 0)),
            pl.BlockSpec((D_MODEL, tn), lambda j: (0, j)),
            pl.BlockSpec((1, tn), lambda j: (0, j)),
        ],
        out_specs=pl.BlockSpec((8, tn), lambda j: (0, j)),
        out_shape=jax.ShapeDtypeStruct((8, n), F32),
        compiler_params=_cparams(("parallel",)),
        name="adaln",
    )(cc, w, b.reshape(1, n))


def _inproj_kernel(x_ref, g_ref, sh_ref, sc_ref, cos_ref, sin_ref, w_ref, z_ref, *, nblk):
    i = pl.program_id(0)
    x = x_ref[...]
    ms = jnp.mean(x * x, axis=-1, keepdims=True)
    h = x * lax.rsqrt(ms + NORM_EPS) * g_ref[...]
    h = h * (1.0 + sc_ref[0]) + sh_ref[0]
    hb = h.astype(BF16)
    for j in range(NZ // PROJ_TN):
        sl = slice(j * PROJ_TN, (j + 1) * PROJ_TN)
        z_ref[:, sl] = _dot(hb, w_ref[:, sl])

    @pl.when(i % nblk != 0)
    def _():
        cs = cos_ref[...]
        sn = sin_ref[...]
        lane = lax.broadcasted_iota(jnp.int32, (ROWS, LANES), 1)
        first = (lane % 32) < 16
        for s in range((Z_VD - Z_Q) // LANES):
            sl = slice(Z_Q + s * LANES, Z_Q + (s + 1) * LANES)
            zz = z_ref[:, sl]
            partner = jnp.where(first, pltpu.roll(zz, LANES - 16, 1), pltpu.roll(zz, 16, 1))
            z_ref[:, sl] = zz * cs + partner * sn


def _inproj(xall, g, sh, sc, cos_t, sin_t, wz, nbatch, nblk):
    n = xall.shape[0]

    def sel(i):
        return jnp.where(i % nblk == 0, nbatch, i // nblk)

    return pl.pallas_call(
        functools.partial(_inproj_kernel, nblk=nblk),
        grid=(n // ROWS,),
        in_specs=[
            pl.BlockSpec((ROWS, D_MODEL), lambda i: (i, 0)),
            pl.BlockSpec((1, D_MODEL), lambda i: (0, 0)),
            pl.BlockSpec((1, 1, D_MODEL), lambda i: (sel(i), 0, 0)),
            pl.BlockSpec((1, 1, D_MODEL), lambda i: (sel(i), 0, 0)),
            pl.BlockSpec((ROWS, LANES), lambda i: (jnp.maximum(i % nblk - 1, 0), 0)),
            pl.BlockSpec((ROWS, LANES), lambda i: (jnp.maximum(i % nblk - 1, 0), 0)),
            pl.BlockSpec((D_MODEL, NZ), lambda i: (0, 0), pipeline_mode=pl.Buffered(1)),
        ],
        out_specs=pl.BlockSpec((ROWS, NZ), lambda i: (i, 0)),
        out_shape=jax.ShapeDtypeStruct((n, NZ), F32),
        compiler_params=_cparams(("parallel",)),
        name="inproj",
    )(xall, g, sh, sc, cos_t, sin_t, wz)


def _attn_kernel(sink_ref, q_ref, kp_ref, kc_ref, kn_ref, vp_ref, vc_ref, vn_ref, kx_ref, vx_ref, o_ref, *, nb):
    i = pl.program_id(1)
    h = pl.program_id(2)
    nctx = kx_ref.shape[0]
    nk = 3 * BLOCK + nctx
    lane = lax.broadcasted_iota(jnp.int32, (BLOCK, LANES), 1)
    lo = lane < HEAD_DIM

    q = q_ref[...]
    parts = []
    for g in range(GQA_GROUP):
        qp = q[:, (g // 2) * LANES:(g // 2 + 1) * LANES]
        parts.append(jnp.where(lo, qp, 0.0) if g % 2 == 0 else jnp.where(lo, 0.0, qp))
    qs = jnp.concatenate(parts, axis=0).astype(BF16)
    ks = jnp.concatenate([kp_ref[...], kc_ref[...], kn_ref[...], kx_ref[...]], axis=0).astype(BF16)
    s = _dot_nt(qs, ks)

    row = lax.broadcasted_iota(jnp.int32, (GQA_GROUP * BLOCK, nk), 0)
    col = lax.broadcasted_iota(jnp.int32, (GQA_GROUP * BLOCK, nk), 1)
    band_pos = col - BLOCK
    diff = band_pos - (row % BLOCK)
    in_band = jnp.where(diff >= -WINDOW, jnp.where(diff <= WINDOW, 1, 0), 0)
    in_seq = jnp.where(band_pos >= -i * BLOCK, jnp.where(band_pos < (nb - i) * BLOCK, 1, 0), 0)
    valid = jnp.where(col >= 3 * BLOCK, 1, in_band * in_seq)
    s = jnp.where(valid > 0, s, MASK_VALUE)

    rowc = lax.broadcasted_iota(jnp.int32, (GQA_GROUP * BLOCK, 1), 0)
    sk = jnp.where(rowc < BLOCK, sink_ref[h * GQA_GROUP],
                   jnp.where(rowc < 2 * BLOCK, sink_ref[h * GQA_GROUP + 1],
                             jnp.where(rowc < 3 * BLOCK, sink_ref[h * GQA_GROUP + 2], sink_ref[h * GQA_GROUP + 3])))
    m = jnp.maximum(jnp.max(s, axis=-1, keepdims=True), sk)
    p = jnp.exp(s - m)
    den = jnp.sum(p, axis=-1, keepdims=True) + jnp.exp(sk - m)
    vs = jnp.concatenate([vp_ref[...], vc_ref[...], vn_ref[...], vx_ref[...]], axis=0).astype(BF16)
    o = _dot(p.astype(BF16), vs) / den
    out = jnp.concatenate([jnp.where(lo, o[0:BLOCK], o[BLOCK:2 * BLOCK]),
                           jnp.where(lo, o[2 * BLOCK:3 * BLOCK], o[3 * BLOCK:4 * BLOCK])], axis=1)
    o_ref[...] = out.astype(o_ref.dtype)


def _attention(z, sink, nbatch, t, nctx):
    nb = t // BLOCK
    rpb = (nctx + t) // BLOCK
    cpb = nctx // BLOCK
    kcol = Z_KD // LANES
    vcol = Z_VD // LANES

    def lat(off):
        def f(b, i, h):
            return b * rpb + cpb + jnp.clip(i + off, 0, nb - 1)
        return f

    def kspec(off, col0):
        return pl.BlockSpec((BLOCK, LANES), lambda b, i, h: (lat(off)(b, i, h), col0 + h))

    def xspec(col0):
        return pl.BlockSpec((nctx, LANES), lambda b, i, h: (b * (rpb * BLOCK // nctx), col0 + h))

    return pl.pallas_call(
        functools.partial(_attn_kernel, nb=nb),
        grid=(nbatch, nb, N_KV_HEADS),
        in_specs=[
            pl.BlockSpec(memory_space=pltpu.SMEM),
            pl.BlockSpec((BLOCK, 2 * LANES), lambda b, i, h: (lat(0)(b, i, h), Z_Q // (2 * LANES) + h)),
            kspec(-1, kcol), kspec(0, kcol), kspec(1, kcol),
            kspec(-1, vcol), kspec(0, vcol), kspec(1, vcol),
            xspec(kcol), xspec(vcol),
        ],
        out_specs=pl.BlockSpec((BLOCK, 2 * LANES), lambda b, i, h: (b * nb + i, h)),
        out_shape=jax.ShapeDtypeStruct((nbatch * t, D_ATTN), BF16),
        compiler_params=_cparams(("parallel", "parallel", "parallel")),
        name="attention",
    )(sink, z, z, z, z, z, z, z, z, z)


def _head_ones():
    r = lax.broadcasted_iota(jnp.int32, (LANES, LANES), 0)
    c = lax.broadcasted_iota(jnp.int32, (LANES, LANES), 1)
    return jnp.where((r // HEAD_DIM) == (c // HEAD_DIM), 1.0, 0.0).astype(F32)


def _prep_kernel(zm_ref, zp_ref, zn_ref, mup_ref, mun_ref, w0_ref, wupf_ref, wupb_ref, a0_ref, aupf_ref, aupb_ref,
                 kk_ref, ka_ref, gup_ref,
                 r_o, v_o, lwf_o, lwb_o, kdf_o, kdb_o, nkk_o, bf_o, bb_o, gate_o, *, nblk):
    li = pl.program_id(0) % nblk
    keep_p = jnp.where((li == 0) | (li == 1), 0.0, 1.0)
    keep_n = jnp.where((li == 0) | (li == nblk - 1), 0.0, 1.0)
    rowi = lax.broadcasted_iota(jnp.int32, (ROWS, LANES), 0)

    def shifted(c0):
        sl = slice(c0, c0 + LANES)
        zc = zm_ref[:, sl]
        prev = jnp.where(rowi == 0, zp_ref[7:8, sl] * keep_p, pltpu.roll(zc, 1, 0))
        nxt = jnp.where(rowi == ROWS - 1, zn_ref[0:1, sl] * keep_n, pltpu.roll(zc, ROWS - 1, 0))
        return zc + mup_ref[:, sl] * (prev - zc) + mun_ref[:, sl] * (nxt - zc)

    ones = _head_ones()
    th_w = jnp.tanh(shifted(Z_LORA))
    ad = shifted(Z_LORA + LANES)
    sg = jnp.concatenate([_sigmoid(shifted(Z_LORA + 2 * LANES)), _sigmoid(shifted(Z_LORA + 3 * LANES))], axis=1)

    def log_decay(pre):
        wlog = -(jnp.maximum(-pre, 0.0) + jnp.log(1.0 + jnp.exp(-jnp.abs(pre)))) - 0.5
        return -jnp.exp(wlog)

    for s in range(D_RWKV // LANES):
        sl = slice(s * LANES, (s + 1) * LANES)
        r = shifted(Z_R + s * LANES)
        k = shifted(Z_K + s * LANES)
        v = shifted(Z_V + s * LANES)
        r_o[:, sl] = r
        v_o[:, sl] = v
        kk = k * kk_ref[:, sl]
        ss = _dot(kk * kk, ones, HI)
        kk = kk * lax.rsqrt(jnp.maximum(ss, 1e-12))
        nkk_o[:, sl] = -kk
        lwf_o[:, sl] = log_decay(w0_ref[0:1, sl] + _dot(th_w, wupf_ref[:, sl], HI))
        lwb_o[:, sl] = log_decay(w0_ref[1:2, sl] + _dot(th_w, wupb_ref[:, sl], HI))
        af = _sigmoid(a0_ref[0:1, sl] + _dot(ad, aupf_ref[:, sl], HI))
        ab = _sigmoid(a0_ref[1:2, sl] + _dot(ad, aupb_ref[:, sl], HI))
        ka = ka_ref[:, sl]
        kdf_o[:, sl] = k * (1.0 + (af - 1.0) * ka)
        kdb_o[:, sl] = k * (1.0 + (ab - 1.0) * ka)
        bf_o[:, sl] = kk * af
        bb_o[:, sl] = kk * ab
        gate_o[:, sl] = _dot(sg, gup_ref[:, sl], HI)


def _rwkv_prep(z, mup, mun, w0, wupf, wupb, a0, aupf, aupb, k_k, k_a, gup, nblk):
    n = z.shape[0]
    nsub = ROWS // 8
    full = lambda a: pl.BlockSpec(a.shape, lambda i: (0,) * a.ndim)
    out = jax.ShapeDtypeStruct((n, D_RWKV), F32)
    ospec = pl.BlockSpec((ROWS, D_RWKV), lambda i: (i, 0))
    return pl.pallas_call(
        functools.partial(_prep_kernel, nblk=nblk),
        grid=(n // ROWS,),
        in_specs=[
            pl.BlockSpec((ROWS, Z_RWKV_W), lambda i: (i, 0)),
            pl.BlockSpec((8, Z_RWKV_W), lambda i: (jnp.maximum(i * nsub - 1, 0), 0)),
            pl.BlockSpec((8, Z_RWKV_W), lambda i: (jnp.minimum((i + 1) * nsub, n // 8 - 1), 0)),
            full(mup), full(mun), full(w0), full(wupf), full(wupb), full(a0), full(aupf), full(aupb),
            full(k_k), full(k_a), full(gup),
        ],
        out_specs=[ospec] * 10,
        out_shape=[out] * 10,
        compiler_params=_cparams(("parallel",)),
        name="rwkv_prep",
    )(z, z, z, mup, mun, w0, wupf, wupb, a0, aupf, aupb, k_k, k_a, gup)


def _wkv_chunk(s_ref, r, lw, k, v, a, b, incl, incl_bd, strict_bd, eye_bd, head_a):
    n = CHUNK
    c = _dot(incl, lw, HI)
    ctot = jnp.sum(lw, axis=0, keepdims=True)
    mid = 0.5 * ctot
    e_in = jnp.exp(c - mid)
    e_out = jnp.exp(mid - c)
    e_abs = jnp.exp(c)
    e_rest = jnp.exp(ctot - c)
    undo = jnp.exp(-lw)
    s0 = s_ref[...]

    xs = _dot_nt(jnp.concatenate([a * e_abs * undo, r * e_abs], axis=0), s0, HI)
    xa, xr = xs[:n], xs[n:]

    def stack(x):
        return jnp.concatenate([jnp.where(head_a, x, 0.0), jnp.where(head_a, 0.0, x)], axis=0)

    lhs = jnp.concatenate([stack(a * e_in * undo), stack(r * e_in)], axis=0)
    bc = b * e_out
    kc = k * e_out
    g_b = _dot_nt(lhs, jnp.concatenate([bc, bc], axis=0), HI)
    g_k = _dot_nt(lhs, jnp.concatenate([kc, kc], axis=0), HI)
    a_ab = jnp.where(strict_bd, g_b[:2 * n], 0.0)
    a_ak = jnp.where(strict_bd, g_k[:2 * n], 0.0)
    a_rb = jnp.where(incl_bd, g_b[2 * n:], 0.0)
    a_rk = jnp.where(incl_bd, g_k[2 * n:], 0.0)

    inv = eye_bd + a_ab
    pw = a_ab
    for _ in range(int(np.log2(n)) - 1):
        pw = _dot(pw, pw, HI)
        inv = inv + _dot(inv, pw, HI)

    vst = stack(v)
    ust = _dot(inv, stack(xa) + _dot(a_ak, vst, HI), HI)
    yst = stack(xr) + _dot(a_rb, ust, HI) + _dot(a_rk, vst, HI)
    s_ref[...] = (s0 * jnp.exp(ctot)
                  + _dot(ust.T, stack(b * e_rest), HI) + _dot(vst.T, stack(k * e_rest), HI))
    return yst[:n] + yst[n:]


def _wkv_kernel(rf, vf, af, lwf, kf, bf, rb, vb, ab, lwb, kb, bb, yf_o, yb_o, sf_ref, sb_ref):
    @pl.when(pl.program_id(1) == 0)
    def _():
        sf_ref[...] = jnp.zeros_like(sf_ref)
        sb_ref[...] = jnp.zeros_like(sb_ref)

    n = CHUNK
    ti = lax.broadcasted_iota(jnp.int32, (n, n), 0)
    si = lax.broadcasted_iota(jnp.int32, (n, n), 1)
    incl_f = jnp.where(si <= ti, 1.0, 0.0).astype(F32)
    incl_b = jnp.where(si >= ti, 1.0, 0.0).astype(F32)
    t2 = lax.broadcasted_iota(jnp.int32, (2 * n, 2 * n), 0)
    s2 = lax.broadcasted_iota(jnp.int32, (2 * n, 2 * n), 1)
    same = (t2 // n) == (s2 // n)
    eye_bd = jnp.where(t2 == s2, 1.0, 0.0).astype(F32)
    incl_f_bd = same & (s2 <= t2)
    strict_f_bd = same & (s2 < t2)
    incl_b_bd = same & (s2 >= t2)
    strict_b_bd = same & (s2 > t2)
    head_a = lax.broadcasted_iota(jnp.int32, (n, LANES), 1) < HEAD_DIM
    nsub = ROWS // n

    def body(u, carry):
        of = pl.multiple_of(u * n, n)
        ob = pl.multiple_of((nsub - 1 - u) * n, n)
        fs = pl.ds(of, n)
        bs = pl.ds(ob, n)
        yf_o[fs, :] = _wkv_chunk(sf_ref, rf[fs, :], lwf[fs, :], kf[fs, :], vf[fs, :], af[fs, :], bf[fs, :],
                                 incl_f, incl_f_bd, strict_f_bd, eye_bd, head_a)
        yb_o[bs, :] = _wkv_chunk(sb_ref, rb[bs, :], lwb[bs, :], kb[bs, :], vb[bs, :], ab[bs, :], bb[bs, :],
                                 incl_b, incl_b_bd, strict_b_bd, eye_bd, head_a)
        return carry

    lax.fori_loop(0, nsub, body, 0)


def _wkv(r, v, nkk, lwf, kdf, bfw, lwb, kdb, bbw, nbatch, nblk):
    n = r.shape[0]
    npair = D_RWKV // LANES

    def fwd(g, j):
        return ((g // npair) * nblk + j, g % npair)

    def bwd(g, j):
        return ((g // npair) * nblk + jnp.where(j == 0, 0, nblk - j), g % npair)

    fspec = pl.BlockSpec((ROWS, LANES), fwd)
    bspec = pl.BlockSpec((ROWS, LANES), bwd)
    out = jax.ShapeDtypeStruct((n, D_RWKV), F32)
    return pl.pallas_call(
        _wkv_kernel,
        grid=(nbatch * npair, nblk),
        in_specs=[fspec] * 6 + [bspec] * 6,
        out_specs=[fspec, bspec],
        out_shape=[out, out],
        scratch_shapes=[pltpu.VMEM((LANES, LANES), F32), pltpu.VMEM((LANES, LANES), F32)],
        compiler_params=_cparams(("parallel", "arbitrary")),
        name="wkv",
    )(r, v, nkk, lwf, kdf, bfw, r, v, nkk, lwb, kdb, bbw)


def _rwkv_out_kernel(yf_ref, yb_ref, r_ref, v_ref, kdf_ref, kdb_ref, gate_ref, rk_ref, lg_ref, lb_ref, o_ref):
    mean_mat = _head_ones() * (1.0 / HEAD_DIM)
    ones = _head_ones()
    for s in range(D_RWKV // LANES):
        sl = slice(s * LANES, (s + 1) * LANES)
        y = yf_ref[:, sl] + yb_ref[:, sl]
        mean = _dot(y, mean_mat, HI)
        yc = y - mean
        var = _dot(yc * yc, mean_mat, HI)
        yn = yc * lax.rsqrt(var + LNX_EPS) * lg_ref[:, sl] + lb_ref[:, sl]
        r = r_ref[:, sl]
        rk = rk_ref[:, sl]
        bonus = (_dot(r * kdf_ref[:, sl] * rk, ones, HI) + _dot(r * kdb_ref[:, sl] * rk, ones, HI)) * v_ref[:, sl]
        o_ref[:, sl] = ((yn + bonus) * gate_ref[:, sl]).astype(o_ref.dtype)


def _rwkv_out(yf, yb, r, v, kdf, kdb, gate, r_k, lnx_g, lnx_b, nbatch, t, nblk):
    lpb = t // ROWS
    ispec = pl.BlockSpec((ROWS, D_RWKV), lambda i: ((i // lpb) * nblk + 1 + i % lpb, 0))
    pspec = pl.BlockSpec((1, D_RWKV), lambda i: (0, 0))
    return pl.pallas_call(
        _rwkv_out_kernel,
        grid=(nbatch * lpb,),
        in_specs=[ispec] * 7 + [pspec] * 3,
        out_specs=pl.BlockSpec((ROWS, D_RWKV), lambda i: (i, 0)),
        out_shape=jax.ShapeDtypeStruct((nbatch * t, D_RWKV), BF16),
        compiler_params=_cparams(("parallel",)),
        name="rwkv_out",
    )(yf, yb, r, v, kdf, kdb, gate, r_k, lnx_g, lnx_b)


def _outproj_kernel(x_ref, attn_ref, rw_ref, g1_ref, wa_ref, wr_ref, o_ref):
    mix = _dot(attn_ref[...], wa_ref[...]) + _dot(rw_ref[...], wr_ref[...])
    o_ref[...] = x_ref[...] + g1_ref[0] * mix


def _outproj(x2, attn, rw, g1, wa, wr, t):
    n = x2.shape[0]
    tm = 512
    return pl.pallas_call(
        _outproj_kernel,
        grid=(n // tm,),
        in_specs=[
            pl.BlockSpec((tm, D_MODEL), lambda i: (i, 0)),
            pl.BlockSpec((tm, D_ATTN), lambda i: (i, 0)),
            pl.BlockSpec((tm, D_RWKV), lambda i: (i, 0)),
            pl.BlockSpec((1, 1, D_MODEL), lambda i: (i // (t // tm), 0, 0)),
            pl.BlockSpec((D_ATTN, D_MODEL), lambda i: (0, 0), pipeline_mode=pl.Buffered(1)),
            pl.BlockSpec((D_RWKV, D_MODEL), lambda i: (0, 0), pipeline_mode=pl.Buffered(1)),
        ],
        out_specs=pl.BlockSpec((tm, D_MODEL), lambda i: (i, 0)),
        out_shape=jax.ShapeDtypeStruct((n, D_MODEL), F32),
        compiler_params=_cparams(("parallel",)),
        name="outproj",
    )(x2, attn, rw, g1, wa, wr)


def _ffn_kernel(x_ref, g_ref, sh_ref, sc_ref, g2_ref, fg_ref, w1_ref, w3_ref, w2_ref, o_ref, h_ref, acc_ref):
    j = pl.program_id(1)

    @pl.when(j == 0)
    def _():
        x = x_ref[...]
        ms = jnp.mean(x * x, axis=-1, keepdims=True)
        h = x * lax.rsqrt(ms + NORM_EPS) * g_ref[...]
        h_ref[...] = (h * (1.0 + sc_ref[0]) + sh_ref[0]).astype(BF16)
        acc_ref[...] = jnp.zeros_like(acc_ref)

    h = h_ref[...]
    a = _dot(h, w1_ref[...])
    b = _dot(h, w3_ref[...])
    u = a * _sigmoid(a) * b
    acc_ref[...] += _dot(u.astype(BF16), w2_ref[...])

    @pl.when(j == pl.num_programs(1) - 1)
    def _():
        y = x_ref[...] + g2_ref[0] * acc_ref[...]
        ms = jnp.mean(y * y, axis=-1, keepdims=True)
        o_ref[...] = y * lax.rsqrt(ms + NORM_EPS) * fg_ref[...]


def _ffn(x1, g, sh, sc, g2, fg, w1, w3, w2, t):
    n = x1.shape[0]
    tm, tf = 512, 512
    modspec = pl.BlockSpec((1, 1, D_MODEL), lambda i, j: (i // (t // tm), 0, 0))
    vecspec = pl.BlockSpec((1, D_MODEL), lambda i, j: (0, 0))
    return pl.pallas_call(
        _ffn_kernel,
        grid=(n // tm, D_FF // tf),
        in_specs=[
            pl.BlockSpec((tm, D_MODEL), lambda i, j: (i, 0)),
            vecspec, modspec, modspec, modspec, vecspec,
            pl.BlockSpec((D_MODEL, tf), lambda i, j: (0, j)),
            pl.BlockSpec((D_MODEL, tf), lambda i, j: (0, j)),
            pl.BlockSpec((tf, D_MODEL), lambda i, j: (j, 0)),
        ],
        out_specs=pl.BlockSpec((tm, D_MODEL), lambda i, j: (i, 0)),
        out_shape=jax.ShapeDtypeStruct((n, D_MODEL), F32),
        scratch_shapes=[pltpu.VMEM((tm, D_MODEL), BF16), pltpu.VMEM((tm, D_MODEL), F32)],
        compiler_params=_cparams(("parallel", "arbitrary")),
        name="ffn",
    )(x1, g, sh, sc, g2, fg, w1, w3, w2)


def _z_weights(w_in):
    q0, k0, v0, r0 = 0, D_ATTN, D_ATTN + 256, D_ATTN + 512
    dup = lambda base: np.concatenate([np.tile(base + h * HEAD_DIM + np.arange(HEAD_DIM), 2) for h in range(N_KV_HEADS)])
    n_rwkv = 3 * D_RWKV + 4 * D_LORA + D_GATE_LORA
    cols = [
        w_in[:, r0:r0 + n_rwkv],
        jnp.zeros((D_MODEL, Z_RWKV_W - n_rwkv), w_in.dtype),
        w_in[:, q0:q0 + D_ATTN] * (HEAD_DIM ** -0.5),
        w_in[:, dup(k0)],
        w_in[:, dup(v0)],
    ]
    return jnp.concatenate(cols, axis=1).astype(BF16)


def _rope_tables(t):
    pos = jnp.arange(t, dtype=jnp.int32)
    row = (pos // GRID_W).astype(F32)
    col = (pos % GRID_W).astype(F32)
    nf = HEAD_DIM // 4
    freqs = 1.0 / (ROPE_BASE ** (jnp.arange(nf, dtype=F32) / nf))
    d = np.arange(LANES) % HEAD_DIM
    use_row = jnp.asarray((d // (HEAD_DIM // 2)) == 0)
    first = jnp.asarray((d % (HEAD_DIM // 2)) < nf)
    f = freqs[np.asarray(d % nf)]
    ang = jnp.where(use_row[None, :], row[:, None], col[:, None]) * f[None, :]
    return jnp.cos(ang), jnp.where(first[None, :], -jnp.sin(ang), jnp.sin(ang))


def _pad_rows(w, top, total):
    return jnp.concatenate([jnp.zeros((top, w.shape[1]), w.dtype), w,
                            jnp.zeros((total - top - w.shape[0], w.shape[1]), w.dtype)], axis=0)


def kernel(x, c, ctx, c_ctx, ada_w, ada_b, norm1_g, norm2_g, w_in, attn_sink, ts_prev, ts_next, w0, w_up, a0, a_up,
           k_k, k_a, r_k, g_up, lnx_g, lnx_b, w_out, ffn_w1, ffn_w3, ffn_w2, final_norm_g):
    nbatch, t, d = x.shape
    nctx = ctx.shape[1]
    assert d == D_MODEL and nctx == ROWS and t % ROWS == 0 and nbatch < 8 and ada_w.shape[0] == 1
    nblk = (nctx + t) // ROWS

    cc = jnp.concatenate([c, c_ctx[None, :], jnp.zeros((8 - nbatch - 1, d), F32)], axis=0)
    mod = _adaln(cc, ada_w[0], ada_b[0])
    mods = mod[:nbatch + 1].reshape(nbatch + 1, 6, 1, d)
    sh1, sc1, g1, sh2, sc2, g2 = (mods[:, i] for i in range(6))

    xall = jnp.concatenate([ctx, x], axis=1).reshape(nbatch * (nctx + t), d)
    cos_t, sin_t = _rope_tables(t)
    z = _inproj(xall, norm1_g, sh1, sc1, cos_t, sin_t, _z_weights(w_in[0]), nbatch, nblk)

    attn = _attention(z, attn_sink[0], nbatch, t, nctx)

    n_ts = ts_prev.shape[1]
    mup = jnp.pad(ts_prev, ((0, 0), (0, Z_RWKV_W - n_ts)))
    mun = jnp.pad(ts_next, ((0, 0), (0, Z_RWKV_W - n_ts)))
    wupf = _pad_rows(w_up[0, 0], 0, LANES)
    wupb = _pad_rows(w_up[0, 1], D_LORA, LANES)
    aupf = _pad_rows(a_up[0, 0], 0, LANES)
    aupb = _pad_rows(a_up[0, 1], D_LORA, LANES)
    gup = _pad_rows(g_up[0], 0, 2 * LANES)
    r, v, lwf, lwb, kdf, kdb, nkk, bfw, bbw, gate = _rwkv_prep(
        z, mup, mun, w0[0], wupf, wupb, a0[0], aupf, aupb, k_k, k_a, gup, nblk)
    yf, yb = _wkv(r, v, nkk, lwf, kdf, bfw, lwb, kdb, bbw, nbatch, nblk)
    rw = _rwkv_out(yf, yb, r, v, kdf, kdb, gate, r_k, lnx_g, lnx_b, nbatch, t, nblk)

    wo = w_out[0].astype(BF16)
    x1 = _outproj(x.reshape(nbatch * t, d), attn, rw, g1, wo[:D_ATTN], wo[D_ATTN:], t)
    out = _ffn(x1, norm2_g, sh2, sc2, g2, final_norm_g.reshape(1, d),
               ffn_w1[0].astype(BF16), ffn_w3[0].astype(BF16), ffn_w2[0].astype(BF16), t)
    return out.reshape(nbatch, t, d)
```

```python
import functools

import numpy as np
import jax
import jax.numpy as jnp
from jax import lax
from jax.experimental import pallas as pl
from jax.experimental.pallas import tpu as pltpu

F32 = jnp.float32
BF16 = jnp.bfloat16
HI = lax.Precision.HIGHEST

D_MODEL = 2048
HEAD_DIM = 64
N_HEADS_A = 16
N_KV_HEADS = 4
GQA_GROUP = 4
D_ATTN = 1024
D_RWKV = 1024
D_LORA = 64
D_GATE_LORA = 160
D_FF = 5632
GRID_W = 64
WINDOW = 128
BLOCK = 128
ROPE_BASE = 10000.0
NORM_EPS = 1e-6
LNX_EPS = 64e-5
MASK_VALUE = -1e30

LANES = 128
ROWS = 256
CHUNK = 64
VMEM_LIMIT = 56 * 1024 * 1024

Z_R, Z_K, Z_V, Z_LORA = 0, 1024, 2048, 3072
Z_RWKV_W = 3584
Z_Q, Z_KD, Z_VD = 3584, 4608, 5120
NZ = 5632
PROJ_TN = 512


def _cparams(sem):
    return pltpu.CompilerParams(dimension_semantics=sem, vmem_limit_bytes=VMEM_LIMIT)


def _sigmoid(x):
    return 1.0 / (1.0 + jnp.exp(-x))


def _dot(a, b, precision=None):
    return jnp.dot(a, b, preferred_element_type=F32, precision=precision)


def _dot_nt(a, b, precision=None):
    return lax.dot_general(a, b, (((1,), (1,)), ((), ())), preferred_element_type=F32, precision=precision)


def _adaln_kernel(c_ref, w_ref, b_ref, o_ref):
    c = c_ref[...]
    s = c * _sigmoid(c)
    o_ref[...] = _dot(s, w_ref[...], HI) + b_ref[...]


def _adaln(cc, w, b):
    n = w.shape[1]
    tn = 1024
    return pl.pallas_call(
        _adaln_kernel,
        grid=(n // tn,),
        in_specs=[
            pl.BlockSpec((8, D_MODEL), lambda j: (0, 0)),
            pl.BlockSpec((D_MODEL, tn), lambda j: (0, j)),
            pl.BlockSpec((1, tn), lambda j: (0, j)),
        ],
        out_specs=pl.BlockSpec((8, tn), lambda j: (0, j)),
        out_shape=jax.ShapeDtypeStruct((8, n), F32),
        compiler_params=_cparams(("parallel",)),
        name="adaln",
    )(cc, w, b.reshape(1, n))


def _inproj_kernel(x_ref, g_ref, sh_ref, sc_ref, cos_ref, sin_ref, w_ref, z_ref, *, nblk):
    i = pl.program_id(0)
    x = x_ref[...]
    ms = jnp.mean(x * x, axis=-1, keepdims=True)
    h = x * lax.rsqrt(ms + NORM_EPS) * g_ref[...]
    h = h * (1.0 + sc_ref[0]) + sh_ref[0]
    hb = h.astype(BF16)
    for j in range(NZ // PROJ_TN):
        sl = slice(j * PROJ_TN, (j + 1) * PROJ_TN)
        z_ref[:, sl] = _dot(hb, w_ref[:, sl])

    @pl.when(i % nblk != 0)
    def _():
        cs = cos_ref[...]
        sn = sin_ref[...]
        lane = lax.broadcasted_iota(jnp.int32, (ROWS, LANES), 1)
        first = (lane % 32) < 16
        for s in range((Z_VD - Z_Q) // LANES):
            sl = slice(Z_Q + s * LANES, Z_Q + (s + 1) * LANES)
            zz = z_ref[:, sl]
            partner = jnp.where(first, pltpu.roll(zz, LANES - 16, 1), pltpu.roll(zz, 16, 1))
            z_ref[:, sl] = zz * cs + partner * sn


def _inproj(xall, g, sh, sc, cos_t, sin_t, wz, nbatch, nblk):
    n = xall.shape[0]

    def sel(i):
        return jnp.where(i % nblk == 0, nbatch, i // nblk)

    return pl.pallas_call(
        functools.partial(_inproj_kernel, nblk=nblk),
        grid=(n // ROWS,),
        in_specs=[
            pl.BlockSpec((ROWS, D_MODEL), lambda i: (i, 0)),
            pl.BlockSpec((1, D_MODEL), lambda i: (0, 0)),
            pl.BlockSpec((1, 1, D_MODEL), lambda i: (sel(i), 0, 0)),
            pl.BlockSpec((1, 1, D_MODEL), lambda i: (sel(i), 0, 0)),
            pl.BlockSpec((ROWS, LANES), lambda i: (jnp.maximum(i % nblk - 1, 0), 0)),
            pl.BlockSpec((ROWS, LANES), lambda i: (jnp.maximum(i % nblk - 1, 0), 0)),
            pl.BlockSpec((D_MODEL, NZ), lambda i: (0, 0), pipeline_mode=pl.Buffered(1)),
        ],
        out_specs=pl.BlockSpec((ROWS, NZ), lambda i: (i, 0)),
        out_shape=jax.ShapeDtypeStruct((n, NZ), F32),
        compiler_params=_cparams(("parallel",)),
        name="inproj",
    )(xall, g, sh, sc, cos_t, sin_t, wz)


def _attn_kernel(sink_ref, q_ref, kp_ref, kc_ref, kn_ref, vp_ref, vc_ref, vn_ref, kx_ref, vx_ref, o_ref, *, nb):
    i = pl.program_id(1)
    h = pl.program_id(2)
    nctx = kx_ref.shape[0]
    nk = 3 * BLOCK + nctx
    lane = lax.broadcasted_iota(jnp.int32, (BLOCK, LANES), 1)
    lo = lane < HEAD_DIM

    q = q_ref[...]
    parts = []
    for g in range(GQA_GROUP):
        qp = q[:, (g // 2) * LANES:(g // 2 + 1) * LANES]
        parts.append(jnp.where(lo, qp, 0.0) if g % 2 == 0 else jnp.where(lo, 0.0, qp))
    qs = jnp.concatenate(parts, axis=0).astype(BF16)
    ks = jnp.concatenate([kp_ref[...], kc_ref[...], kn_ref[...], kx_ref[...]], axis=0).astype(BF16)
    s = _dot_nt(qs, ks)

    row = lax.broadcasted_iota(jnp.int32, (GQA_GROUP * BLOCK, nk), 0)
    col = lax.broadcasted_iota(jnp.int32, (GQA_GROUP * BLOCK, nk), 1)
    band_pos = col - BLOCK
    diff = band_pos - (row % BLOCK)
    in_band = jnp.where(diff >= -WINDOW, jnp.where(diff <= WINDOW, 1, 0), 0)
    in_seq = jnp.where(band_pos >= -i * BLOCK, jnp.where(band_pos < (nb - i) * BLOCK, 1, 0), 0)
    valid = jnp.where(col >= 3 * BLOCK, 1, in_band * in_seq)
    s = jnp.where(valid > 0, s, MASK_VALUE)

    rowc = lax.broadcasted_iota(jnp.int32, (GQA_GROUP * BLOCK, 1), 0)
    sk = jnp.where(rowc < BLOCK, sink_ref[h * GQA_GROUP],
                   jnp.where(rowc < 2 * BLOCK, sink_ref[h * GQA_GROUP + 1],
                             jnp.where(rowc < 3 * BLOCK, sink_ref[h * GQA_GROUP + 2], sink_ref[h * GQA_GROUP + 3])))
    m = jnp.maximum(jnp.max(s, axis=-1, keepdims=True), sk)
    p = jnp.exp(s - m)
    den = jnp.sum(p, axis=-1, keepdims=True) + jnp.exp(sk - m)
    vs = jnp.concatenate([vp_ref[...], vc_ref[...], vn_ref[...], vx_ref[...]], axis=0).astype(BF16)
    o = _dot(p.astype(BF16), vs) / den
    out = jnp.concatenate([jnp.where(lo, o[0:BLOCK], o[BLOCK:2 * BLOCK]),
                           jnp.where(lo, o[2 * BLOCK:3 * BLOCK], o[3 * BLOCK:4 * BLOCK])], axis=1)
    o_ref[...] = out.astype(o_ref.dtype)


def _attention(z, sink, nbatch, t, nctx):
    nb = t // BLOCK
    rpb = (nctx + t) // BLOCK
    cpb = nctx // BLOCK
    kcol = Z_KD // LANES
    vcol = Z_VD // LANES

    def lat(off):
        def f(b, i, h):
            return b * rpb + cpb + jnp.clip(i + off, 0, nb - 1)
        return f

    def kspec(off, col0):
        return pl.BlockSpec((BLOCK, LANES), lambda b, i, h: (lat(off)(b, i, h), col0 + h))

    def xspec(col0):
        return pl.BlockSpec((nctx, LANES), lambda b, i, h: (b * (rpb * BLOCK // nctx), col0 + h))

    return pl.pallas_call(
        functools.partial(_attn_kernel, nb=nb),
        grid=(nbatch, nb, N_KV_HEADS),
        in_specs=[
            pl.BlockSpec(memory_space=pltpu.SMEM),
            pl.BlockSpec((BLOCK, 2 * LANES), lambda b, i, h: (lat(0)(b, i, h), Z_Q // (2 * LANES) + h)),
            kspec(-1, kcol), kspec(0, kcol), kspec(1, kcol),
            kspec(-1, vcol), kspec(0, vcol), kspec(1, vcol),
            xspec(kcol), xspec(vcol),
        ],
        out_specs=pl.BlockSpec((BLOCK, 2 * LANES), lambda b, i, h: (b * nb + i, h)),
        out_shape=jax.ShapeDtypeStruct((nbatch * t, D_ATTN), BF16),
        compiler_params=_cparams(("parallel", "parallel", "parallel")),
        name="attention",
    )(sink, z, z, z, z, z, z, z, z, z)


def _head_ones():
    r = lax.broadcasted_iota(jnp.int32, (LANES, LANES), 0)
    c = lax.broadcasted_iota(jnp.int32, (LANES, LANES), 1)
    return jnp.where((r // HEAD_DIM) == (c // HEAD_DIM), 1.0, 0.0).astype(F32)


def _prep_kernel(zm_ref, zp_ref, zn_ref, mup_ref, mun_ref, w0_ref, wupf_ref, wupb_ref, a0_ref, aupf_ref, aupb_ref,
                 kk_ref, ka_ref, gup_ref,
                 r_o, v_o, lwf_o, lwb_o, kdf_o, kdb_o, nkk_o, bf_o, bb_o, gate_o, *, nblk):
    li = pl.program_id(0) % nblk
    keep_p = jnp.where((li == 0) | (li == 1), 0.0, 1.0)
    keep_n = jnp.where((li == 0) | (li == nblk - 1), 0.0, 1.0)
    rowi = lax.broadcasted_iota(jnp.int32, (ROWS, LANES), 0)

    def shifted(c0):
        sl = slice(c0, c0 + LANES)
        zc = zm_ref[:, sl]
        prev = jnp.where(rowi == 0, zp_ref[7:8, sl] * keep_p, pltpu.roll(zc, 1, 0))
        nxt = jnp.where(rowi == ROWS - 1, zn_ref[0:1, sl] * keep_n, pltpu.roll(zc, ROWS - 1, 0))
        return zc + mup_ref[:, sl] * (prev - zc) + mun_ref[:, sl] * (nxt - zc)

    ones = _head_ones()
    th_w = jnp.tanh(shifted(Z_LORA))
    ad = shifted(Z_LORA + LANES)
    sg = jnp.concatenate([_sigmoid(shifted(Z_LORA + 2 * LANES)), _sigmoid(shifted(Z_LORA + 3 * LANES))], axis=1)

    def log_decay(pre):
        wlog = -(jnp.maximum(-pre, 0.0) + jnp.log(1.0 + jnp.exp(-jnp.abs(pre)))) - 0.5
        return -jnp.exp(wlog)

    for s in range(D_RWKV // LANES):
        sl = slice(s * LANES, (s + 1) * LANES)
        r = shifted(Z_R + s * LANES)
        k = shifted(Z_K + s * LANES)
        v = shifted(Z_V + s * LANES)
        r_o[:, sl] = r
        v_o[:, sl] = v
        kk = k * kk_ref[:, sl]
        ss = _dot(kk * kk, ones, HI)
        kk = kk * lax.rsqrt(jnp.maximum(ss, 1e-12))
        nkk_o[:, sl] = -kk
        lwf_o[:, sl] = log_decay(w0_ref[0:1, sl] + _dot(th_w, wupf_ref[:, sl], HI))
        lwb_o[:, sl] = log_decay(w0_ref[1:2, sl] + _dot(th_w, wupb_ref[:, sl], HI))
        af = _sigmoid(a0_ref[0:1, sl] + _dot(ad, aupf_ref[:, sl], HI))
        ab = _sigmoid(a0_ref[1:2, sl] + _dot(ad, aupb_ref[:, sl], HI))
        ka = ka_ref[:, sl]
        kdf_o[:, sl] = k * (1.0 + (af - 1.0) * ka)
        kdb_o[:, sl] = k * (1.0 + (ab - 1.0) * ka)
        bf_o[:, sl] = kk * af
        bb_o[:, sl] = kk * ab
        gate_o[:, sl] = _dot(sg, gup_ref[:, sl], HI)


def _rwkv_prep(z, mup, mun, w0, wupf, wupb, a0, aupf, aupb, k_k, k_a, gup, nblk):
    n = z.shape[0]
    nsub = ROWS // 8
    full = lambda a: pl.BlockSpec(a.shape, lambda i: (0,) * a.ndim)
    out = jax.ShapeDtypeStruct((n, D_RWKV), F32)
    ospec = pl.BlockSpec((ROWS, D_RWKV), lambda i: (i, 0))
    return pl.pallas_call(
        functools.partial(_prep_kernel, nblk=nblk),
        grid=(n // ROWS,),
        in_specs=[
            pl.BlockSpec((ROWS, Z_RWKV_W), lambda i: (i, 0)),
            pl.BlockSpec((8, Z_RWKV_W), lambda i: (jnp.maximum(i * nsub - 1, 0), 0)),
            pl.BlockSpec((8, Z_RWKV_W), lambda i: (jnp.minimum((i + 1) * nsub, n // 8 - 1), 0)),
            full(mup), full(mun), full(w0), full(wupf), full(wupb), full(a0), full(aupf), full(aupb),
            full(k_k), full(k_a), full(gup),
        ],
        out_specs=[ospec] * 10,
        out_shape=[out] * 10,
        compiler_params=_cparams(("parallel",)),
        name="rwkv_prep",
    )(z, z, z, mup, mun, w0, wupf, wupb, a0, aupf, aupb, k_k, k_a, gup)


def _split(x, terms):
    out = []
    for _ in range(terms - 1):
        hi = x.astype(BF16)
        out.append(hi)
        x = x - hi.astype(F32)
    out.append(x.astype(BF16))
    return out


def _mm(a, b, passes, nt=False):
    d = _dot_nt if nt else _dot
    if passes == 1:
        return d(a.astype(BF16), b.astype(BF16))
    if passes == 3:
        ah, al = _split(a, 2)
        bh, bl = _split(b, 2)
        return d(ah, bh) + (d(ah, bl) + d(al, bh))
    return d(a, b, HI)


P_GRAM, P_INV, P_SOLVE, P_READ, P_STATE, P_OUT, P_CARRY = 1, 1, 1, 1, 1, 1, 3


def _wkv_intra(probs, bd, eye_bd, head_a, head_a2):
    n = CHUNK
    zero = jnp.zeros((n, LANES), F32)
    each = lambda f, *ls: [f(*xs) for xs in zip(*ls)]

    def stack(x):
        return jnp.concatenate([jnp.where(head_a, x, 0.0), jnp.where(head_a, 0.0, x)], axis=0)

    def cumulative(p):
        inclb = p[6].astype(BF16)
        l0, l1, l2 = _split(p[1], 3)
        return _dot(inclb, l0) + (_dot(inclb, l1) + _dot(inclb, l2))

    cs = each(cumulative, probs)

    def factors(p, c):
        r, lw, k, v, a, b = p[:6]
        ctot = jnp.sum(lw, axis=0, keepdims=True)
        mid = 0.5 * ctot
        e_in = jnp.exp(c - mid)
        e_out = jnp.exp(mid - c)
        e_abs = jnp.exp(c)
        e_rest = jnp.exp(ctot - c)
        undo = jnp.exp(-lw)
        lhs = jnp.concatenate([stack(a * e_in * undo), stack(r * e_in)], axis=0)
        rhs = jnp.concatenate([b * e_out, k * e_out], axis=0)
        rest = jnp.concatenate([b * e_rest, k * e_rest], axis=0)
        return lhs, rhs, rest, a * e_abs * undo, r * e_abs, jnp.exp(ctot)

    fs = each(factors, probs, cs)
    grams = [_mm(f[0], f[1], P_GRAM, nt=True) for f in fs]
    aas = [jnp.where(p[7], g[:2 * n], 0.0) for p, g in zip(probs, grams)]
    ars = [jnp.where(p[8], g[2 * n:], 0.0) for p, g in zip(probs, grams)]
    a_abs = [jnp.where(bd, jnp.concatenate([aa[:n], pltpu.roll(aa[n:], n, 1)], axis=0), 0.0) for aa in aas]

    invs = [eye_bd + x for x in a_abs]
    pws = [_mm(x, x, P_INV) for x in a_abs]
    for _ in range(int(np.log2(n)) - 2):
        boths = [_mm(jnp.concatenate([pw, inv], axis=0), pw, P_INV) for pw, inv in zip(pws, invs)]
        pws = [x[:2 * n] for x in boths]
        invs = [inv + x[2 * n:] for inv, x in zip(invs, boths)]
    invs = [inv + _mm(inv, pw, P_INV) for inv, pw in zip(invs, pws)]

    akvs = [jnp.where(bd, _mm(aa, jnp.concatenate([zero, p[3]], axis=0), P_SOLVE), 0.0)
            for aa, p in zip(aas, probs)]
    wu_sts = [_mm(inv, jnp.concatenate([stack(f[3]), akv], axis=1), P_SOLVE)
              for inv, f, akv in zip(invs, fs, akvs)]
    m2s = [jnp.concatenate([x[:n] + x[n:], jnp.concatenate([zero, p[3]], axis=1)], axis=0)
           for x, p in zip(wu_sts, probs)]
    qy_sts = [_mm(ar, m2, P_READ) for ar, m2 in zip(ars, m2s)]
    ghs = [_mm(m2.T, f[2], P_STATE) for m2, f in zip(m2s, fs)]
    out = []
    for f, qy_st, gh in zip(fs, qy_sts, ghs):
        qy = jnp.where(head_a2, qy_st[:n], qy_st[n:])
        out.append((f[4] + qy[:, :LANES], qy[:, LANES:],
                    jnp.where(bd, gh[:LANES], 0.0) + eye_bd * f[5], jnp.where(bd, gh[LANES:], 0.0)))
    return out


def _wkv_kernel(rf, vf, af, lwf, kf, bf, rb, vb, ab, lwb, kb, bb, yf_o, yb_o,
                sf_ref, sb_ref, q_ref, y0_ref, gm_ref, h_ref):
    @pl.when(pl.program_id(1) == 0)
    def _():
        sf_ref[...] = jnp.zeros_like(sf_ref)
        sb_ref[...] = jnp.zeros_like(sb_ref)

    n = CHUNK
    nsub = ROWS // n
    ti = lax.broadcasted_iota(jnp.int32, (n, n), 0)
    si = lax.broadcasted_iota(jnp.int32, (n, n), 1)
    t2 = lax.broadcasted_iota(jnp.int32, (2 * n, 2 * n), 0)
    s2 = lax.broadcasted_iota(jnp.int32, (2 * n, 2 * n), 1)
    tw, sw = t2 % n, s2 % n
    bd = (t2 // n) == (s2 // n)
    eye_bd = jnp.where(t2 == s2, 1.0, 0.0).astype(F32)
    head_a = lax.broadcasted_iota(jnp.int32, (n, LANES), 1) < HEAD_DIM
    head_a2 = lax.broadcasted_iota(jnp.int32, (n, 2 * LANES), 1) % LANES < HEAD_DIM
    masks = (
        (jnp.where(si <= ti, 1.0, 0.0).astype(F32), sw < tw, sw <= tw),
        (jnp.where(si >= ti, 1.0, 0.0).astype(F32), sw > tw, sw >= tw),
    )
    srcs = ((rf, lwf, kf, vf, af, bf), (rb, lwb, kb, vb, ab, bb))

    probs = [tuple(x[u * n:(u + 1) * n, :] for x in srcs[d]) + masks[d] for d in range(2) for u in range(nsub)]
    for idx, (q, y0, gm, h) in enumerate(_wkv_intra(probs, bd, eye_bd, head_a, head_a2)):
        q_ref[idx] = q
        y0_ref[idx] = y0
        gm_ref[idx] = gm
        h_ref[idx] = h

    for step in range(nsub):
        for d, (s_ref, y_o) in enumerate(((sf_ref, yf_o), (sb_ref, yb_o))):
            u = step if d == 0 else nsub - 1 - step
            idx = d * nsub + u
            s0 = s_ref[...]
            y_o[u * n:(u + 1) * n, :] = _mm(q_ref[idx], s0, P_OUT, nt=True) + y0_ref[idx]
            s_ref[...] = _mm(s0, gm_ref[idx], P_CARRY) + h_ref[idx]


def _wkv(r, v, nkk, lwf, kdf, bfw, lwb, kdb, bbw, nbatch, nblk):
    n = r.shape[0]
    npair = D_RWKV // LANES

    def fwd(g, j):
        return ((g // npair) * nblk + j, g % npair)

    def bwd(g, j):
        return ((g // npair) * nblk + jnp.where(j == 0, 0, nblk - j), g % npair)

    fspec = pl.BlockSpec((ROWS, LANES), fwd)
    bspec = pl.BlockSpec((ROWS, LANES), bwd)
    out = jax.ShapeDtypeStruct((n, D_RWKV), F32)
    return pl.pallas_call(
        _wkv_kernel,
        grid=(nbatch * npair, nblk),
        in_specs=[fspec] * 6 + [bspec] * 6,
        out_specs=[fspec, bspec],
        out_shape=[out, out],
        scratch_shapes=[pltpu.VMEM((LANES, LANES), F32), pltpu.VMEM((LANES, LANES), F32),
                        pltpu.VMEM((2 * ROWS // CHUNK, CHUNK, LANES), F32), pltpu.VMEM((2 * ROWS // CHUNK, CHUNK, LANES), F32),
                        pltpu.VMEM((2 * ROWS // CHUNK, LANES, LANES), F32), pltpu.VMEM((2 * ROWS // CHUNK, LANES, LANES), F32)],
        compiler_params=_cparams(("parallel", "arbitrary")),
        name="wkv",
    )(r, v, nkk, lwf, kdf, bfw, r, v, nkk, lwb, kdb, bbw)


def _rwkv_out_kernel(yf_ref, yb_ref, r_ref, v_ref, kdf_ref, kdb_ref, gate_ref, rk_ref, lg_ref, lb_ref, o_ref):
    mean_mat = _head_ones() * (1.0 / HEAD_DIM)
    ones = _head_ones()
    for s in range(D_RWKV // LANES):
        sl = slice(s * LANES, (s + 1) * LANES)
        y = yf_ref[:, sl] + yb_ref[:, sl]
        mean = _dot(y, mean_mat, HI)
        yc = y - mean
        var = _dot(yc * yc, mean_mat, HI)
        yn = yc * lax.rsqrt(var + LNX_EPS) * lg_ref[:, sl] + lb_ref[:, sl]
        r = r_ref[:, sl]
        rk = rk_ref[:, sl]
        bonus = (_dot(r * kdf_ref[:, sl] * rk, ones, HI) + _dot(r * kdb_ref[:, sl] * rk, ones, HI)) * v_ref[:, sl]
        o_ref[:, sl] = ((yn + bonus) * gate_ref[:, sl]).astype(o_ref.dtype)


def _rwkv_out(yf, yb, r, v, kdf, kdb, gate, r_k, lnx_g, lnx_b, nbatch, t, nblk):
    lpb = t // ROWS
    ispec = pl.BlockSpec((ROWS, D_RWKV), lambda i: ((i // lpb) * nblk + 1 + i % lpb, 0))
    pspec = pl.BlockSpec((1, D_RWKV), lambda i: (0, 0))
    return pl.pallas_call(
        _rwkv_out_kernel,
        grid=(nbatch * lpb,),
        in_specs=[ispec] * 7 + [pspec] * 3,
        out_specs=pl.BlockSpec((ROWS, D_RWKV), lambda i: (i, 0)),
        out_shape=jax.ShapeDtypeStruct((nbatch * t, D_RWKV), BF16),
        compiler_params=_cparams(("parallel",)),
        name="rwkv_out",
    )(yf, yb, r, v, kdf, kdb, gate, r_k, lnx_g, lnx_b)


def _outproj_kernel(x_ref, attn_ref, rw_ref, g1_ref, wa_ref, wr_ref, o_ref):
    mix = _dot(attn_ref[...], wa_ref[...]) + _dot(rw_ref[...], wr_ref[...])
    o_ref[...] = x_ref[...] + g1_ref[0] * mix


def _outproj(x2, attn, rw, g1, wa, wr, t):
    n = x2.shape[0]
    tm = 512
    return pl.pallas_call(
        _outproj_kernel,
        grid=(n // tm,),
        in_specs=[
            pl.BlockSpec((tm, D_MODEL), lambda i: (i, 0)),
            pl.BlockSpec((tm, D_ATTN), lambda i: (i, 0)),
            pl.BlockSpec((tm, D_RWKV), lambda i: (i, 0)),
            pl.BlockSpec((1, 1, D_MODEL), lambda i: (i // (t // tm), 0, 0)),
            pl.BlockSpec((D_ATTN, D_MODEL), lambda i: (0, 0), pipeline_mode=pl.Buffered(1)),
            pl.BlockSpec((D_RWKV, D_MODEL), lambda i: (0, 0), pipeline_mode=pl.Buffered(1)),
        ],
        out_specs=pl.BlockSpec((tm, D_MODEL), lambda i: (i, 0)),
        out_shape=jax.ShapeDtypeStruct((n, D_MODEL), F32),
        compiler_params=_cparams(("parallel",)),
        name="outproj",
    )(x2, attn, rw, g1, wa, wr)


def _ffn_kernel(x_ref, g_ref, sh_ref, sc_ref, g2_ref, fg_ref, w1_ref, w3_ref, w2_ref, o_ref, h_ref, acc_ref):
    j = pl.program_id(1)

    @pl.when(j == 0)
    def _():
        x = x_ref[...]
        ms = jnp.mean(x * x, axis=-1, keepdims=True)
        h = x * lax.rsqrt(ms + NORM_EPS) * g_ref[...]
        h_ref[...] = (h * (1.0 + sc_ref[0]) + sh_ref[0]).astype(BF16)
        acc_ref[...] = jnp.zeros_like(acc_ref)

    h = h_ref[...]
    a = _dot(h, w1_ref[...])
    b = _dot(h, w3_ref[...])
    u = a * _sigmoid(a) * b
    acc_ref[...] += _dot(u.astype(BF16), w2_ref[...])

    @pl.when(j == pl.num_programs(1) - 1)
    def _():
        y = x_ref[...] + g2_ref[0] * acc_ref[...]
        ms = jnp.mean(y * y, axis=-1, keepdims=True)
        o_ref[...] = y * lax.rsqrt(ms + NORM_EPS) * fg_ref[...]


def _ffn(x1, g, sh, sc, g2, fg, w1, w3, w2, t):
    n = x1.shape[0]
    tm, tf = 512, 512
    modspec = pl.BlockSpec((1, 1, D_MODEL), lambda i, j: (i // (t // tm), 0, 0))
    vecspec = pl.BlockSpec((1, D_MODEL), lambda i, j: (0, 0))
    return pl.pallas_call(
        _ffn_kernel,
        grid=(n // tm, D_FF // tf),
        in_specs=[
            pl.BlockSpec((tm, D_MODEL), lambda i, j: (i, 0)),
            vecspec, modspec, modspec, modspec, vecspec,
            pl.BlockSpec((D_MODEL, tf), lambda i, j: (0, j)),
            pl.BlockSpec((D_MODEL, tf), lambda i, j: (0, j)),
            pl.BlockSpec((tf, D_MODEL), lambda i, j: (j, 0)),
        ],
        out_specs=pl.BlockSpec((tm, D_MODEL), lambda i, j: (i, 0)),
        out_shape=jax.ShapeDtypeStruct((n, D_MODEL), F32),
        scratch_shapes=[pltpu.VMEM((tm, D_MODEL), BF16), pltpu.VMEM((tm, D_MODEL), F32)],
        compiler_params=_cparams(("parallel", "arbitrary")),
        name="ffn",
    )(x1, g, sh, sc, g2, fg, w1, w3, w2)


def _z_weights(w_in):
    q0, k0, v0, r0 = 0, D_ATTN, D_ATTN + 256, D_ATTN + 512
    dup = lambda base: np.concatenate([np.tile(base + h * HEAD_DIM + np.arange(HEAD_DIM), 2) for h in range(N_KV_HEADS)])
    n_rwkv = 3 * D_RWKV + 4 * D_LORA + D_GATE_LORA
    cols = [
        w_in[:, r0:r0 + n_rwkv],
        jnp.zeros((D_MODEL, Z_RWKV_W - n_rwkv), w_in.dtype),
        w_in[:, q0:q0 + D_ATTN] * (HEAD_DIM ** -0.5),
        w_in[:, dup(k0)],
        w_in[:, dup(v0)],
    ]
    return jnp.concatenate(cols, axis=1).astype(BF16)


def _rope_tables(t):
    pos = jnp.arange(t, dtype=jnp.int32)
    row = (pos // GRID_W).astype(F32)
    col = (pos % GRID_W).astype(F32)
    nf = HEAD_DIM // 4
    freqs = 1.0 / (ROPE_BASE ** (jnp.arange(nf, dtype=F32) / nf))
    d = np.arange(LANES) % HEAD_DIM
    use_row = jnp.asarray((d // (HEAD_DIM // 2)) == 0)
    first = jnp.asarray((d % (HEAD_DIM // 2)) < nf)
    f = freqs[np.asarray(d % nf)]
    ang = jnp.where(use_row[None, :], row[:, None], col[:, None]) * f[None, :]
    return jnp.cos(ang), jnp.where(first[None, :], -jnp.sin(ang), jnp.sin(ang))


def _pad_rows(w, top, total):
    return jnp.concatenate([jnp.zeros((top, w.shape[1]), w.dtype), w,
                            jnp.zeros((total - top - w.shape[0], w.shape[1]), w.dtype)], axis=0)


def kernel(x, c, ctx, c_ctx, ada_w, ada_b, norm1_g, norm2_g, w_in, attn_sink, ts_prev, ts_next, w0, w_up, a0, a_up,
           k_k, k_a, r_k, g_up, lnx_g, lnx_b, w_out, ffn_w1, ffn_w3, ffn_w2, final_norm_g):
    nbatch, t, d = x.shape
    nctx = ctx.shape[1]
    assert d == D_MODEL and nctx == ROWS and t % ROWS == 0 and nbatch < 8 and ada_w.shape[0] == 1
    nblk = (nctx + t) // ROWS

    cc = jnp.concatenate([c, c_ctx[None, :], jnp.zeros((8 - nbatch - 1, d), F32)], axis=0)
    mod = _adaln(cc, ada_w[0], ada_b[0])
    mods = mod[:nbatch + 1].reshape(nbatch + 1, 6, 1, d)
    sh1, sc1, g1, sh2, sc2, g2 = (mods[:, i] for i in range(6))

    xall = jnp.concatenate([ctx, x], axis=1).reshape(nbatch * (nctx + t), d)
    cos_t, sin_t = _rope_tables(t)
    z = _inproj(xall, norm1_g, sh1, sc1, cos_t, sin_t, _z_weights(w_in[0]), nbatch, nblk)

    attn = _attention(z, attn_sink[0], nbatch, t, nctx)

    n_ts = ts_prev.shape[1]
    mup = jnp.pad(ts_prev, ((0, 0), (0, Z_RWKV_W - n_ts)))
    mun = jnp.pad(ts_next, ((0, 0), (0, Z_RWKV_W - n_ts)))
    wupf = _pad_rows(w_up[0, 0], 0, LANES)
    wupb = _pad_rows(w_up[0, 1], D_LORA, LANES)
    aupf = _pad_rows(a_up[0, 0], 0, LANES)
    aupb = _pad_rows(a_up[0, 1], D_LORA, LANES)
    gup = _pad_rows(g_up[0], 0, 2 * LANES)
    r, v, lwf, lwb, kdf, kdb, nkk, bfw, bbw, gate = _rwkv_prep(
        z, mup, mun, w0[0], wupf, wupb, a0[0], aupf, aupb, k_k, k_a, gup, nblk)
    yf, yb = _wkv(r, v, nkk, lwf, kdf, bfw, lwb, kdb, bbw, nbatch, nblk)
    rw = _rwkv_out(yf, yb, r, v, kdf, kdb, gate, r_k, lnx_g, lnx_b, nbatch, t, nblk)

    wo = w_out[0].astype(BF16)
    x1 = _outproj(x.reshape(nbatch * t, d), attn, rw, g1, wo[:D_ATTN], wo[D_ATTN:], t)
    out = _ffn(x1, norm2_g, sh2, sc2, g2, final_norm_g.reshape(1, d),
               ffn_w1[0].astype(BF16), ffn_w3[0].astype(BF16), ffn_w2[0].astype(BF16), t)
    return out.reshape(nbatch, t, d)
```

```python
import functools

import numpy as np
import jax
import jax.numpy as jnp
from jax import lax
from jax.experimental import pallas as pl
from jax.experimental.pallas import tpu as pltpu

F32 = jnp.float32
BF16 = jnp.bfloat16
HI = lax.Precision.HIGHEST

D_MODEL = 2048
HEAD_DIM = 64
N_HEADS_A = 16
N_KV_HEADS = 4
GQA_GROUP = 4
D_ATTN = 1024
D_RWKV = 1024
D_LORA = 64
D_GATE_LORA = 160
D_FF = 5632
GRID_W = 64
WINDOW = 128
BLOCK = 128
ROPE_BASE = 10000.0
NORM_EPS = 1e-6
LNX_EPS = 64e-5
MASK_VALUE = -1e30

LANES = 128
ROWS = 256
CHUNK = 64
WKV_PAIRS = 2
VMEM_LIMIT = 56 * 1024 * 1024

Z_R, Z_K, Z_V, Z_LORA = 0, 1024, 2048, 3072
Z_RWKV_W = 3584
Z_Q, Z_KD, Z_VD = 3584, 4608, 5120
NZ = 5632
PROJ_TN = 512


def _cparams(sem):
    return pltpu.CompilerParams(dimension_semantics=sem, vmem_limit_bytes=VMEM_LIMIT)


def _sigmoid(x):
    return 1.0 / (1.0 + jnp.exp(-x))


def _dot(a, b, precision=None):
    return jnp.dot(a, b, preferred_element_type=F32, precision=precision)


def _dot_nt(a, b, precision=None):
    return lax.dot_general(a, b, (((1,), (1,)), ((), ())), preferred_element_type=F32, precision=precision)


def _split(x, terms):
    out = []
    for _ in range(terms - 1):
        hi = x.astype(BF16)
        out.append(hi)
        x = x - hi.astype(F32)
    out.append(x.astype(BF16))
    return out


def _mm(a, b, passes, nt=False):
    d = _dot_nt if nt else _dot
    if passes == 1:
        return d(a.astype(BF16), b.astype(BF16))
    if passes == 3:
        ah, al = _split(a, 2)
        bh, bl = _split(b, 2)
        return d(ah, bh) + (d(ah, bl) + d(al, bh))
    return d(a, b, HI)


def _adaln_kernel(c_ref, w_ref, b_ref, o_ref):
    c = c_ref[...]
    s = c * _sigmoid(c)
    o_ref[...] = _dot(s, w_ref[...], HI) + b_ref[...]


def _adaln(cc, w, b):
    n = w.shape[1]
    tn = 1024
    return pl.pallas_call(
        _adaln_kernel,
        grid=(n // tn,),
        in_specs=[
            pl.BlockSpec((8, D_MODEL), lambda j: (0, 0)),
            pl.BlockSpec((D_MODEL, tn), lambda j: (0, j)),
            pl.BlockSpec((1, tn), lambda j: (0, j)),
        ],
        out_specs=pl.BlockSpec((8, tn), lambda j: (0, j)),
        out_shape=jax.ShapeDtypeStruct((8, n), F32),
        compiler_params=_cparams(("parallel",)),
        name="adaln",
    )(cc, w, b.reshape(1, n))


def _inproj_kernel(x_ref, g_ref, sh_ref, sc_ref, cos_ref, sin_ref, w_ref, z_ref, *, nblk):
    i = pl.program_id(0)
    x = x_ref[...]
    ms = jnp.mean(x * x, axis=-1, keepdims=True)
    h = x * lax.rsqrt(ms + NORM_EPS) * g_ref[...]
    h = h * (1.0 + sc_ref[0]) + sh_ref[0]
    hb = h.astype(BF16)
    for j in range(NZ // PROJ_TN):
        sl = slice(j * PROJ_TN, (j + 1) * PROJ_TN)
        z_ref[:, sl] = _dot(hb, w_ref[:, sl])

    @pl.when(i % nblk != 0)
    def _():
        cs = cos_ref[...]
        sn = sin_ref[...]
        lane = lax.broadcasted_iota(jnp.int32, (ROWS, LANES), 1)
        first = (lane % 32) < 16
        for s in range((Z_VD - Z_Q) // LANES):
            sl = slice(Z_Q + s * LANES, Z_Q + (s + 1) * LANES)
            zz = z_ref[:, sl]
            partner = jnp.where(first, pltpu.roll(zz, LANES - 16, 1), pltpu.roll(zz, 16, 1))
            z_ref[:, sl] = zz * cs + partner * sn


def _inproj(xall, g, sh, sc, cos_t, sin_t, wz, nbatch, nblk):
    n = xall.shape[0]

    def sel(i):
        return jnp.where(i % nblk == 0, nbatch, i // nblk)

    return pl.pallas_call(
        functools.partial(_inproj_kernel, nblk=nblk),
        grid=(n // ROWS,),
        in_specs=[
            pl.BlockSpec((ROWS, D_MODEL), lambda i: (i, 0)),
            pl.BlockSpec((1, D_MODEL), lambda i: (0, 0)),
            pl.BlockSpec((1, 1, D_MODEL), lambda i: (sel(i), 0, 0)),
            pl.BlockSpec((1, 1, D_MODEL), lambda i: (sel(i), 0, 0)),
            pl.BlockSpec((ROWS, LANES), lambda i: (jnp.maximum(i % nblk - 1, 0), 0)),
            pl.BlockSpec((ROWS, LANES), lambda i: (jnp.maximum(i % nblk - 1, 0), 0)),
            pl.BlockSpec((D_MODEL, NZ), lambda i: (0, 0), pipeline_mode=pl.Buffered(1)),
        ],
        out_specs=pl.BlockSpec((ROWS, NZ), lambda i: (i, 0)),
        out_shape=jax.ShapeDtypeStruct((n, NZ), F32),
        compiler_params=_cparams(("parallel",)),
        name="inproj",
    )(xall, g, sh, sc, cos_t, sin_t, wz)


def _attn_kernel(sink_ref, q_ref, kp_ref, kc_ref, kn_ref, vp_ref, vc_ref, vn_ref, kx_ref, vx_ref, o_ref, *, nb):
    i = pl.program_id(1)
    h = pl.program_id(2)
    nctx = kx_ref.shape[0]
    nk = 3 * BLOCK + nctx
    lane = lax.broadcasted_iota(jnp.int32, (BLOCK, LANES), 1)
    lo = lane < HEAD_DIM

    q = q_ref[...]
    parts = []
    for g in range(GQA_GROUP):
        qp = q[:, (g // 2) * LANES:(g // 2 + 1) * LANES]
        parts.append(jnp.where(lo, qp, 0.0) if g % 2 == 0 else jnp.where(lo, 0.0, qp))
    qs = jnp.concatenate(parts, axis=0).astype(BF16)
    ks = jnp.concatenate([kp_ref[...], kc_ref[...], kn_ref[...], kx_ref[...]], axis=0).astype(BF16)
    s = _dot_nt(qs, ks)

    row = lax.broadcasted_iota(jnp.int32, (GQA_GROUP * BLOCK, nk), 0)
    col = lax.broadcasted_iota(jnp.int32, (GQA_GROUP * BLOCK, nk), 1)
    band_pos = col - BLOCK
    diff = band_pos - (row % BLOCK)
    in_band = jnp.where(diff >= -WINDOW, jnp.where(diff <= WINDOW, 1, 0), 0)
    in_seq = jnp.where(band_pos >= -i * BLOCK, jnp.where(band_pos < (nb - i) * BLOCK, 1, 0), 0)
    valid = jnp.where(col >= 3 * BLOCK, 1, in_band * in_seq)
    s = jnp.where(valid > 0, s, MASK_VALUE)

    rowc = lax.broadcasted_iota(jnp.int32, (GQA_GROUP * BLOCK, 1), 0)
    sk = jnp.where(rowc < BLOCK, sink_ref[h * GQA_GROUP],
                   jnp.where(rowc < 2 * BLOCK, sink_ref[h * GQA_GROUP + 1],
                             jnp.where(rowc < 3 * BLOCK, sink_ref[h * GQA_GROUP + 2], sink_ref[h * GQA_GROUP + 3])))
    m = jnp.maximum(jnp.max(s, axis=-1, keepdims=True), sk)
    p = jnp.exp(s - m)
    den = jnp.sum(p, axis=-1, keepdims=True) + jnp.exp(sk - m)
    vs = jnp.concatenate([vp_ref[...], vc_ref[...], vn_ref[...], vx_ref[...]], axis=0).astype(BF16)
    o = _dot(p.astype(BF16), vs) / den
    out = jnp.concatenate([jnp.where(lo, o[0:BLOCK], o[BLOCK:2 * BLOCK]),
                           jnp.where(lo, o[2 * BLOCK:3 * BLOCK], o[3 * BLOCK:4 * BLOCK])], axis=1)
    o_ref[...] = out.astype(o_ref.dtype)


def _attention(z, sink, nbatch, t, nctx):
    nb = t // BLOCK
    rpb = (nctx + t) // BLOCK
    cpb = nctx // BLOCK
    kcol = Z_KD // LANES
    vcol = Z_VD // LANES

    def lat(off):
        def f(b, i, h):
            return b * rpb + cpb + jnp.clip(i + off, 0, nb - 1)
        return f

    def kspec(off, col0):
        return pl.BlockSpec((BLOCK, LANES), lambda b, i, h: (lat(off)(b, i, h), col0 + h))

    def xspec(col0):
        return pl.BlockSpec((nctx, LANES), lambda b, i, h: (b * (rpb * BLOCK // nctx), col0 + h))

    return pl.pallas_call(
        functools.partial(_attn_kernel, nb=nb),
        grid=(nbatch, nb, N_KV_HEADS),
        in_specs=[
            pl.BlockSpec(memory_space=pltpu.SMEM),
            pl.BlockSpec((BLOCK, 2 * LANES), lambda b, i, h: (lat(0)(b, i, h), Z_Q // (2 * LANES) + h)),
            kspec(-1, kcol), kspec(0, kcol), kspec(1, kcol),
            kspec(-1, vcol), kspec(0, vcol), kspec(1, vcol),
            xspec(kcol), xspec(vcol),
        ],
        out_specs=pl.BlockSpec((BLOCK, 2 * LANES), lambda b, i, h: (b * nb + i, h)),
        out_shape=jax.ShapeDtypeStruct((nbatch * t, D_ATTN), BF16),
        compiler_params=_cparams(("parallel", "parallel", "parallel")),
        name="attention",
    )(sink, z, z, z, z, z, z, z, z, z)


def _head_ones():
    r = lax.broadcasted_iota(jnp.int32, (LANES, LANES), 0)
    c = lax.broadcasted_iota(jnp.int32, (LANES, LANES), 1)
    return jnp.where((r // HEAD_DIM) == (c // HEAD_DIM), 1.0, 0.0).astype(F32)


def _prep_kernel(zm_ref, zp_ref, zn_ref, mup_ref, mun_ref, w0_ref, wupf_ref, wupb_ref, a0_ref, aupf_ref, aupb_ref,
                 kk_ref, ka_ref, gup_ref,
                 r_o, v_o, lwf_o, lwb_o, kdf_o, kdb_o, nkk_o, bf_o, bb_o, gate_o, *, nblk):
    li = pl.program_id(0) % nblk
    keep_p = jnp.where((li == 0) | (li == 1), 0.0, 1.0)
    keep_n = jnp.where((li == 0) | (li == nblk - 1), 0.0, 1.0)
    rowi = lax.broadcasted_iota(jnp.int32, (ROWS, LANES), 0)

    def shifted(c0):
        sl = slice(c0, c0 + LANES)
        zc = zm_ref[:, sl]
        prev = jnp.where(rowi == 0, zp_ref[7:8, sl] * keep_p, pltpu.roll(zc, 1, 0))
        nxt = jnp.where(rowi == ROWS - 1, zn_ref[0:1, sl] * keep_n, pltpu.roll(zc, ROWS - 1, 0))
        return zc + mup_ref[:, sl] * (prev - zc) + mun_ref[:, sl] * (nxt - zc)

    ones = _head_ones()
    th_w = jnp.tanh(shifted(Z_LORA))
    ad = shifted(Z_LORA + LANES)
    sg = jnp.concatenate([_sigmoid(shifted(Z_LORA + 2 * LANES)), _sigmoid(shifted(Z_LORA + 3 * LANES))], axis=1)

    def log_decay(pre):
        wlog = -(jnp.maximum(-pre, 0.0) + jnp.log(1.0 + jnp.exp(-jnp.abs(pre)))) - 0.5
        return -jnp.exp(wlog)

    for s in range(D_RWKV // LANES):
        sl = slice(s * LANES, (s + 1) * LANES)
        r = shifted(Z_R + s * LANES)
        k = shifted(Z_K + s * LANES)
        v = shifted(Z_V + s * LANES)
        r_o[:, sl] = r.astype(r_o.dtype)
        v_o[:, sl] = v.astype(v_o.dtype)
        kk = k * kk_ref[:, sl]
        ss = _mm(kk * kk, ones, 1)
        kk = kk * lax.rsqrt(jnp.maximum(ss, 1e-12))
        nkk_o[:, sl] = (-kk).astype(nkk_o.dtype)
        lwf_o[:, sl] = log_decay(w0_ref[0:1, sl] + _mm(th_w, wupf_ref[:, sl], 3))
        lwb_o[:, sl] = log_decay(w0_ref[1:2, sl] + _mm(th_w, wupb_ref[:, sl], 3))
        af = _sigmoid(a0_ref[0:1, sl] + _mm(ad, aupf_ref[:, sl], 1))
        ab = _sigmoid(a0_ref[1:2, sl] + _mm(ad, aupb_ref[:, sl], 1))
        ka = ka_ref[:, sl]
        kdf_o[:, sl] = (k * (1.0 + (af - 1.0) * ka)).astype(kdf_o.dtype)
        kdb_o[:, sl] = (k * (1.0 + (ab - 1.0) * ka)).astype(kdb_o.dtype)
        bf_o[:, sl] = (kk * af).astype(bf_o.dtype)
        bb_o[:, sl] = (kk * ab).astype(bb_o.dtype)
        gate_o[:, sl] = _mm(sg, gup_ref[:, sl], 1).astype(gate_o.dtype)


def _rwkv_prep(z, mup, mun, w0, wupf, wupb, a0, aupf, aupb, k_k, k_a, gup, nblk):
    n = z.shape[0]
    nsub = ROWS // 8
    full = lambda a: pl.BlockSpec(a.shape, lambda i: (0,) * a.ndim)
    outs = [jax.ShapeDtypeStruct((n, D_RWKV), F32 if name in ("lwf", "lwb") else BF16)
            for name in ("r", "v", "lwf", "lwb", "kdf", "kdb", "nkk", "bf", "bb", "gate")]
    ospec = pl.BlockSpec((ROWS, D_RWKV), lambda i: (i, 0))
    return pl.pallas_call(
        functools.partial(_prep_kernel, nblk=nblk),
        grid=(n // ROWS,),
        in_specs=[
            pl.BlockSpec((ROWS, Z_RWKV_W), lambda i: (i, 0)),
            pl.BlockSpec((8, Z_RWKV_W), lambda i: (jnp.maximum(i * nsub - 1, 0), 0)),
            pl.BlockSpec((8, Z_RWKV_W), lambda i: (jnp.minimum((i + 1) * nsub, n // 8 - 1), 0)),
            full(mup), full(mun), full(w0), full(wupf), full(wupb), full(a0), full(aupf), full(aupb),
            full(k_k), full(k_a), full(gup),
        ],
        out_specs=[ospec] * 10,
        out_shape=outs,
        compiler_params=_cparams(("parallel",)),
        name="rwkv_prep",
    )(z, z, z, mup, mun, w0, wupf, wupb, a0, aupf, aupb, k_k, k_a, gup)


P_GRAM, P_INV, P_SOLVE, P_READ, P_STATE, P_OUT, P_CARRY = 1, 1, 1, 1, 1, 1, 1


def _wkv_intra(probs, bd, eye_bd, head_a, head_a2):
    n = CHUNK
    zero = jnp.zeros((n, LANES), F32)
    each = lambda f, *ls: [f(*xs) for xs in zip(*ls)]

    def stack(x):
        return jnp.concatenate([jnp.where(head_a, x, 0.0), jnp.where(head_a, 0.0, x)], axis=0)

    def cumulative(p):
        inclb = p[6].astype(BF16)
        l0, l1, l2 = _split(p[1], 3)
        return _dot(inclb, l0) + (_dot(inclb, l1) + _dot(inclb, l2))

    cs = each(cumulative, probs)

    def factors(p, c):
        r, lw, k, v, a, b = p[:6]
        ctot = jnp.sum(lw, axis=0, keepdims=True)
        mid = 0.5 * ctot
        e_in = jnp.exp(c - mid)
        e_out = jnp.exp(mid - c)
        e_abs = jnp.exp(c)
        e_rest = jnp.exp(ctot - c)
        undo = jnp.exp(-lw)
        lhs = jnp.concatenate([stack(a * e_in * undo), stack(r * e_in)], axis=0)
        rhs = jnp.concatenate([b * e_out, k * e_out], axis=0)
        rest = jnp.concatenate([b * e_rest, k * e_rest], axis=0)
        return lhs, rhs, rest, a * e_abs * undo, r * e_abs, jnp.exp(ctot)

    fs = each(factors, probs, cs)
    grams = [_mm(f[0], f[1], P_GRAM, nt=True) for f in fs]
    aas = [jnp.where(p[7], g[:2 * n], 0.0) for p, g in zip(probs, grams)]
    ars = [jnp.where(p[8], g[2 * n:], 0.0) for p, g in zip(probs, grams)]
    a_abs = [jnp.where(bd, jnp.concatenate([aa[:n], pltpu.roll(aa[n:], n, 1)], axis=0), 0.0) for aa in aas]

    invs = [eye_bd + x for x in a_abs]
    pws = [_mm(x, x, P_INV) for x in a_abs]
    for _ in range(int(np.log2(n)) - 2):
        boths = [_mm(jnp.concatenate([pw, inv], axis=0), pw, P_INV) for pw, inv in zip(pws, invs)]
        pws = [x[:2 * n] for x in boths]
        invs = [inv + x[2 * n:] for inv, x in zip(invs, boths)]
    invs = [inv + _mm(inv, pw, P_INV) for inv, pw in zip(invs, pws)]

    akvs = [jnp.where(bd, _mm(aa, jnp.concatenate([zero, p[3]], axis=0), P_SOLVE), 0.0)
            for aa, p in zip(aas, probs)]
    wu_sts = [_mm(inv, jnp.concatenate([stack(f[3]), akv], axis=1), P_SOLVE)
              for inv, f, akv in zip(invs, fs, akvs)]
    m2s = [jnp.concatenate([x[:n] + x[n:], jnp.concatenate([zero, p[3]], axis=1)], axis=0)
           for x, p in zip(wu_sts, probs)]
    qy_sts = [_mm(ar, m2, P_READ) for ar, m2 in zip(ars, m2s)]
    ghs = [_mm(m2.T, f[2], P_STATE) for m2, f in zip(m2s, fs)]
    out = []
    for f, qy_st, gh in zip(fs, qy_sts, ghs):
        qy = jnp.where(head_a2, qy_st[:n], qy_st[n:])
        out.append((f[4] + qy[:, :LANES], qy[:, LANES:],
                    jnp.where(bd, gh[:LANES], 0.0) + eye_bd * f[5], jnp.where(bd, gh[LANES:], 0.0)))
    return out


def _wkv_kernel(rf, vf, af, lwf, kf, bf, rb, vb, ab, lwb, kb, bb, yf_o, yb_o, s_ref, q_ref, y0_ref, gm_ref, h_ref):
    j = pl.program_id(1)
    n = CHUNK
    nsub = ROWS // n
    cur = j % 2
    prv = 1 - cur

    @pl.when(j == 0)
    def _():
        s_ref[...] = jnp.zeros_like(s_ref)
        q_ref[1] = jnp.zeros(q_ref.shape[1:], F32)
        y0_ref[1] = jnp.zeros(y0_ref.shape[1:], F32)
        gm_ref[1] = jnp.zeros(gm_ref.shape[1:], F32)
        h_ref[1] = jnp.zeros(h_ref.shape[1:], F32)

    for step in range(nsub):
        for d, y_o in enumerate((yf_o, yb_o)):
            for p in range(WKV_PAIRS):
                u = step if d == 0 else nsub - 1 - step
                idx = (d * WKV_PAIRS + p) * nsub + u
                s0 = s_ref[d * WKV_PAIRS + p]
                y_o[u * n:(u + 1) * n, p * LANES:(p + 1) * LANES] = (
                    _mm(q_ref[prv, idx], s0, P_OUT, nt=True) + y0_ref[prv, idx])
                s_ref[d * WKV_PAIRS + p] = _mm(s0, gm_ref[prv, idx], P_CARRY) + h_ref[prv, idx]

    ti = lax.broadcasted_iota(jnp.int32, (n, n), 0)
    si = lax.broadcasted_iota(jnp.int32, (n, n), 1)
    t2 = lax.broadcasted_iota(jnp.int32, (2 * n, 2 * n), 0)
    s2 = lax.broadcasted_iota(jnp.int32, (2 * n, 2 * n), 1)
    tw, sw = t2 % n, s2 % n
    bd = (t2 // n) == (s2 // n)
    eye_bd = jnp.where(t2 == s2, 1.0, 0.0).astype(F32)
    head_a = lax.broadcasted_iota(jnp.int32, (n, LANES), 1) < HEAD_DIM
    head_a2 = lax.broadcasted_iota(jnp.int32, (n, 2 * LANES), 1) % LANES < HEAD_DIM
    masks = (
        (jnp.where(si <= ti, 1.0, 0.0).astype(F32), sw < tw, sw <= tw),
        (jnp.where(si >= ti, 1.0, 0.0).astype(F32), sw > tw, sw >= tw),
    )
    srcs = ((rf, lwf, kf, vf, af, bf), (rb, lwb, kb, vb, ab, bb))

    probs = [tuple(x[u * n:(u + 1) * n, p * LANES:(p + 1) * LANES].astype(F32) for x in srcs[d]) + masks[d]
             for d in range(2) for p in range(WKV_PAIRS) for u in range(nsub)]
    for idx, (q, y0, gm, h) in enumerate(_wkv_intra(probs, bd, eye_bd, head_a, head_a2)):
        q_ref[cur, idx] = q
        y0_ref[cur, idx] = y0
        gm_ref[cur, idx] = gm
        h_ref[cur, idx] = h


def _wkv(r, v, nkk, lwf, kdf, bfw, lwb, kdb, bbw, nbatch, nblk):
    n = r.shape[0]
    ngrp = D_RWKV // (WKV_PAIRS * LANES)
    nprob = 2 * WKV_PAIRS * (ROWS // CHUNK)
    last = nblk - 1
    width = WKV_PAIRS * LANES

    def bwd_blk(j):
        return jnp.where(j == 0, 0, nblk - j)

    def spec(blk):
        return pl.BlockSpec((ROWS, width), lambda g, j: ((g // ngrp) * nblk + blk(j), g % ngrp))

    fin = spec(lambda j: jnp.minimum(j, last))
    bin_ = spec(lambda j: bwd_blk(jnp.minimum(j, last)))
    fout = spec(lambda j: jnp.maximum(j - 1, 0))
    bout = spec(lambda j: bwd_blk(jnp.maximum(j - 1, 0)))
    out = jax.ShapeDtypeStruct((n, D_RWKV), F32)
    return pl.pallas_call(
        _wkv_kernel,
        grid=(nbatch * ngrp, nblk + 1),
        in_specs=[fin] * 6 + [bin_] * 6,
        out_specs=[fout, bout],
        out_shape=[out, out],
        scratch_shapes=[pltpu.VMEM((2 * WKV_PAIRS, LANES, LANES), F32),
                        pltpu.VMEM((2, nprob, CHUNK, LANES), F32), pltpu.VMEM((2, nprob, CHUNK, LANES), F32),
                        pltpu.VMEM((2, nprob, LANES, LANES), F32), pltpu.VMEM((2, nprob, LANES, LANES), F32)],
        compiler_params=_cparams(("parallel", "arbitrary")),
        name="wkv",
    )(r, v, nkk, lwf, kdf, bfw, r, v, nkk, lwb, kdb, bbw)


def _rwkv_out_kernel(yf_ref, yb_ref, r_ref, v_ref, kdf_ref, kdb_ref, gate_ref, rk_ref, lg_ref, lb_ref, o_ref):
    mean_mat = _head_ones() * (1.0 / HEAD_DIM)
    ones = _head_ones()
    for s in range(D_RWKV // LANES):
        sl = slice(s * LANES, (s + 1) * LANES)
        y = yf_ref[:, sl] + yb_ref[:, sl]
        mean = _mm(y, mean_mat, 3)
        yc = y - mean
        var = _mm(yc * yc, mean_mat, 1)
        yn = yc * lax.rsqrt(var + LNX_EPS) * lg_ref[:, sl] + lb_ref[:, sl]
        r = r_ref[:, sl].astype(F32)
        rk = rk_ref[:, sl]
        kd = kdf_ref[:, sl].astype(F32) + kdb_ref[:, sl].astype(F32)
        bonus = _mm(r * kd * rk, ones, 1) * v_ref[:, sl].astype(F32)
        o_ref[:, sl] = ((yn + bonus) * gate_ref[:, sl].astype(F32)).astype(o_ref.dtype)


def _rwkv_out(yf, yb, r, v, kdf, kdb, gate, r_k, lnx_g, lnx_b, nbatch, t, nblk):
    lpb = t // ROWS
    ispec = pl.BlockSpec((ROWS, D_RWKV), lambda i: ((i // lpb) * nblk + 1 + i % lpb, 0))
    pspec = pl.BlockSpec((1, D_RWKV), lambda i: (0, 0))
    return pl.pallas_call(
        _rwkv_out_kernel,
        grid=(nbatch * lpb,),
        in_specs=[ispec] * 7 + [pspec] * 3,
        out_specs=pl.BlockSpec((ROWS, D_RWKV), lambda i: (i, 0)),
        out_shape=jax.ShapeDtypeStruct((nbatch * t, D_RWKV), BF16),
        compiler_params=_cparams(("parallel",)),
        name="rwkv_out",
    )(yf, yb, r, v, kdf, kdb, gate, r_k, lnx_g, lnx_b)


def _outproj_kernel(x_ref, attn_ref, rw_ref, g1_ref, wa_ref, wr_ref, o_ref):
    mix = _dot(attn_ref[...], wa_ref[...]) + _dot(rw_ref[...], wr_ref[...])
    o_ref[...] = x_ref[...] + g1_ref[0] * mix


def _outproj(x2, attn, rw, g1, wa, wr, t):
    n = x2.shape[0]
    tm = 512
    return pl.pallas_call(
        _outproj_kernel,
        grid=(n // tm,),
        in_specs=[
            pl.BlockSpec((tm, D_MODEL), lambda i: (i, 0)),
            pl.BlockSpec((tm, D_ATTN), lambda i: (i, 0)),
            pl.BlockSpec((tm, D_RWKV), lambda i: (i, 0)),
            pl.BlockSpec((1, 1, D_MODEL), lambda i: (i // (t // tm), 0, 0)),
            pl.BlockSpec((D_ATTN, D_MODEL), lambda i: (0, 0), pipeline_mode=pl.Buffered(1)),
            pl.BlockSpec((D_RWKV, D_MODEL), lambda i: (0, 0), pipeline_mode=pl.Buffered(1)),
        ],
        out_specs=pl.BlockSpec((tm, D_MODEL), lambda i: (i, 0)),
        out_shape=jax.ShapeDtypeStruct((n, D_MODEL), F32),
        compiler_params=_cparams(("parallel",)),
        name="outproj",
    )(x2, attn, rw, g1, wa, wr)


def _ffn_kernel(x_ref, g_ref, sh_ref, sc_ref, g2_ref, fg_ref, w1_ref, w3_ref, w2_ref, o_ref, h_ref, acc_ref):
    j = pl.program_id(1)

    @pl.when(j == 0)
    def _():
        x = x_ref[...]
        ms = jnp.mean(x * x, axis=-1, keepdims=True)
        h = x * lax.rsqrt(ms + NORM_EPS) * g_ref[...]
        h_ref[...] = (h * (1.0 + sc_ref[0]) + sh_ref[0]).astype(BF16)
        acc_ref[...] = jnp.zeros_like(acc_ref)

    h = h_ref[...]
    a = _dot(h, w1_ref[...])
    b = _dot(h, w3_ref[...])
    u = a * _sigmoid(a) * b
    acc_ref[...] += _dot(u.astype(BF16), w2_ref[...])

    @pl.when(j == pl.num_programs(1) - 1)
    def _():
        y = x_ref[...] + g2_ref[0] * acc_ref[...]
        ms = jnp.mean(y * y, axis=-1, keepdims=True)
        o_ref[...] = y * lax.rsqrt(ms + NORM_EPS) * fg_ref[...]


def _ffn(x1, g, sh, sc, g2, fg, w1, w3, w2, t):
    n = x1.shape[0]
    tm, tf = 512, 512
    modspec = pl.BlockSpec((1, 1, D_MODEL), lambda i, j: (i // (t // tm), 0, 0))
    vecspec = pl.BlockSpec((1, D_MODEL), lambda i, j: (0, 0))
    return pl.pallas_call(
        _ffn_kernel,
        grid=(n // tm, D_FF // tf),
        in_specs=[
            pl.BlockSpec((tm, D_MODEL), lambda i, j: (i, 0)),
            vecspec, modspec, modspec, modspec, vecspec,
            pl.BlockSpec((D_MODEL, tf), lambda i, j: (0, j)),
            pl.BlockSpec((D_MODEL, tf), lambda i, j: (0, j)),
            pl.BlockSpec((tf, D_MODEL), lambda i, j: (j, 0)),
        ],
        out_specs=pl.BlockSpec((tm, D_MODEL), lambda i, j: (i, 0)),
        out_shape=jax.ShapeDtypeStruct((n, D_MODEL), F32),
        scratch_shapes=[pltpu.VMEM((tm, D_MODEL), BF16), pltpu.VMEM((tm, D_MODEL), F32)],
        compiler_params=_cparams(("parallel", "arbitrary")),
        name="ffn",
    )(x1, g, sh, sc, g2, fg, w1, w3, w2)


def _z_weights(w_in):
    q0, k0, v0, r0 = 0, D_ATTN, D_ATTN + 256, D_ATTN + 512
    dup = lambda base: np.concatenate([np.tile(base + h * HEAD_DIM + np.arange(HEAD_DIM), 2) for h in range(N_KV_HEADS)])
    n_rwkv = 3 * D_RWKV + 4 * D_LORA + D_GATE_LORA
    cols = [
        w_in[:, r0:r0 + n_rwkv],
        jnp.zeros((D_MODEL, Z_RWKV_W - n_rwkv), w_in.dtype),
        w_in[:, q0:q0 + D_ATTN] * (HEAD_DIM ** -0.5),
        w_in[:, dup(k0)],
        w_in[:, dup(v0)],
    ]
    return jnp.concatenate(cols, axis=1).astype(BF16)


def _rope_tables(t):
    pos = jnp.arange(t, dtype=jnp.int32)
    row = (pos // GRID_W).astype(F32)
    col = (pos % GRID_W).astype(F32)
    nf = HEAD_DIM // 4
    freqs = 1.0 / (ROPE_BASE ** (jnp.arange(nf, dtype=F32) / nf))
    d = np.arange(LANES) % HEAD_DIM
    use_row = jnp.asarray((d // (HEAD_DIM // 2)) == 0)
    first = jnp.asarray((d % (HEAD_DIM // 2)) < nf)
    f = freqs[np.asarray(d % nf)]
    ang = jnp.where(use_row[None, :], row[:, None], col[:, None]) * f[None, :]
    return jnp.cos(ang), jnp.where(first[None, :], -jnp.sin(ang), jnp.sin(ang))


def _pad_rows(w, top, total):
    return jnp.concatenate([jnp.zeros((top, w.shape[1]), w.dtype), w,
                            jnp.zeros((total - top - w.shape[0], w.shape[1]), w.dtype)], axis=0)


def kernel(x, c, ctx, c_ctx, ada_w, ada_b, norm1_g, norm2_g, w_in, attn_sink, ts_prev, ts_next, w0, w_up, a0, a_up,
           k_k, k_a, r_k, g_up, lnx_g, lnx_b, w_out, ffn_w1, ffn_w3, ffn_w2, final_norm_g):
    nbatch, t, d = x.shape
    nctx = ctx.shape[1]
    assert d == D_MODEL and nctx == ROWS and t % ROWS == 0 and nbatch < 8 and ada_w.shape[0] == 1
    nblk = (nctx + t) // ROWS

    cc = jnp.concatenate([c, c_ctx[None, :], jnp.zeros((8 - nbatch - 1, d), F32)], axis=0)
    mod = _adaln(cc, ada_w[0], ada_b[0])
    mods = mod[:nbatch + 1].reshape(nbatch + 1, 6, 1, d)
    sh1, sc1, g1, sh2, sc2, g2 = (mods[:, i] for i in range(6))

    xall = jnp.concatenate([ctx, x], axis=1).reshape(nbatch * (nctx + t), d)
    cos_t, sin_t = _rope_tables(t)
    z = _inproj(xall, norm1_g, sh1, sc1, cos_t, sin_t, _z_weights(w_in[0]), nbatch, nblk)

    attn = _attention(z, attn_sink[0], nbatch, t, nctx)

    n_ts = ts_prev.shape[1]
    mup = jnp.pad(ts_prev, ((0, 0), (0, Z_RWKV_W - n_ts)))
    mun = jnp.pad(ts_next, ((0, 0), (0, Z_RWKV_W - n_ts)))
    wupf = _pad_rows(w_up[0, 0], 0, LANES)
    wupb = _pad_rows(w_up[0, 1], D_LORA, LANES)
    aupf = _pad_rows(a_up[0, 0], 0, LANES)
    aupb = _pad_rows(a_up[0, 1], D_LORA, LANES)
    gup = _pad_rows(g_up[0], 0, 2 * LANES)
    r, v, lwf, lwb, kdf, kdb, nkk, bfw, bbw, gate = _rwkv_prep(
        z, mup, mun, w0[0], wupf, wupb, a0[0], aupf, aupb, k_k, k_a, gup, nblk)
    yf, yb = _wkv(r, v, nkk, lwf, kdf, bfw, lwb, kdb, bbw, nbatch, nblk)
    rw = _rwkv_out(yf, yb, r, v, kdf, kdb, gate, r_k, lnx_g, lnx_b, nbatch, t, nblk)

    wo = w_out[0].astype(BF16)
    x1 = _outproj(x.reshape(nbatch * t, d), attn, rw, g1, wo[:D_ATTN], wo[D_ATTN:], t)
    out = _ffn(x1, norm2_g, sh2, sc2, g2, final_norm_g.reshape(1, d),
               ffn_w1[0].astype(BF16), ffn_w3[0].astype(BF16), ffn_w2[0].astype(BF16), t)
    return out.reshape(nbatch, t, d)
```

```python
import functools

import numpy as np
import jax
import jax.numpy as jnp
from jax import lax
from jax.experimental import pallas as pl
from jax.experimental.pallas import tpu as pltpu

F32 = jnp.float32
BF16 = jnp.bfloat16
HI = lax.Precision.HIGHEST

D_MODEL = 2048
HEAD_DIM = 64
N_HEADS_A = 16
N_KV_HEADS = 4
GQA_GROUP = 4
D_ATTN = 1024
D_RWKV = 1024
D_LORA = 64
D_GATE_LORA = 160
D_FF = 5632
GRID_W = 64
WINDOW = 128
BLOCK = 128
ROPE_BASE = 10000.0
NORM_EPS = 1e-6
LNX_EPS = 64e-5
MASK_VALUE = -1e30

LANES = 128
ROWS = 256
CHUNK = 64
WKV_PAIRS = 2
ATT_HEADS = 4
VMEM_LIMIT = 56 * 1024 * 1024

Z_R, Z_K, Z_V = 0, 1024, 2048
Z_RKV_W = 3072
Z_Q = 3072
Z_LORA = 4096
Z_LORA_W = 512
Z_KD, Z_VD = 4608, 5120
NZ = 5632
ROPE_SLABS = tuple(range(Z_Q, Z_Q + D_ATTN, LANES)) + tuple(range(Z_KD, Z_VD, LANES))
PROJ_TN = 512


def _cparams(sem):
    return pltpu.CompilerParams(dimension_semantics=sem, vmem_limit_bytes=VMEM_LIMIT)


def _sigmoid(x):
    return 1.0 / (1.0 + jnp.exp(-x))


def _dot(a, b, precision=None):
    return jnp.dot(a, b, preferred_element_type=F32, precision=precision)


def _dot_nt(a, b, precision=None):
    return lax.dot_general(a, b, (((1,), (1,)), ((), ())), preferred_element_type=F32, precision=precision)


def _split(x, terms):
    out = []
    for _ in range(terms - 1):
        hi = x.astype(BF16)
        out.append(hi)
        x = x - hi.astype(F32)
    out.append(x.astype(BF16))
    return out


def _mm(a, b, passes, nt=False):
    d = _dot_nt if nt else _dot
    if passes == 1:
        return d(a.astype(BF16), b.astype(BF16))
    if passes == 3:
        ah, al = _split(a, 2)
        bh, bl = _split(b, 2)
        return d(ah, bh) + (d(ah, bl) + d(al, bh))
    return d(a, b, HI)


def _adaln_kernel(c_ref, w_ref, b_ref, o_ref):
    c = c_ref[...]
    s = c * _sigmoid(c)
    o_ref[...] = _dot(s, w_ref[...], HI) + b_ref[...]


def _adaln(cc, w, b):
    n = w.shape[1]
    tn = 1024
    return pl.pallas_call(
        _adaln_kernel,
        grid=(n // tn,),
        in_specs=[
            pl.BlockSpec((8, D_MODEL), lambda j: (0, 0)),
            pl.BlockSpec((D_MODEL, tn), lambda j: (0, j)),
            pl.BlockSpec((1, tn), lambda j: (0, j)),
        ],
        out_specs=pl.BlockSpec((8, tn), lambda j: (0, j)),
        out_shape=jax.ShapeDtypeStruct((8, n), F32),
        compiler_params=_cparams(("parallel",)),
        name="adaln",
    )(cc, w, b.reshape(1, n))


def _inproj_kernel(x_ref, ctx_ref, g_ref, sh_ref, sc_ref, cos_ref, sin_ref, w_ref, z_ref, *, nblk):
    i = pl.program_id(0)
    x = jnp.where(i % nblk == 0, ctx_ref[...], x_ref[...])
    ms = jnp.mean(x * x, axis=-1, keepdims=True)
    h = x * lax.rsqrt(ms + NORM_EPS) * g_ref[...]
    h = h * (1.0 + sc_ref[0]) + sh_ref[0]
    hb = h.astype(BF16)
    for j in range(NZ // PROJ_TN):
        sl = slice(j * PROJ_TN, (j + 1) * PROJ_TN)
        z_ref[:, sl] = _dot(hb, w_ref[:, sl])

    @pl.when(i % nblk != 0)
    def _():
        cs = cos_ref[...]
        sn = sin_ref[...]
        lane = lax.broadcasted_iota(jnp.int32, (ROWS, LANES), 1)
        first = (lane % 32) < 16
        for c0 in ROPE_SLABS:
            sl = slice(c0, c0 + LANES)
            zz = z_ref[:, sl]
            partner = jnp.where(first, pltpu.roll(zz, LANES - 16, 1), pltpu.roll(zz, 16, 1))
            z_ref[:, sl] = zz * cs + partner * sn


def _inproj(x2, ctx2, g, sh, sc, cos_t, sin_t, wz, nbatch, nblk):
    n = nbatch * nblk * ROWS
    lpb = nblk - 1

    def sel(i):
        return jnp.where(i % nblk == 0, nbatch, i // nblk)

    return pl.pallas_call(
        functools.partial(_inproj_kernel, nblk=nblk),
        grid=(n // ROWS,),
        in_specs=[
            pl.BlockSpec((ROWS, D_MODEL), lambda i: ((i // nblk) * lpb + jnp.maximum(i % nblk - 1, 0), 0)),
            pl.BlockSpec((ROWS, D_MODEL), lambda i: (i // nblk, 0)),
            pl.BlockSpec((1, D_MODEL), lambda i: (0, 0)),
            pl.BlockSpec((1, 1, D_MODEL), lambda i: (sel(i), 0, 0)),
            pl.BlockSpec((1, 1, D_MODEL), lambda i: (sel(i), 0, 0)),
            pl.BlockSpec((ROWS, LANES), lambda i: (jnp.maximum(i % nblk - 1, 0), 0)),
            pl.BlockSpec((ROWS, LANES), lambda i: (jnp.maximum(i % nblk - 1, 0), 0)),
            pl.BlockSpec((D_MODEL, NZ), lambda i: (0, 0), pipeline_mode=pl.Buffered(1)),
        ],
        out_specs=pl.BlockSpec((ROWS, NZ), lambda i: (i, 0)),
        out_shape=jax.ShapeDtypeStruct((n, NZ), F32),
        compiler_params=_cparams(("parallel",)),
        name="inproj",
    )(x2, ctx2, g, sh, sc, cos_t, sin_t, wz)


def _attn_kernel(sink_ref, q_ref, kp_ref, kc_ref, kn_ref, vp_ref, vc_ref, vn_ref, kx_ref, vx_ref, o_ref, *, nb):
    i = pl.program_id(1)
    hg = pl.program_id(2)
    nctx = kx_ref.shape[0]
    nk = 3 * BLOCK + nctx
    lane = lax.broadcasted_iota(jnp.int32, (BLOCK, LANES), 1)
    lo = lane < HEAD_DIM
    heads = range(ATT_HEADS)

    row = lax.broadcasted_iota(jnp.int32, (GQA_GROUP * BLOCK, nk), 0)
    col = lax.broadcasted_iota(jnp.int32, (GQA_GROUP * BLOCK, nk), 1)
    band_pos = col - BLOCK
    diff = band_pos - (row % BLOCK)
    in_band = jnp.where(diff >= -WINDOW, jnp.where(diff <= WINDOW, 1, 0), 0)
    in_seq = jnp.where(band_pos >= -i * BLOCK, jnp.where(band_pos < (nb - i) * BLOCK, 1, 0), 0)
    valid = jnp.where(col >= 3 * BLOCK, 1, in_band * in_seq) > 0
    rowc = lax.broadcasted_iota(jnp.int32, (GQA_GROUP * BLOCK, 1), 0)

    def queries(h):
        parts = []
        for g in range(GQA_GROUP):
            c0 = (h * GQA_GROUP + g) // 2 * LANES
            qp = q_ref[:, c0:c0 + LANES]
            parts.append(jnp.where(lo, qp, 0.0) if g % 2 == 0 else jnp.where(lo, 0.0, qp))
        return jnp.concatenate(parts, axis=0).astype(BF16)

    def rows_of(refs, h):
        return jnp.concatenate([r[:, h * LANES:(h + 1) * LANES] for r in refs], axis=0).astype(BF16)

    def sinks(h):
        base = (hg * ATT_HEADS + h) * GQA_GROUP
        return jnp.where(rowc < BLOCK, sink_ref[base],
                         jnp.where(rowc < 2 * BLOCK, sink_ref[base + 1],
                                   jnp.where(rowc < 3 * BLOCK, sink_ref[base + 2], sink_ref[base + 3])))

    ss = [jnp.where(valid, _dot_nt(queries(h), rows_of((kp_ref, kc_ref, kn_ref, kx_ref), h)), MASK_VALUE)
          for h in heads]
    sks = [sinks(h) for h in heads]
    ms = [jnp.maximum(jnp.max(s, axis=-1, keepdims=True), sk) for s, sk in zip(ss, sks)]
    ps = [jnp.exp(s - m) for s, m in zip(ss, ms)]
    dens = [jnp.sum(p, axis=-1, keepdims=True) + jnp.exp(sk - m) for p, sk, m in zip(ps, sks, ms)]
    os_ = [_dot(p.astype(BF16), rows_of((vp_ref, vc_ref, vn_ref, vx_ref), h)) / den
           for h, (p, den) in enumerate(zip(ps, dens))]
    for h, o in enumerate(os_):
        o_ref[:, 2 * h * LANES:(2 * h + 1) * LANES] = jnp.where(lo, o[0:BLOCK], o[BLOCK:2 * BLOCK]).astype(o_ref.dtype)
        o_ref[:, (2 * h + 1) * LANES:(2 * h + 2) * LANES] = jnp.where(
            lo, o[2 * BLOCK:3 * BLOCK], o[3 * BLOCK:4 * BLOCK]).astype(o_ref.dtype)


def _attention(z, sink, nbatch, t, nctx):
    nb = t // BLOCK
    rpb = (nctx + t) // BLOCK
    cpb = nctx // BLOCK

    def lat(off):
        def f(b, i, h):
            return b * rpb + cpb + jnp.clip(i + off, 0, nb - 1)
        return f

    kw = ATT_HEADS * LANES
    qw = ATT_HEADS * GQA_GROUP * HEAD_DIM

    def kspec(off, col0):
        return pl.BlockSpec((BLOCK, kw), lambda b, i, h: (lat(off)(b, i, h), col0 // kw + h))

    def xspec(col0):
        return pl.BlockSpec((nctx, kw), lambda b, i, h: (b * (rpb * BLOCK // nctx), col0 // kw + h))

    kcol, vcol = Z_KD, Z_VD
    return pl.pallas_call(
        functools.partial(_attn_kernel, nb=nb),
        grid=(nbatch, nb, N_KV_HEADS // ATT_HEADS),
        in_specs=[
            pl.BlockSpec(memory_space=pltpu.SMEM),
            pl.BlockSpec((BLOCK, qw), lambda b, i, h: (lat(0)(b, i, h), Z_Q // qw + h)),
            kspec(-1, kcol), kspec(0, kcol), kspec(1, kcol),
            kspec(-1, vcol), kspec(0, vcol), kspec(1, vcol),
            xspec(kcol), xspec(vcol),
        ],
        out_specs=pl.BlockSpec((BLOCK, qw), lambda b, i, h: (b * nb + i, h)),
        out_shape=jax.ShapeDtypeStruct((nbatch * t, D_ATTN), BF16),
        compiler_params=_cparams(("parallel", "parallel", "parallel")),
        name="attention",
    )(sink, z, z, z, z, z, z, z, z, z)


def _head_ones():
    r = lax.broadcasted_iota(jnp.int32, (LANES, LANES), 0)
    c = lax.broadcasted_iota(jnp.int32, (LANES, LANES), 1)
    return jnp.where((r // HEAD_DIM) == (c // HEAD_DIM), 1.0, 0.0).astype(F32)


def _prep_kernel(zm_ref, zp_ref, zn_ref, lm_ref, lp_ref, ln_ref, mup_ref, mun_ref, lmup_ref, lmun_ref,
                 w0_ref, wupf_ref, wupb_ref, a0_ref, aupf_ref, aupb_ref, kk_ref, ka_ref, gup_ref,
                 r_o, v_o, lwf_o, lwb_o, kdf_o, kdb_o, nkk_o, bf_o, bb_o, gate_o, *, nblk):
    li = pl.program_id(0) % nblk
    keep_p = jnp.where((li == 0) | (li == 1), 0.0, 1.0)
    keep_n = jnp.where((li == 0) | (li == nblk - 1), 0.0, 1.0)
    rowi = lax.broadcasted_iota(jnp.int32, (ROWS, LANES), 0)

    def shift(refs, c0):
        m_ref, p_ref, n_ref, mp_ref, mn_ref = refs
        sl = slice(c0, c0 + LANES)
        zc = m_ref[:, sl]
        prev = jnp.where(rowi == 0, p_ref[7:8, sl] * keep_p, pltpu.roll(zc, 1, 0))
        nxt = jnp.where(rowi == ROWS - 1, n_ref[0:1, sl] * keep_n, pltpu.roll(zc, ROWS - 1, 0))
        return zc + mp_ref[:, sl] * (prev - zc) + mn_ref[:, sl] * (nxt - zc)

    shifted = functools.partial(shift, (zm_ref, zp_ref, zn_ref, mup_ref, mun_ref))
    lora = functools.partial(shift, (lm_ref, lp_ref, ln_ref, lmup_ref, lmun_ref))
    ones = _head_ones()
    th_w = jnp.tanh(lora(0))
    ad = lora(LANES)
    sg = jnp.concatenate([_sigmoid(lora(2 * LANES)), _sigmoid(lora(3 * LANES))], axis=1)

    def log_decay(pre):
        wlog = -(jnp.maximum(-pre, 0.0) + jnp.log(1.0 + jnp.exp(-jnp.abs(pre)))) - 0.5
        return -jnp.exp(wlog)

    for s in range(D_RWKV // LANES):
        sl = slice(s * LANES, (s + 1) * LANES)
        r = shifted(Z_R + s * LANES)
        k = shifted(Z_K + s * LANES)
        v = shifted(Z_V + s * LANES)
        r_o[:, sl] = r.astype(r_o.dtype)
        v_o[:, sl] = v.astype(v_o.dtype)
        kk = k * kk_ref[:, sl]
        ss = _mm(kk * kk, ones, 1)
        kk = kk * lax.rsqrt(jnp.maximum(ss, 1e-12))
        nkk_o[:, sl] = (-kk).astype(nkk_o.dtype)
        lwf_o[:, sl] = log_decay(w0_ref[0:1, sl] + _mm(th_w, wupf_ref[:, sl], 3))
        lwb_o[:, sl] = log_decay(w0_ref[1:2, sl] + _mm(th_w, wupb_ref[:, sl], 3))
        af = _sigmoid(a0_ref[0:1, sl] + _mm(ad, aupf_ref[:, sl], 1))
        ab = _sigmoid(a0_ref[1:2, sl] + _mm(ad, aupb_ref[:, sl], 1))
        ka = ka_ref[:, sl]
        kdf_o[:, sl] = (k * (1.0 + (af - 1.0) * ka)).astype(kdf_o.dtype)
        kdb_o[:, sl] = (k * (1.0 + (ab - 1.0) * ka)).astype(kdb_o.dtype)
        bf_o[:, sl] = (kk * af).astype(bf_o.dtype)
        bb_o[:, sl] = (kk * ab).astype(bb_o.dtype)
        gate_o[:, sl] = _mm(sg, gup_ref[:, sl], 1).astype(gate_o.dtype)


def _rwkv_prep(z, mup, mun, lmup, lmun, w0, wupf, wupb, a0, aupf, aupb, k_k, k_a, gup, nblk):
    n = z.shape[0]
    nsub = ROWS // 8
    full = lambda a: pl.BlockSpec(a.shape, lambda i: (0,) * a.ndim)
    outs = [jax.ShapeDtypeStruct((n, D_RWKV), F32 if name in ("lwf", "lwb") else BF16)
            for name in ("r", "v", "lwf", "lwb", "kdf", "kdb", "nkk", "bf", "bb", "gate")]
    ospec = pl.BlockSpec((ROWS, D_RWKV), lambda i: (i, 0))
    return pl.pallas_call(
        functools.partial(_prep_kernel, nblk=nblk),
        grid=(n // ROWS,),
        in_specs=[
            pl.BlockSpec((ROWS, Z_RKV_W), lambda i: (i, 0)),
            pl.BlockSpec((8, Z_RKV_W), lambda i: (jnp.maximum(i * nsub - 1, 0), 0)),
            pl.BlockSpec((8, Z_RKV_W), lambda i: (jnp.minimum((i + 1) * nsub, n // 8 - 1), 0)),
            pl.BlockSpec((ROWS, Z_LORA_W), lambda i: (i, Z_LORA // Z_LORA_W)),
            pl.BlockSpec((8, Z_LORA_W), lambda i: (jnp.maximum(i * nsub - 1, 0), Z_LORA // Z_LORA_W)),
            pl.BlockSpec((8, Z_LORA_W), lambda i: (jnp.minimum((i + 1) * nsub, n // 8 - 1), Z_LORA // Z_LORA_W)),
            full(mup), full(mun), full(lmup), full(lmun),
            full(w0), full(wupf), full(wupb), full(a0), full(aupf), full(aupb),
            full(k_k), full(k_a), full(gup),
        ],
        out_specs=[ospec] * 10,
        out_shape=outs,
        compiler_params=_cparams(("parallel",)),
        name="rwkv_prep",
    )(z, z, z, z, z, z, mup, mun, lmup, lmun, w0, wupf, wupb, a0, aupf, aupb, k_k, k_a, gup)


P_GRAM, P_INV, P_SOLVE, P_READ, P_STATE, P_OUT, P_CARRY = 1, 1, 1, 1, 1, 1, 1


def _wkv_intra(probs, bd, eye_bd, head_a, head_a2):
    n = CHUNK
    zero = jnp.zeros((n, LANES), F32)
    each = lambda f, *ls: [f(*xs) for xs in zip(*ls)]

    def stack(x):
        return jnp.concatenate([jnp.where(head_a, x, 0.0), jnp.where(head_a, 0.0, x)], axis=0)

    def cumulative(p):
        inclb = p[6].astype(BF16)
        l0, l1, l2 = _split(p[1], 3)
        return _dot(inclb, l0) + (_dot(inclb, l1) + _dot(inclb, l2))

    cs = each(cumulative, probs)

    def factors(p, c):
        r, lw, k, v, a, b = p[:6]
        ctot = jnp.sum(lw, axis=0, keepdims=True)
        mid = 0.5 * ctot
        e_in = jnp.exp(c - mid)
        e_out = jnp.exp(mid - c)
        e_abs = jnp.exp(c)
        e_rest = jnp.exp(ctot - c)
        undo = jnp.exp(-lw)
        lhs = jnp.concatenate([stack(a * e_in * undo), stack(r * e_in)], axis=0)
        rhs = jnp.concatenate([b * e_out, k * e_out], axis=0)
        rest = jnp.concatenate([b * e_rest, k * e_rest], axis=0)
        return lhs, rhs, rest, a * e_abs * undo, r * e_abs, jnp.exp(ctot)

    fs = each(factors, probs, cs)
    grams = [_mm(f[0], f[1], P_GRAM, nt=True) for f in fs]
    aas = [jnp.where(p[7], g[:2 * n], 0.0) for p, g in zip(probs, grams)]
    ars = [jnp.where(p[8], g[2 * n:], 0.0) for p, g in zip(probs, grams)]
    a_abs = [jnp.where(bd, jnp.concatenate([aa[:n], pltpu.roll(aa[n:], n, 1)], axis=0), 0.0) for aa in aas]

    invs = [eye_bd + x for x in a_abs]
    pws = [_mm(x, x, P_INV) for x in a_abs]
    for _ in range(int(np.log2(n)) - 2):
        boths = [_mm(jnp.concatenate([pw, inv], axis=0), pw, P_INV) for pw, inv in zip(pws, invs)]
        pws = [x[:2 * n] for x in boths]
        invs = [inv + x[2 * n:] for inv, x in zip(invs, boths)]
    invs = [inv + _mm(inv, pw, P_INV) for inv, pw in zip(invs, pws)]

    akvs = [jnp.where(bd, _mm(aa, jnp.concatenate([zero, p[3]], axis=0), P_SOLVE), 0.0)
            for aa, p in zip(aas, probs)]
    wu_sts = [_mm(inv, jnp.concatenate([stack(f[3]), akv], axis=1), P_SOLVE)
              for inv, f, akv in zip(invs, fs, akvs)]
    m2s = [jnp.concatenate([x[:n] + x[n:], jnp.concatenate([zero, p[3]], axis=1)], axis=0)
           for x, p in zip(wu_sts, probs)]
    qy_sts = [_mm(ar, m2, P_READ) for ar, m2 in zip(ars, m2s)]
    ghs = [_mm(m2.T, f[2], P_STATE) for m2, f in zip(m2s, fs)]
    out = []
    for f, qy_st, gh in zip(fs, qy_sts, ghs):
        qy = jnp.where(head_a2, qy_st[:n], qy_st[n:])
        out.append((f[4] + qy[:, :LANES], qy[:, LANES:],
                    jnp.where(bd, gh[:LANES], 0.0) + eye_bd * f[5], jnp.where(bd, gh[LANES:], 0.0)))
    return out


def _wkv_kernel(rf, vf, af, lwf, kf, bf, rb, vb, ab, lwb, kb, bb, yf_o, yb_o, s_ref, q_ref, y0_ref, gm_ref, h_ref):
    j = pl.program_id(1)
    n = CHUNK
    nsub = ROWS // n
    cur = j % 2
    prv = 1 - cur

    @pl.when(j == 0)
    def _():
        s_ref[...] = jnp.zeros_like(s_ref)
        q_ref[1] = jnp.zeros(q_ref.shape[1:], F32)
        y0_ref[1] = jnp.zeros(y0_ref.shape[1:], F32)
        gm_ref[1] = jnp.zeros(gm_ref.shape[1:], F32)
        h_ref[1] = jnp.zeros(h_ref.shape[1:], F32)

    for step in range(nsub):
        for d, y_o in enumerate((yf_o, yb_o)):
            for p in range(WKV_PAIRS):
                u = step if d == 0 else nsub - 1 - step
                idx = (d * WKV_PAIRS + p) * nsub + u
                s0 = s_ref[d * WKV_PAIRS + p]
                y_o[u * n:(u + 1) * n, p * LANES:(p + 1) * LANES] = (
                    _mm(q_ref[prv, idx], s0, P_OUT, nt=True) + y0_ref[prv, idx])
                s_ref[d * WKV_PAIRS + p] = _mm(s0, gm_ref[prv, idx], P_CARRY) + h_ref[prv, idx]

    ti = lax.broadcasted_iota(jnp.int32, (n, n), 0)
    si = lax.broadcasted_iota(jnp.int32, (n, n), 1)
    t2 = lax.broadcasted_iota(jnp.int32, (2 * n, 2 * n), 0)
    s2 = lax.broadcasted_iota(jnp.int32, (2 * n, 2 * n), 1)
    tw, sw = t2 % n, s2 % n
    bd = (t2 // n) == (s2 // n)
    eye_bd = jnp.where(t2 == s2, 1.0, 0.0).astype(F32)
    head_a = lax.broadcasted_iota(jnp.int32, (n, LANES), 1) < HEAD_DIM
    head_a2 = lax.broadcasted_iota(jnp.int32, (n, 2 * LANES), 1) % LANES < HEAD_DIM
    masks = (
        (jnp.where(si <= ti, 1.0, 0.0).astype(F32), sw < tw, sw <= tw),
        (jnp.where(si >= ti, 1.0, 0.0).astype(F32), sw > tw, sw >= tw),
    )
    srcs = ((rf, lwf, kf, vf, af, bf), (rb, lwb, kb, vb, ab, bb))

    probs = [tuple(x[u * n:(u + 1) * n, p * LANES:(p + 1) * LANES].astype(F32) for x in srcs[d]) + masks[d]
             for d in range(2) for p in range(WKV_PAIRS) for u in range(nsub)]
    for idx, (q, y0, gm, h) in enumerate(_wkv_intra(probs, bd, eye_bd, head_a, head_a2)):
        q_ref[cur, idx] = q
        y0_ref[cur, idx] = y0
        gm_ref[cur, idx] = gm
        h_ref[cur, idx] = h


def _wkv(r, v, nkk, lwf, kdf, bfw, lwb, kdb, bbw, nbatch, nblk):
    n = r.shape[0]
    ngrp = D_RWKV // (WKV_PAIRS * LANES)
    nprob = 2 * WKV_PAIRS * (ROWS // CHUNK)
    last = nblk - 1
    width = WKV_PAIRS * LANES

    def bwd_blk(j):
        return jnp.where(j == 0, 0, nblk - j)

    def spec(blk):
        return pl.BlockSpec((ROWS, width), lambda g, j: ((g // ngrp) * nblk + blk(j), g % ngrp))

    fin = spec(lambda j: jnp.minimum(j, last))
    bin_ = spec(lambda j: bwd_blk(jnp.minimum(j, last)))
    fout = spec(lambda j: jnp.maximum(j - 1, 0))
    bout = spec(lambda j: bwd_blk(jnp.maximum(j - 1, 0)))
    out = jax.ShapeDtypeStruct((n, D_RWKV), F32)
    return pl.pallas_call(
        _wkv_kernel,
        grid=(nbatch * ngrp, nblk + 1),
        in_specs=[fin] * 6 + [bin_] * 6,
        out_specs=[fout, bout],
        out_shape=[out, out],
        scratch_shapes=[pltpu.VMEM((2 * WKV_PAIRS, LANES, LANES), F32),
                        pltpu.VMEM((2, nprob, CHUNK, LANES), F32), pltpu.VMEM((2, nprob, CHUNK, LANES), F32),
                        pltpu.VMEM((2, nprob, LANES, LANES), F32), pltpu.VMEM((2, nprob, LANES, LANES), F32)],
        compiler_params=_cparams(("parallel", "arbitrary")),
        name="wkv",
    )(r, v, nkk, lwf, kdf, bfw, r, v, nkk, lwb, kdb, bbw)


def _rwkv_out_kernel(yf_ref, yb_ref, r_ref, v_ref, kdf_ref, kdb_ref, gate_ref, rk_ref, lg_ref, lb_ref, o_ref):
    mean_mat = _head_ones() * (1.0 / HEAD_DIM)
    ones = _head_ones()
    for s in range(D_RWKV // LANES):
        sl = slice(s * LANES, (s + 1) * LANES)
        y = yf_ref[:, sl] + yb_ref[:, sl]
        mean = _mm(y, mean_mat, 3)
        yc = y - mean
        var = _mm(yc * yc, mean_mat, 1)
        yn = yc * lax.rsqrt(var + LNX_EPS) * lg_ref[:, sl] + lb_ref[:, sl]
        r = r_ref[:, sl].astype(F32)
        rk = rk_ref[:, sl]
        kd = kdf_ref[:, sl].astype(F32) + kdb_ref[:, sl].astype(F32)
        bonus = _mm(r * kd * rk, ones, 1) * v_ref[:, sl].astype(F32)
        o_ref[:, sl] = ((yn + bonus) * gate_ref[:, sl].astype(F32)).astype(o_ref.dtype)


def _rwkv_out(yf, yb, r, v, kdf, kdb, gate, r_k, lnx_g, lnx_b, nbatch, t, nblk):
    lpb = t // ROWS
    ispec = pl.BlockSpec((ROWS, D_RWKV), lambda i: ((i // lpb) * nblk + 1 + i % lpb, 0))
    pspec = pl.BlockSpec((1, D_RWKV), lambda i: (0, 0))
    return pl.pallas_call(
        _rwkv_out_kernel,
        grid=(nbatch * lpb,),
        in_specs=[ispec] * 7 + [pspec] * 3,
        out_specs=pl.BlockSpec((ROWS, D_RWKV), lambda i: (i, 0)),
        out_shape=jax.ShapeDtypeStruct((nbatch * t, D_RWKV), BF16),
        compiler_params=_cparams(("parallel",)),
        name="rwkv_out",
    )(yf, yb, r, v, kdf, kdb, gate, r_k, lnx_g, lnx_b)


def _outproj_kernel(x_ref, attn_ref, rw_ref, g1_ref, g_ref, sh_ref, sc_ref, wa_ref, wr_ref, o_ref, h_ref):
    mix = _dot(attn_ref[...], wa_ref[...]) + _dot(rw_ref[...], wr_ref[...])
    x1 = x_ref[...] + g1_ref[0] * mix
    o_ref[...] = x1
    ms = jnp.mean(x1 * x1, axis=-1, keepdims=True)
    h = x1 * lax.rsqrt(ms + NORM_EPS) * g_ref[...]
    h_ref[...] = (h * (1.0 + sc_ref[0]) + sh_ref[0]).astype(h_ref.dtype)


def _outproj(x2, attn, rw, g1, g, sh, sc, wa, wr, t):
    n = x2.shape[0]
    tm = 512
    modspec = pl.BlockSpec((1, 1, D_MODEL), lambda i: (i // (t // tm), 0, 0))
    rowspec = pl.BlockSpec((tm, D_MODEL), lambda i: (i, 0))
    return pl.pallas_call(
        _outproj_kernel,
        grid=(n // tm,),
        in_specs=[
            rowspec,
            pl.BlockSpec((tm, D_ATTN), lambda i: (i, 0)),
            pl.BlockSpec((tm, D_RWKV), lambda i: (i, 0)),
            modspec,
            pl.BlockSpec((1, D_MODEL), lambda i: (0, 0)),
            modspec, modspec,
            pl.BlockSpec((D_ATTN, D_MODEL), lambda i: (0, 0), pipeline_mode=pl.Buffered(1)),
            pl.BlockSpec((D_RWKV, D_MODEL), lambda i: (0, 0), pipeline_mode=pl.Buffered(1)),
        ],
        out_specs=[rowspec, rowspec],
        out_shape=[jax.ShapeDtypeStruct((n, D_MODEL), F32), jax.ShapeDtypeStruct((n, D_MODEL), BF16)],
        compiler_params=_cparams(("parallel",)),
        name="outproj",
    )(x2, attn, rw, g1, g, sh, sc, wa, wr)


def _ffn_kernel(x_ref, h_ref, g2_ref, fg_ref, w1_ref, w3_ref, w2_ref, o_ref, acc_ref):
    j = pl.program_id(1)

    @pl.when(j == 0)
    def _():
        acc_ref[...] = jnp.zeros_like(acc_ref)

    h = h_ref[...]
    a = _dot(h, w1_ref[...])
    b = _dot(h, w3_ref[...])
    u = a * _sigmoid(a) * b
    acc_ref[...] += _dot(u.astype(BF16), w2_ref[...])

    @pl.when(j == pl.num_programs(1) - 1)
    def _():
        y = x_ref[...] + g2_ref[0] * acc_ref[...]
        ms = jnp.mean(y * y, axis=-1, keepdims=True)
        o_ref[...] = y * lax.rsqrt(ms + NORM_EPS) * fg_ref[...]


def _ffn(x1, h2, g2, fg, w1, w3, w2, t):
    n = x1.shape[0]
    tm, tf = 512, 512
    modspec = pl.BlockSpec((1, 1, D_MODEL), lambda i, j: (i // (t // tm), 0, 0))
    vecspec = pl.BlockSpec((1, D_MODEL), lambda i, j: (0, 0))
    return pl.pallas_call(
        _ffn_kernel,
        grid=(n // tm, D_FF // tf),
        in_specs=[
            pl.BlockSpec((tm, D_MODEL), lambda i, j: (i, 0)),
            pl.BlockSpec((tm, D_MODEL), lambda i, j: (i, 0)),
            modspec, vecspec,
            pl.BlockSpec((D_MODEL, tf), lambda i, j: (0, j)),
            pl.BlockSpec((D_MODEL, tf), lambda i, j: (0, j)),
            pl.BlockSpec((tf, D_MODEL), lambda i, j: (j, 0)),
        ],
        out_specs=pl.BlockSpec((tm, D_MODEL), lambda i, j: (i, 0)),
        out_shape=jax.ShapeDtypeStruct((n, D_MODEL), F32),
        scratch_shapes=[pltpu.VMEM((tm, D_MODEL), F32)],
        compiler_params=_cparams(("parallel", "arbitrary")),
        name="ffn",
    )(x1, h2, g2, fg, w1, w3, w2)


def _z_weights(w_in):
    q0, k0, v0, r0 = 0, D_ATTN, D_ATTN + N_KV_HEADS * HEAD_DIM, D_ATTN + 2 * N_KV_HEADS * HEAD_DIM
    lora0 = r0 + Z_RKV_W
    n_lora = 4 * D_LORA + D_GATE_LORA
    dup = lambda base: [w_in[:, base + h * HEAD_DIM:base + (h + 1) * HEAD_DIM]
                        for h in range(N_KV_HEADS) for _ in range(2)]
    cols = [
        w_in[:, r0:lora0],
        w_in[:, q0:q0 + D_ATTN] * (HEAD_DIM ** -0.5),
        w_in[:, lora0:lora0 + n_lora],
        jnp.zeros((D_MODEL, Z_LORA_W - n_lora), w_in.dtype),
    ] + dup(k0) + dup(v0)
    return jnp.concatenate(cols, axis=1).astype(BF16)


def _rope_tables(t):
    pos = np.arange(t)
    row = (pos // GRID_W).astype(np.float32)
    col = (pos % GRID_W).astype(np.float32)
    nf = HEAD_DIM // 4
    freqs = (np.float32(1.0) / np.float32(ROPE_BASE) ** (np.arange(nf, dtype=np.float32) / np.float32(nf))).astype(np.float32)
    d = np.arange(LANES) % HEAD_DIM
    use_row = (d // (HEAD_DIM // 2)) == 0
    first = (d % (HEAD_DIM // 2)) < nf
    ang = (np.where(use_row[None, :], row[:, None], col[:, None]) * freqs[d % nf][None, :]).astype(np.float32)
    sin = np.sin(ang)
    return jnp.asarray(np.cos(ang), F32), jnp.asarray(np.where(first[None, :], -sin, sin), F32)


def _pad_rows(w, top, total):
    return jnp.concatenate([jnp.zeros((top, w.shape[1]), w.dtype), w,
                            jnp.zeros((total - top - w.shape[0], w.shape[1]), w.dtype)], axis=0)


def kernel(x, c, ctx, c_ctx, ada_w, ada_b, norm1_g, norm2_g, w_in, attn_sink, ts_prev, ts_next, w0, w_up, a0, a_up,
           k_k, k_a, r_k, g_up, lnx_g, lnx_b, w_out, ffn_w1, ffn_w3, ffn_w2, final_norm_g):
    nbatch, t, d = x.shape
    nctx = ctx.shape[1]
    assert d == D_MODEL and nctx == ROWS and t % ROWS == 0 and nbatch < 8 and ada_w.shape[0] == 1
    nblk = (nctx + t) // ROWS

    cc = jnp.concatenate([c, c_ctx[None, :], jnp.zeros((8 - nbatch - 1, d), F32)], axis=0)
    mod = _adaln(cc, ada_w[0], ada_b[0])
    mods = mod[:nbatch + 1].reshape(nbatch + 1, 6, 1, d)
    sh1, sc1, g1, sh2, sc2, g2 = (mods[:, i] for i in range(6))

    x2 = x.reshape(nbatch * t, d)
    cos_t, sin_t = _rope_tables(t)
    z = _inproj(x2, ctx.reshape(nbatch * nctx, d), norm1_g, sh1, sc1, cos_t, sin_t, _z_weights(w_in[0]), nbatch, nblk)

    attn = _attention(z, attn_sink[0], nbatch, t, nctx)

    n_lora = ts_prev.shape[1] - Z_RKV_W
    pad_lora = lambda m: jnp.pad(m[:, Z_RKV_W:], ((0, 0), (0, Z_LORA_W - n_lora)))
    wupf = _pad_rows(w_up[0, 0], 0, LANES)
    wupb = _pad_rows(w_up[0, 1], D_LORA, LANES)
    aupf = _pad_rows(a_up[0, 0], 0, LANES)
    aupb = _pad_rows(a_up[0, 1], D_LORA, LANES)
    gup = _pad_rows(g_up[0], 0, 2 * LANES)
    r, v, lwf, lwb, kdf, kdb, nkk, bfw, bbw, gate = _rwkv_prep(
        z, ts_prev[:, :Z_RKV_W], ts_next[:, :Z_RKV_W], pad_lora(ts_prev), pad_lora(ts_next),
        w0[0], wupf, wupb, a0[0], aupf, aupb, k_k, k_a, gup, nblk)
    yf, yb = _wkv(r, v, nkk, lwf, kdf, bfw, lwb, kdb, bbw, nbatch, nblk)
    rw = _rwkv_out(yf, yb, r, v, kdf, kdb, gate, r_k, lnx_g, lnx_b, nbatch, t, nblk)

    wo = w_out[0].astype(BF16)
    x1, h2 = _outproj(x2, attn, rw, g1, norm2_g, sh2, sc2, wo[:D_ATTN], wo[D_ATTN:], t)
    out = _ffn(x1, h2, g2, final_norm_g.reshape(1, d),
               ffn_w1[0].astype(BF16), ffn_w3[0].astype(BF16), ffn_w2[0].astype(BF16), t)
    return out.reshape(nbatch, t, d)
```

```python
import functools
import math

import numpy as np
import jax
import jax.numpy as jnp
from jax import lax
from jax.experimental import pallas as pl
from jax.experimental.pallas import tpu as pltpu

F32 = jnp.float32
BF16 = jnp.bfloat16
HI = lax.Precision.HIGHEST

D_MODEL = 2048
HEAD_DIM = 64
N_HEADS_A = 16
N_KV_HEADS = 4
GQA_GROUP = 4
D_ATTN = 1024
D_RWKV = 1024
D_LORA = 64
D_GATE_LORA = 160
D_FF = 5632
GRID_W = 64
WINDOW = 128
BLOCK = 128
ROPE_BASE = 10000.0
NORM_EPS = 1e-6
LNX_EPS = 64e-5
MASK_VALUE = -1e30

LANES = 128
ROWS = 256
CHUNK = 64
WKV_PAIRS = 4
ATT_HEADS = 4
VMEM_LIMIT = 56 * 1024 * 1024

Z_R, Z_K, Z_V = 0, 1024, 2048
Z_RKV_W = 3072
Z_Q = 3072
Z_LORA = 4096
Z_LORA_W = 512
Z_KV_W = N_KV_HEADS * HEAD_DIM
Z_KC, Z_VC = 4608, 4864
NZ = 5120
ROPE_SLABS = tuple(range(Z_Q, Z_Q + D_ATTN, LANES)) + tuple(range(Z_KC, Z_VC, LANES))
PROJ_TN = 512


def _cparams(sem):
    return pltpu.CompilerParams(dimension_semantics=sem, vmem_limit_bytes=VMEM_LIMIT)


def _sigmoid(x):
    return 0.5 * jnp.tanh(0.5 * x) + 0.5


def _dot(a, b, precision=None):
    return jnp.dot(a, b, preferred_element_type=F32, precision=precision)


def _dot_nt(a, b, precision=None):
    return lax.dot_general(a, b, (((1,), (1,)), ((), ())), preferred_element_type=F32, precision=precision)


def _split(x, terms):
    out = []
    for _ in range(terms - 1):
        hi = x.astype(BF16)
        out.append(hi)
        x = x - hi.astype(F32)
    out.append(x.astype(BF16))
    return out


def _mm(a, b, passes, nt=False):
    d = _dot_nt if nt else _dot
    if passes == 1:
        return d(a.astype(BF16), b.astype(BF16))
    if passes == 3:
        ah, al = _split(a, 2)
        bh, bl = _split(b, 2)
        return d(ah, bh) + (d(ah, bl) + d(al, bh))
    return d(a, b, HI)


def _adaln_kernel(c_ref, w_ref, b_ref, o_ref):
    c = c_ref[...]
    s = c * _sigmoid(c)
    o_ref[...] = _dot(s, w_ref[...], HI) + b_ref[...]


def _adaln(cc, w, b):
    n = w.shape[1]
    tn = 1024
    return pl.pallas_call(
        _adaln_kernel,
        grid=(n // tn,),
        in_specs=[
            pl.BlockSpec((8, D_MODEL), lambda j: (0, 0)),
            pl.BlockSpec((D_MODEL, tn), lambda j: (0, j)),
            pl.BlockSpec((1, tn), lambda j: (0, j)),
        ],
        out_specs=pl.BlockSpec((8, tn), lambda j: (0, j)),
        out_shape=jax.ShapeDtypeStruct((8, n), F32),
        compiler_params=_cparams(("parallel",)),
        name="adaln",
    )(cc, w, b.reshape(1, n))


def _inproj_kernel(x_ref, ctx_ref, g_ref, sh_ref, sc_ref, cos_ref, sin_ref, w_ref, z_ref, shw_ref, *, nblk):
    li = pl.program_id(0) % nblk
    is_ctx = li == 0

    @pl.when(li <= 1)
    def _():
        shb = jnp.broadcast_to(sh_ref[0], (8, D_MODEL)).astype(BF16)
        for j in range(NZ // PROJ_TN):
            sl = slice(j * PROJ_TN, (j + 1) * PROJ_TN)
            shw_ref[:, sl] = _dot(shb, w_ref[:, sl])

    x = jnp.where(is_ctx, ctx_ref[...], x_ref[...])
    xb = (x * (g_ref[...] * (1.0 + sc_ref[0]))).astype(BF16)
    rs = lax.rsqrt(jnp.mean(x * x, axis=-1, keepdims=True) + NORM_EPS)
    cs = cos_ref[...]
    sn = sin_ref[...]
    first = (lax.broadcasted_iota(jnp.int32, (ROWS, LANES), 1) % 32) < 16
    for j in range(NZ // PROJ_TN):
        sl = slice(j * PROJ_TN, (j + 1) * PROJ_TN)
        zt = _dot(xb, w_ref[:, sl]) * rs + shw_ref[0:1, sl]
        for c0 in range(j * PROJ_TN, (j + 1) * PROJ_TN, LANES):
            zz = zt[:, c0 - j * PROJ_TN:c0 - j * PROJ_TN + LANES]
            if c0 in ROPE_SLABS:
                partner = jnp.where(first, pltpu.roll(zz, LANES - 16, 1), pltpu.roll(zz, 16, 1))
                zz = jnp.where(is_ctx, zz, zz * cs + partner * sn)
            z_ref[:, c0:c0 + LANES] = zz


def _inproj(x2, ctx2, g, sh, sc, cos_t, sin_t, wz, nbatch, nblk):
    n = nbatch * nblk * ROWS
    lpb = nblk - 1

    def sel(i):
        return jnp.where(i % nblk == 0, nbatch, i // nblk)

    return pl.pallas_call(
        functools.partial(_inproj_kernel, nblk=nblk),
        grid=(n // ROWS,),
        in_specs=[
            pl.BlockSpec((ROWS, D_MODEL), lambda i: ((i // nblk) * lpb + jnp.maximum(i % nblk - 1, 0), 0)),
            pl.BlockSpec((ROWS, D_MODEL), lambda i: (i // nblk, 0)),
            pl.BlockSpec((1, D_MODEL), lambda i: (0, 0)),
            pl.BlockSpec((1, 1, D_MODEL), lambda i: (sel(i), 0, 0)),
            pl.BlockSpec((1, 1, D_MODEL), lambda i: (sel(i), 0, 0)),
            pl.BlockSpec((ROWS, LANES), lambda i: (jnp.maximum(i % nblk - 1, 0), 0)),
            pl.BlockSpec((ROWS, LANES), lambda i: (jnp.maximum(i % nblk - 1, 0), 0)),
            pl.BlockSpec((D_MODEL, NZ), lambda i: (0, 0), pipeline_mode=pl.Buffered(1)),
        ],
        out_specs=pl.BlockSpec((ROWS, NZ), lambda i: (i, 0)),
        out_shape=jax.ShapeDtypeStruct((n, NZ), F32),
        scratch_shapes=[pltpu.VMEM((8, NZ), F32)],
        compiler_params=_cparams(("arbitrary",)),
        name="inproj",
    )(x2, ctx2, g, sh, sc, cos_t, sin_t, wz)


def _attn_kernel(sink_ref, q_ref, kp_ref, kc_ref, kn_ref, vp_ref, vc_ref, vn_ref, kx_ref, vx_ref, o_ref, *, nb):
    i = pl.program_id(1)
    hg = pl.program_id(2)
    nctx = kx_ref.shape[0]
    nk = 3 * BLOCK + nctx
    lane = lax.broadcasted_iota(jnp.int32, (BLOCK, LANES), 1)
    lo = lane < HEAD_DIM
    heads = range(ATT_HEADS)

    row = lax.broadcasted_iota(jnp.int32, (GQA_GROUP * BLOCK, nk), 0)
    col = lax.broadcasted_iota(jnp.int32, (GQA_GROUP * BLOCK, nk), 1)
    band_pos = col - BLOCK
    diff = band_pos - (row % BLOCK)
    in_band = jnp.where(diff >= -WINDOW, jnp.where(diff <= WINDOW, 1, 0), 0)
    in_seq = jnp.where(band_pos >= -i * BLOCK, jnp.where(band_pos < (nb - i) * BLOCK, 1, 0), 0)
    valid = jnp.where(col >= 3 * BLOCK, 1, in_band * in_seq) > 0
    rowc = lax.broadcasted_iota(jnp.int32, (GQA_GROUP * BLOCK, 1), 0)

    def queries(h):
        parts = []
        for g in range(GQA_GROUP):
            c0 = (h * GQA_GROUP + g) // 2 * LANES
            qp = q_ref[:, c0:c0 + LANES]
            parts.append(jnp.where(lo, qp, 0.0) if g % 2 == 0 else jnp.where(lo, 0.0, qp))
        return jnp.concatenate(parts, axis=0).astype(BF16)

    rolled = {}

    def rows_of(refs, h):
        parts = []
        for r in refs:
            key = (id(r), h // 2)
            if key not in rolled:
                x = r[:, (h // 2) * LANES:(h // 2 + 1) * LANES]
                rolled[key] = (x, pltpu.roll(x, HEAD_DIM, 1))
            x, xr = rolled[key]
            lo_r = lax.broadcasted_iota(jnp.int32, x.shape, 1) < HEAD_DIM
            parts.append(jnp.where(lo_r, x, xr) if h % 2 == 0 else jnp.where(lo_r, xr, x))
        return jnp.concatenate(parts, axis=0).astype(BF16)

    def sinks(h):
        base = (hg * ATT_HEADS + h) * GQA_GROUP
        return jnp.where(rowc < BLOCK, sink_ref[base],
                         jnp.where(rowc < 2 * BLOCK, sink_ref[base + 1],
                                   jnp.where(rowc < 3 * BLOCK, sink_ref[base + 2], sink_ref[base + 3])))

    ss = [jnp.where(valid, _dot_nt(queries(h), rows_of((kp_ref, kc_ref, kn_ref, kx_ref), h)), MASK_VALUE)
          for h in heads]
    sks = [sinks(h) for h in heads]
    ms = [jnp.maximum(jnp.max(s, axis=-1, keepdims=True), sk) for s, sk in zip(ss, sks)]
    ps = [jnp.exp(s - m) for s, m in zip(ss, ms)]
    dens = [jnp.sum(p, axis=-1, keepdims=True) + jnp.exp(sk - m) for p, sk, m in zip(ps, sks, ms)]
    os_ = [_dot(p.astype(BF16), rows_of((vp_ref, vc_ref, vn_ref, vx_ref), h)) / den
           for h, (p, den) in enumerate(zip(ps, dens))]
    for h, o in enumerate(os_):
        o_ref[:, 2 * h * LANES:(2 * h + 1) * LANES] = jnp.where(lo, o[0:BLOCK], o[BLOCK:2 * BLOCK]).astype(o_ref.dtype)
        o_ref[:, (2 * h + 1) * LANES:(2 * h + 2) * LANES] = jnp.where(
            lo, o[2 * BLOCK:3 * BLOCK], o[3 * BLOCK:4 * BLOCK]).astype(o_ref.dtype)


def _attention(z, sink, nbatch, t, nctx):
    nb = t // BLOCK
    rpb = (nctx + t) // BLOCK
    cpb = nctx // BLOCK

    def lat(off):
        def f(b, i, h):
            return b * rpb + cpb + jnp.clip(i + off, 0, nb - 1)
        return f

    kw = ATT_HEADS * HEAD_DIM
    qw = ATT_HEADS * GQA_GROUP * HEAD_DIM

    def kspec(off, col0):
        return pl.BlockSpec((BLOCK, kw), lambda b, i, h: (lat(off)(b, i, h), col0 // kw + h))

    def xspec(col0):
        return pl.BlockSpec((nctx, kw), lambda b, i, h: (b * (rpb * BLOCK // nctx), col0 // kw + h))

    kcol, vcol = Z_KC, Z_VC
    return pl.pallas_call(
        functools.partial(_attn_kernel, nb=nb),
        grid=(nbatch, nb, N_KV_HEADS // ATT_HEADS),
        in_specs=[
            pl.BlockSpec(memory_space=pltpu.SMEM),
            pl.BlockSpec((BLOCK, qw), lambda b, i, h: (lat(0)(b, i, h), Z_Q // qw + h)),
            kspec(-1, kcol), kspec(0, kcol), kspec(1, kcol),
            kspec(-1, vcol), kspec(0, vcol), kspec(1, vcol),
            xspec(kcol), xspec(vcol),
        ],
        out_specs=pl.BlockSpec((BLOCK, qw), lambda b, i, h: (b * nb + i, h)),
        out_shape=jax.ShapeDtypeStruct((nbatch * t, D_ATTN), BF16),
        compiler_params=_cparams(("parallel", "parallel", "parallel")),
        name="attention",
    )(sink, z, z, z, z, z, z, z, z, z)


def _head_ones():
    r = lax.broadcasted_iota(jnp.int32, (LANES, LANES), 0)
    c = lax.broadcasted_iota(jnp.int32, (LANES, LANES), 1)
    return jnp.where((r // HEAD_DIM) == (c // HEAD_DIM), 1.0, 0.0).astype(F32)


def _prep_kernel(zm_ref, zp_ref, zn_ref, lm_ref, lp_ref, ln_ref, mup_ref, mun_ref, lmup_ref, lmun_ref,
                 w0_ref, wupf_ref, wupb_ref, a0_ref, aupf_ref, aupb_ref, kk_ref, ka_ref, gup_ref,
                 r_o, v_o, lwf_o, lwb_o, kdf_o, kdb_o, nkk_o, bf_o, bb_o, gate_o, *, nblk):
    li = pl.program_id(0) % nblk
    keep_p = jnp.where((li == 0) | (li == 1), 0.0, 1.0)
    keep_n = jnp.where((li == 0) | (li == nblk - 1), 0.0, 1.0)
    rowi = lax.broadcasted_iota(jnp.int32, (ROWS, LANES), 0)

    def shift(refs, c0):
        m_ref, p_ref, n_ref, mp_ref, mn_ref = refs
        sl = slice(c0, c0 + LANES)
        zc = m_ref[:, sl]
        prev = jnp.where(rowi == 0, p_ref[7:8, sl] * keep_p, pltpu.roll(zc, 1, 0))
        nxt = jnp.where(rowi == ROWS - 1, n_ref[0:1, sl] * keep_n, pltpu.roll(zc, ROWS - 1, 0))
        return zc + mp_ref[:, sl] * (prev - zc) + mn_ref[:, sl] * (nxt - zc)

    shifted = functools.partial(shift, (zm_ref, zp_ref, zn_ref, mup_ref, mun_ref))
    lora = functools.partial(shift, (lm_ref, lp_ref, ln_ref, lmup_ref, lmun_ref))
    ones = _head_ones()
    th_w = jnp.tanh(lora(0))
    ad = lora(LANES)
    sg = jnp.concatenate([_sigmoid(lora(2 * LANES)), _sigmoid(lora(3 * LANES))], axis=1)

    def log_decay(pre):
        return -math.exp(-0.5) * _sigmoid(pre)

    th_b = th_w.astype(BF16)
    ad_b = ad.astype(BF16)
    sg_b = sg.astype(BF16)

    def lora_fine(xb, w):
        hi, lo = _split(w, 2)
        both = _dot(xb, jnp.concatenate([hi, lo], axis=1))
        return both[:, :LANES] + both[:, LANES:]

    for s in range(D_RWKV // LANES):
        sl = slice(s * LANES, (s + 1) * LANES)
        r = shifted(Z_R + s * LANES)
        k = shifted(Z_K + s * LANES)
        v = shifted(Z_V + s * LANES)
        r_o[:, sl] = r.astype(r_o.dtype)
        v_o[:, sl] = v.astype(v_o.dtype)
        kk = k * kk_ref[:, sl]
        ss = _mm(kk * kk, ones, 1)
        kk = kk * lax.rsqrt(jnp.maximum(ss, 1e-12))
        nkk_o[:, sl] = (-kk).astype(nkk_o.dtype)
        lwf_o[:, sl] = log_decay(w0_ref[0:1, sl] + lora_fine(th_b, wupf_ref[:, sl]))
        lwb_o[:, sl] = log_decay(w0_ref[1:2, sl] + lora_fine(th_b, wupb_ref[:, sl]))
        a_pre = _dot(ad_b, jnp.concatenate([aupf_ref[:, sl], aupb_ref[:, sl]], axis=1).astype(BF16))
        af = _sigmoid(a0_ref[0:1, sl] + a_pre[:, :LANES])
        ab = _sigmoid(a0_ref[1:2, sl] + a_pre[:, LANES:])
        ka = ka_ref[:, sl]
        kdf_o[:, sl] = (k * (1.0 + (af - 1.0) * ka)).astype(kdf_o.dtype)
        kdb_o[:, sl] = (k * (1.0 + (ab - 1.0) * ka)).astype(kdb_o.dtype)
        bf_o[:, sl] = (kk * af).astype(bf_o.dtype)
        bb_o[:, sl] = (kk * ab).astype(bb_o.dtype)
        gate_o[:, sl] = _dot(sg_b, gup_ref[:, sl].astype(BF16)).astype(gate_o.dtype)


def _rwkv_prep(z, mup, mun, lmup, lmun, w0, wupf, wupb, a0, aupf, aupb, k_k, k_a, gup, nblk):
    n = z.shape[0]
    nsub = ROWS // 8
    full = lambda a: pl.BlockSpec(a.shape, lambda i: (0,) * a.ndim)
    outs = [jax.ShapeDtypeStruct((n, D_RWKV), F32 if name in ("lwf", "lwb") else BF16)
            for name in ("r", "v", "lwf", "lwb", "kdf", "kdb", "nkk", "bf", "bb", "gate")]
    ospec = pl.BlockSpec((ROWS, D_RWKV), lambda i: (i, 0))
    return pl.pallas_call(
        functools.partial(_prep_kernel, nblk=nblk),
        grid=(n // ROWS,),
        in_specs=[
            pl.BlockSpec((ROWS, Z_RKV_W), lambda i: (i, 0)),
            pl.BlockSpec((8, Z_RKV_W), lambda i: (jnp.maximum(i * nsub - 1, 0), 0)),
            pl.BlockSpec((8, Z_RKV_W), lambda i: (jnp.minimum((i + 1) * nsub, n // 8 - 1), 0)),
            pl.BlockSpec((ROWS, Z_LORA_W), lambda i: (i, Z_LORA // Z_LORA_W)),
            pl.BlockSpec((8, Z_LORA_W), lambda i: (jnp.maximum(i * nsub - 1, 0), Z_LORA // Z_LORA_W)),
            pl.BlockSpec((8, Z_LORA_W), lambda i: (jnp.minimum((i + 1) * nsub, n // 8 - 1), Z_LORA // Z_LORA_W)),
            full(mup), full(mun), full(lmup), full(lmun),
            full(w0), full(wupf), full(wupb), full(a0), full(aupf), full(aupb),
            full(k_k), full(k_a), full(gup),
        ],
        out_specs=[ospec] * 10,
        out_shape=outs,
        compiler_params=_cparams(("parallel",)),
        name="rwkv_prep",
    )(z, z, z, z, z, z, mup, mun, lmup, lmun, w0, wupf, wupb, a0, aupf, aupb, k_k, k_a, gup)


P_GRAM, P_INV, P_SOLVE, P_READ, P_STATE, P_OUT, P_CARRY = 1, 1, 1, 1, 1, 1, 1


def _wkv_intra(probs, bd, eye_bd, head_a, head_a2):
    n = CHUNK
    zero = jnp.zeros((n, LANES), F32)
    each = lambda f, *ls: [f(*xs) for xs in zip(*ls)]

    def stack(x):
        return jnp.concatenate([jnp.where(head_a, x, 0.0), jnp.where(head_a, 0.0, x)], axis=0)

    def cumulative(p):
        inclb = p[6].astype(BF16)
        l0, l1, l2 = _split(p[1], 3)
        return _dot(inclb, l0) + (_dot(inclb, l1) + _dot(inclb, l2))

    cs = each(cumulative, probs)

    def factors(p, c):
        r, lw, k, v, a, b = p[:6]
        ctot = jnp.sum(lw, axis=0, keepdims=True)
        mid = 0.5 * ctot
        e_in = jnp.exp(c - mid)
        e_out = jnp.exp(mid - c)
        e_abs = jnp.exp(c)
        e_rest = jnp.exp(ctot - c)
        undo = jnp.exp(-lw)
        lhs = jnp.concatenate([stack(a * e_in * undo), stack(r * e_in)], axis=0)
        rhs = jnp.concatenate([b * e_out, k * e_out], axis=0)
        rest = jnp.concatenate([b * e_rest, k * e_rest], axis=0)
        return lhs, rhs, rest, a * e_abs * undo, r * e_abs, jnp.exp(ctot)

    fs = each(factors, probs, cs)
    grams = [_mm(f[0], f[1], P_GRAM, nt=True) for f in fs]
    aas = [jnp.where(p[7], g[:2 * n], 0.0) for p, g in zip(probs, grams)]
    ars = [jnp.where(p[8], g[2 * n:], 0.0) for p, g in zip(probs, grams)]
    a_abs = [jnp.where(bd, jnp.concatenate([aa[:n], pltpu.roll(aa[n:], n, 1)], axis=0), 0.0) for aa in aas]

    invs = [eye_bd + x for x in a_abs]
    pws = [_mm(x, x, P_INV) for x in a_abs]
    for _ in range(int(np.log2(n)) - 2):
        boths = [_mm(jnp.concatenate([pw, inv], axis=0), pw, P_INV) for pw, inv in zip(pws, invs)]
        pws = [x[:2 * n] for x in boths]
        invs = [inv + x[2 * n:] for inv, x in zip(invs, boths)]
    invs = [inv + _mm(inv, pw, P_INV) for inv, pw in zip(invs, pws)]

    akvs = [jnp.where(bd, _mm(aa, jnp.concatenate([zero, p[3]], axis=0), P_SOLVE), 0.0)
            for aa, p in zip(aas, probs)]
    wu_sts = [_mm(inv, jnp.concatenate([stack(f[3]), akv], axis=1), P_SOLVE)
              for inv, f, akv in zip(invs, fs, akvs)]
    m2s = [jnp.concatenate([x[:n] + x[n:], jnp.concatenate([zero, p[3]], axis=1)], axis=0)
           for x, p in zip(wu_sts, probs)]
    qy_sts = [_mm(ar, m2, P_READ) for ar, m2 in zip(ars, m2s)]
    ghs = [_mm(m2.T, f[2], P_STATE) for m2, f in zip(m2s, fs)]
    out = []
    for f, qy_st, gh in zip(fs, qy_sts, ghs):
        qy = jnp.where(head_a2, qy_st[:n], qy_st[n:])
        out.append((f[4] + qy[:, :LANES], qy[:, LANES:],
                    jnp.where(bd, gh[:LANES], 0.0) + eye_bd * f[5], jnp.where(bd, gh[LANES:], 0.0)))
    return out


def _wkv_kernel(rf, vf, af, lwf, kf, bf, rb, vb, ab, lwb, kb, bb, yf_o, yb_o, s_ref, q_ref, y0_ref, gm_ref, h_ref):
    j = pl.program_id(1)
    n = CHUNK
    nsub = ROWS // n
    cur = j % 2
    prv = 1 - cur

    @pl.when(j == 0)
    def _():
        s_ref[...] = jnp.zeros_like(s_ref)
        q_ref[1] = jnp.zeros(q_ref.shape[1:], F32)
        y0_ref[1] = jnp.zeros(y0_ref.shape[1:], F32)
        gm_ref[1] = jnp.zeros(gm_ref.shape[1:], F32)
        h_ref[1] = jnp.zeros(h_ref.shape[1:], F32)

    for step in range(nsub):
        for d, y_o in enumerate((yf_o, yb_o)):
            for p in range(WKV_PAIRS):
                u = step if d == 0 else nsub - 1 - step
                idx = (d * WKV_PAIRS + p) * nsub + u
                s0 = s_ref[d * WKV_PAIRS + p]
                y_o[u * n:(u + 1) * n, p * LANES:(p + 1) * LANES] = (
                    _mm(q_ref[prv, idx], s0, P_OUT, nt=True) + y0_ref[prv, idx])
                s_ref[d * WKV_PAIRS + p] = _mm(s0, gm_ref[prv, idx], P_CARRY) + h_ref[prv, idx]

    ti = lax.broadcasted_iota(jnp.int32, (n, n), 0)
    si = lax.broadcasted_iota(jnp.int32, (n, n), 1)
    t2 = lax.broadcasted_iota(jnp.int32, (2 * n, 2 * n), 0)
    s2 = lax.broadcasted_iota(jnp.int32, (2 * n, 2 * n), 1)
    tw, sw = t2 % n, s2 % n
    bd = (t2 // n) == (s2 // n)
    eye_bd = jnp.where(t2 == s2, 1.0, 0.0).astype(F32)
    head_a = lax.broadcasted_iota(jnp.int32, (n, LANES), 1) < HEAD_DIM
    head_a2 = lax.broadcasted_iota(jnp.int32, (n, 2 * LANES), 1) % LANES < HEAD_DIM
    masks = (
        (jnp.where(si <= ti, 1.0, 0.0).astype(F32), sw < tw, sw <= tw),
        (jnp.where(si >= ti, 1.0, 0.0).astype(F32), sw > tw, sw >= tw),
    )
    srcs = ((rf, lwf, kf, vf, af, bf), (rb, lwb, kb, vb, ab, bb))

    probs = [tuple(x[u * n:(u + 1) * n, p * LANES:(p + 1) * LANES].astype(F32) for x in srcs[d]) + masks[d]
             for d in range(2) for p in range(WKV_PAIRS) for u in range(nsub)]
    for idx, (q, y0, gm, h) in enumerate(_wkv_intra(probs, bd, eye_bd, head_a, head_a2)):
        q_ref[cur, idx] = q
        y0_ref[cur, idx] = y0
        gm_ref[cur, idx] = gm
        h_ref[cur, idx] = h


def _wkv(r, v, nkk, lwf, kdf, bfw, lwb, kdb, bbw, nbatch, nblk):
    n = r.shape[0]
    ngrp = D_RWKV // (WKV_PAIRS * LANES)
    nprob = 2 * WKV_PAIRS * (ROWS // CHUNK)
    last = nblk - 1
    width = WKV_PAIRS * LANES

    def bwd_blk(j):
        return jnp.where(j == 0, 0, nblk - j)

    def spec(blk):
        return pl.BlockSpec((ROWS, width), lambda g, j: ((g // ngrp) * nblk + blk(j), g % ngrp))

    fin = spec(lambda j: jnp.minimum(j, last))
    bin_ = spec(lambda j: bwd_blk(jnp.minimum(j, last)))
    fout = spec(lambda j: jnp.maximum(j - 1, 0))
    bout = spec(lambda j: bwd_blk(jnp.maximum(j - 1, 0)))
    out = jax.ShapeDtypeStruct((n, D_RWKV), F32)
    return pl.pallas_call(
        _wkv_kernel,
        grid=(nbatch * ngrp, nblk + 1),
        in_specs=[fin] * 6 + [bin_] * 6,
        out_specs=[fout, bout],
        out_shape=[out, out],
        scratch_shapes=[pltpu.VMEM((2 * WKV_PAIRS, LANES, LANES), F32),
                        pltpu.VMEM((2, nprob, CHUNK, LANES), F32), pltpu.VMEM((2, nprob, CHUNK, LANES), F32),
                        pltpu.VMEM((2, nprob, LANES, LANES), F32), pltpu.VMEM((2, nprob, LANES, LANES), F32)],
        compiler_params=_cparams(("parallel", "arbitrary")),
        name="wkv",
    )(r, v, nkk, lwf, kdf, bfw, r, v, nkk, lwb, kdb, bbw)


def _rwkv_out_kernel(yf_ref, yb_ref, r_ref, v_ref, kdf_ref, kdb_ref, gate_ref, rk_ref, lg_ref, lb_ref, o_ref):
    ones = _head_ones().astype(BF16)
    mean_mat = (_head_ones() * (1.0 / HEAD_DIM)).astype(BF16)
    slabs = [slice(s * LANES, (s + 1) * LANES) for s in range(D_RWKV // LANES)]
    n = yf_ref.shape[0]

    def head_mean(y):
        hi, lo = _split(y, 2)
        both = _dot(jnp.concatenate([hi, lo], axis=0), mean_mat)
        return both[:n] + both[n:]

    ys = [yf_ref[:, sl] + yb_ref[:, sl] for sl in slabs]
    ycs = [y - head_mean(y) for y in ys]
    variances = [_dot((yc * yc).astype(BF16), mean_mat) for yc in ycs]
    sums = [_dot((r_ref[:, sl].astype(F32) * (kdf_ref[:, sl].astype(F32) + kdb_ref[:, sl].astype(F32))
                  * rk_ref[:, sl]).astype(BF16), ones) for sl in slabs]
    for sl, yc, var, sm in zip(slabs, ycs, variances, sums):
        yn = yc * lax.rsqrt(var + LNX_EPS) * lg_ref[:, sl] + lb_ref[:, sl]
        bonus = sm * v_ref[:, sl].astype(F32)
        o_ref[:, sl] = ((yn + bonus) * gate_ref[:, sl].astype(F32)).astype(o_ref.dtype)


def _rwkv_out(yf, yb, r, v, kdf, kdb, gate, r_k, lnx_g, lnx_b, nbatch, t, nblk):
    lpb = t // ROWS
    ispec = pl.BlockSpec((ROWS, D_RWKV), lambda i: ((i // lpb) * nblk + 1 + i % lpb, 0))
    pspec = pl.BlockSpec((1, D_RWKV), lambda i: (0, 0))
    return pl.pallas_call(
        _rwkv_out_kernel,
        grid=(nbatch * lpb,),
        in_specs=[ispec] * 7 + [pspec] * 3,
        out_specs=pl.BlockSpec((ROWS, D_RWKV), lambda i: (i, 0)),
        out_shape=jax.ShapeDtypeStruct((nbatch * t, D_RWKV), BF16),
        compiler_params=_cparams(("parallel",)),
        name="rwkv_out",
    )(yf, yb, r, v, kdf, kdb, gate, r_k, lnx_g, lnx_b)


def _outproj_kernel(x_ref, attn_ref, rw_ref, g1_ref, g_ref, sh_ref, sc_ref, wa_ref, wr_ref, o_ref, h_ref):
    half = x_ref.shape[0] // 2
    for rows in (slice(0, half), slice(half, 2 * half)):
        mix = _dot(attn_ref[rows, :], wa_ref[...]) + _dot(rw_ref[rows, :], wr_ref[...])
        x1 = x_ref[rows, :] + g1_ref[0] * mix
        o_ref[rows, :] = x1
        ms = jnp.mean(x1 * x1, axis=-1, keepdims=True)
        h = x1 * lax.rsqrt(ms + NORM_EPS) * g_ref[...]
        h_ref[rows, :] = (h * (1.0 + sc_ref[0]) + sh_ref[0]).astype(h_ref.dtype)


def _outproj(x2, attn, rw, g1, g, sh, sc, wa, wr, t):
    n = x2.shape[0]
    tm = 512
    modspec = pl.BlockSpec((1, 1, D_MODEL), lambda i: (i // (t // tm), 0, 0))
    rowspec = pl.BlockSpec((tm, D_MODEL), lambda i: (i, 0))
    return pl.pallas_call(
        _outproj_kernel,
        grid=(n // tm,),
        in_specs=[
            rowspec,
            pl.BlockSpec((tm, D_ATTN), lambda i: (i, 0)),
            pl.BlockSpec((tm, D_RWKV), lambda i: (i, 0)),
            modspec,
            pl.BlockSpec((1, D_MODEL), lambda i: (0, 0)),
            modspec, modspec,
            pl.BlockSpec((D_ATTN, D_MODEL), lambda i: (0, 0), pipeline_mode=pl.Buffered(1)),
            pl.BlockSpec((D_RWKV, D_MODEL), lambda i: (0, 0), pipeline_mode=pl.Buffered(1)),
        ],
        out_specs=[rowspec, rowspec],
        out_shape=[jax.ShapeDtypeStruct((n, D_MODEL), F32), jax.ShapeDtypeStruct((n, D_MODEL), BF16)],
        compiler_params=_cparams(("parallel",)),
        name="outproj",
    )(x2, attn, rw, g1, g, sh, sc, wa, wr)


def _ffn_kernel(x_ref, h_ref, g2_ref, fg_ref, w1_ref, w3_ref, w2_ref, o_ref, acc_ref):
    j = pl.program_id(1)

    @pl.when(j == 0)
    def _():
        acc_ref[...] = jnp.zeros_like(acc_ref)

    h = h_ref[...]
    a = _dot(h, w1_ref[...])
    b = _dot(h, w3_ref[...])
    u = a * _sigmoid(a) * b
    acc_ref[...] += _dot(u.astype(BF16), w2_ref[...])

    @pl.when(j == pl.num_programs(1) - 1)
    def _():
        y = x_ref[...] + g2_ref[0] * acc_ref[...]
        ms = jnp.mean(y * y, axis=-1, keepdims=True)
        o_ref[...] = y * lax.rsqrt(ms + NORM_EPS) * fg_ref[...]


def _ffn(x1, h2, g2, fg, w1, w3, w2, t):
    n = x1.shape[0]
    tm, tf = 512, 512
    modspec = pl.BlockSpec((1, 1, D_MODEL), lambda i, j: (i // (t // tm), 0, 0))
    vecspec = pl.BlockSpec((1, D_MODEL), lambda i, j: (0, 0))
    return pl.pallas_call(
        _ffn_kernel,
        grid=(n // tm, D_FF // tf),
        in_specs=[
            pl.BlockSpec((tm, D_MODEL), lambda i, j: (i, 0)),
            pl.BlockSpec((tm, D_MODEL), lambda i, j: (i, 0)),
            modspec, vecspec,
            pl.BlockSpec((D_MODEL, tf), lambda i, j: (0, j)),
            pl.BlockSpec((D_MODEL, tf), lambda i, j: (0, j)),
            pl.BlockSpec((tf, D_MODEL), lambda i, j: (j, 0)),
        ],
        out_specs=pl.BlockSpec((tm, D_MODEL), lambda i, j: (i, 0)),
        out_shape=jax.ShapeDtypeStruct((n, D_MODEL), F32),
        scratch_shapes=[pltpu.VMEM((tm, D_MODEL), F32)],
        compiler_params=_cparams(("parallel", "arbitrary")),
        name="ffn",
    )(x1, h2, g2, fg, w1, w3, w2)


def _z_weights(w_in):
    q0, k0, v0, r0 = 0, D_ATTN, D_ATTN + N_KV_HEADS * HEAD_DIM, D_ATTN + 2 * N_KV_HEADS * HEAD_DIM
    lora0 = r0 + Z_RKV_W
    n_lora = 4 * D_LORA + D_GATE_LORA
    cols = [
        w_in[:, r0:lora0],
        w_in[:, q0:q0 + D_ATTN] * (HEAD_DIM ** -0.5),
        w_in[:, lora0:lora0 + n_lora],
        jnp.zeros((D_MODEL, Z_LORA_W - n_lora), w_in.dtype),
        w_in[:, k0:k0 + Z_KV_W],
        w_in[:, v0:v0 + Z_KV_W],
    ]
    return jnp.concatenate(cols, axis=1).astype(BF16)


def _rope_tables(t):
    pos = np.arange(t)
    row = (pos // GRID_W).astype(np.float32)
    col = (pos % GRID_W).astype(np.float32)
    nf = HEAD_DIM // 4
    freqs = (np.float32(1.0) / np.float32(ROPE_BASE) ** (np.arange(nf, dtype=np.float32) / np.float32(nf))).astype(np.float32)
    d = np.arange(LANES) % HEAD_DIM
    use_row = (d // (HEAD_DIM // 2)) == 0
    first = (d % (HEAD_DIM // 2)) < nf
    ang = (np.where(use_row[None, :], row[:, None], col[:, None]) * freqs[d % nf][None, :]).astype(np.float32)
    sin = np.sin(ang)
    return jnp.asarray(np.cos(ang), F32), jnp.asarray(np.where(first[None, :], -sin, sin), F32)


def _pad_rows(w, top, total):
    return jnp.concatenate([jnp.zeros((top, w.shape[1]), w.dtype), w,
                            jnp.zeros((total - top - w.shape[0], w.shape[1]), w.dtype)], axis=0)


def kernel(x, c, ctx, c_ctx, ada_w, ada_b, norm1_g, norm2_g, w_in, attn_sink, ts_prev, ts_next, w0, w_up, a0, a_up,
           k_k, k_a, r_k, g_up, lnx_g, lnx_b, w_out, ffn_w1, ffn_w3, ffn_w2, final_norm_g):
    nbatch, t, d = x.shape
    nctx = ctx.shape[1]
    assert d == D_MODEL and nctx == ROWS and t % ROWS == 0 and nbatch < 8 and ada_w.shape[0] == 1
    nblk = (nctx + t) // ROWS

    cc = jnp.concatenate([c, c_ctx[None, :], jnp.zeros((8 - nbatch - 1, d), F32)], axis=0)
    mod = _adaln(cc, ada_w[0], ada_b[0])
    mods = mod[:nbatch + 1].reshape(nbatch + 1, 6, 1, d)
    sh1, sc1, g1, sh2, sc2, g2 = (mods[:, i] for i in range(6))

    x2 = x.reshape(nbatch * t, d)
    cos_t, sin_t = _rope_tables(t)
    z = _inproj(x2, ctx.reshape(nbatch * nctx, d), norm1_g, sh1, sc1, cos_t, sin_t, _z_weights(w_in[0]), nbatch, nblk)

    attn = _attention(z, attn_sink[0], nbatch, t, nctx)

    n_lora = ts_prev.shape[1] - Z_RKV_W
    pad_lora = lambda m: jnp.pad(m[:, Z_RKV_W:], ((0, 0), (0, Z_LORA_W - n_lora)))
    wupf = _pad_rows(w_up[0, 0], 0, LANES)
    wupb = _pad_rows(w_up[0, 1], D_LORA, LANES)
    aupf = _pad_rows(a_up[0, 0], 0, LANES)
    aupb = _pad_rows(a_up[0, 1], D_LORA, LANES)
    gup = _pad_rows(g_up[0], 0, 2 * LANES)
    r, v, lwf, lwb, kdf, kdb, nkk, bfw, bbw, gate = _rwkv_prep(
        z, ts_prev[:, :Z_RKV_W], ts_next[:, :Z_RKV_W], pad_lora(ts_prev), pad_lora(ts_next),
        w0[0], wupf, wupb, a0[0], aupf, aupb, k_k, k_a, gup, nblk)
    yf, yb = _wkv(r, v, nkk, lwf, kdf, bfw, lwb, kdb, bbw, nbatch, nblk)
    rw = _rwkv_out(yf, yb, r, v, kdf, kdb, gate, r_k, lnx_g, lnx_b, nbatch, t, nblk)

    wo = w_out[0].astype(BF16)
    x1, h2 = _outproj(x2, attn, rw, g1, norm2_g, sh2, sc2, wo[:D_ATTN], wo[D_ATTN:], t)
    out = _ffn(x1, h2, g2, final_norm_g.reshape(1, d),
               ffn_w1[0].astype(BF16), ffn_w3[0].astype(BF16), ffn_w2[0].astype(BF16), t)
    return out.reshape(nbatch, t, d)
```

```python
import functools
import math

import numpy as np
import jax
import jax.numpy as jnp
from jax import lax
from jax.experimental import pallas as pl
from jax.experimental.pallas import tpu as pltpu

F32 = jnp.float32
BF16 = jnp.bfloat16
HI = lax.Precision.HIGHEST

D_MODEL = 2048
HEAD_DIM = 64
N_HEADS_A = 16
N_KV_HEADS = 4
GQA_GROUP = 4
D_ATTN = 1024
D_RWKV = 1024
D_LORA = 64
D_GATE_LORA = 160
D_FF = 5632
GRID_W = 64
WINDOW = 128
BLOCK = 128
ROPE_BASE = 10000.0
NORM_EPS = 1e-6
LNX_EPS = 64e-5
MASK_VALUE = -1e30

LANES = 128
ROWS = 256
CHUNK = 64
WKV_PAIRS = 4
ATT_HEADS = 4
ATT_GROUP = 4
LOG2E = math.log2(math.e)
VMEM_LIMIT = 56 * 1024 * 1024

Z_R, Z_K, Z_V = 0, 1024, 2048
Z_RKV_W = 3072
Z_Q = 3072
Z_LORA = 4096
Z_LORA_W = 512
Z_KV_W = N_KV_HEADS * HEAD_DIM
Z_KC, Z_VC = 4608, 4864
NZ = 5120
ROPE_SLABS = tuple(range(Z_Q, Z_Q + D_ATTN, LANES)) + tuple(range(Z_KC, Z_VC, LANES))
PROJ_TN = 512


def _cparams(sem):
    return pltpu.CompilerParams(dimension_semantics=sem, vmem_limit_bytes=VMEM_LIMIT)


def _sigmoid(x):
    return 0.5 * jnp.tanh(0.5 * x) + 0.5


def _dot(a, b, precision=None):
    return jnp.dot(a, b, preferred_element_type=F32, precision=precision)


def _dot_nt(a, b, precision=None):
    return lax.dot_general(a, b, (((1,), (1,)), ((), ())), preferred_element_type=F32, precision=precision)


def _split(x, terms):
    out = []
    for _ in range(terms - 1):
        hi = x.astype(BF16)
        out.append(hi)
        x = x - hi.astype(F32)
    out.append(x.astype(BF16))
    return out


def _mm(a, b, passes, nt=False):
    d = _dot_nt if nt else _dot
    if passes == 1:
        return d(a.astype(BF16), b.astype(BF16))
    if passes == 3:
        ah, al = _split(a, 2)
        bh, bl = _split(b, 2)
        return d(ah, bh) + (d(ah, bl) + d(al, bh))
    return d(a, b, HI)


def _adaln_kernel(c_ref, w_ref, b_ref, o_ref):
    c = c_ref[...]
    s = c * _sigmoid(c)
    o_ref[...] = _dot(s, w_ref[...], HI) + b_ref[...]


def _adaln(cc, w, b):
    n = w.shape[1]
    tn = 1024
    return pl.pallas_call(
        _adaln_kernel,
        grid=(n // tn,),
        in_specs=[
            pl.BlockSpec((8, D_MODEL), lambda j: (0, 0)),
            pl.BlockSpec((D_MODEL, tn), lambda j: (0, j)),
            pl.BlockSpec((1, tn), lambda j: (0, j)),
        ],
        out_specs=pl.BlockSpec((8, tn), lambda j: (0, j)),
        out_shape=jax.ShapeDtypeStruct((8, n), F32),
        compiler_params=_cparams(("parallel",)),
        name="adaln",
    )(cc, w, b.reshape(1, n))


def _inproj_kernel(x_ref, ctx_ref, g_ref, sh_ref, sc_ref, cos_ref, sin_ref, w_ref, z_ref, shw_ref, *, nblk):
    li = pl.program_id(0) % nblk
    is_ctx = li == 0

    @pl.when(li <= 1)
    def _():
        shb = jnp.broadcast_to(sh_ref[0], (8, D_MODEL)).astype(BF16)
        for j in range(NZ // PROJ_TN):
            sl = slice(j * PROJ_TN, (j + 1) * PROJ_TN)
            shw_ref[:, sl] = _dot(shb, w_ref[:, sl])

    x = jnp.where(is_ctx, ctx_ref[...], x_ref[...])
    xb = (x * (g_ref[...] * (1.0 + sc_ref[0]))).astype(BF16)
    rs = lax.rsqrt(jnp.mean(x * x, axis=-1, keepdims=True) + NORM_EPS)
    cs = cos_ref[...]
    sn = sin_ref[...]
    first = (lax.broadcasted_iota(jnp.int32, (ROWS, LANES), 1) % 32) < 16
    for j in range(NZ // PROJ_TN):
        sl = slice(j * PROJ_TN, (j + 1) * PROJ_TN)
        zt = _dot(xb, w_ref[:, sl]) * rs + shw_ref[0:1, sl]
        for c0 in range(j * PROJ_TN, (j + 1) * PROJ_TN, LANES):
            zz = zt[:, c0 - j * PROJ_TN:c0 - j * PROJ_TN + LANES]
            if c0 in ROPE_SLABS:
                partner = jnp.where(first, pltpu.roll(zz, LANES - 16, 1), pltpu.roll(zz, 16, 1))
                zz = jnp.where(is_ctx, zz, zz * cs + partner * sn)
            z_ref[:, c0:c0 + LANES] = zz


def _inproj(x2, ctx2, g, sh, sc, cos_t, sin_t, wz, nbatch, nblk):
    n = nbatch * nblk * ROWS
    lpb = nblk - 1

    def sel(i):
        return jnp.where(i % nblk == 0, nbatch, i // nblk)

    return pl.pallas_call(
        functools.partial(_inproj_kernel, nblk=nblk),
        grid=(n // ROWS,),
        in_specs=[
            pl.BlockSpec((ROWS, D_MODEL), lambda i: ((i // nblk) * lpb + jnp.maximum(i % nblk - 1, 0), 0)),
            pl.BlockSpec((ROWS, D_MODEL), lambda i: (i // nblk, 0)),
            pl.BlockSpec((1, D_MODEL), lambda i: (0, 0)),
            pl.BlockSpec((1, 1, D_MODEL), lambda i: (sel(i), 0, 0)),
            pl.BlockSpec((1, 1, D_MODEL), lambda i: (sel(i), 0, 0)),
            pl.BlockSpec((ROWS, LANES), lambda i: (jnp.maximum(i % nblk - 1, 0), 0)),
            pl.BlockSpec((ROWS, LANES), lambda i: (jnp.maximum(i % nblk - 1, 0), 0)),
            pl.BlockSpec((D_MODEL, NZ), lambda i: (0, 0), pipeline_mode=pl.Buffered(1)),
        ],
        out_specs=pl.BlockSpec((ROWS, NZ), lambda i: (i, 0)),
        out_shape=jax.ShapeDtypeStruct((n, NZ), F32),
        scratch_shapes=[pltpu.VMEM((8, NZ), F32)],
        compiler_params=_cparams(("arbitrary",)),
        name="inproj",
    )(x2, ctx2, g, sh, sc, cos_t, sin_t, wz)


def _attn_kernel(sink_ref, q_ref, kp_ref, kc_ref, kn_ref, vp_ref, vc_ref, vn_ref, kx_ref, vx_ref, o_ref, *, nb):
    i = pl.program_id(1)
    hg = pl.program_id(2)
    nctx = kx_ref.shape[0]
    nk = 3 * BLOCK + nctx
    lane = lax.broadcasted_iota(jnp.int32, (BLOCK, LANES), 1)
    lo = lane < HEAD_DIM
    heads = range(ATT_HEADS)

    qoff = lax.broadcasted_iota(jnp.int32, (GQA_GROUP * BLOCK, BLOCK), 0) % BLOCK
    koff = lax.broadcasted_iota(jnp.int32, (GQA_GROUP * BLOCK, BLOCK), 1)
    see_prev = jnp.where(i > 0, jnp.where(koff >= qoff, 1, 0), 0) > 0
    see_next = jnp.where(i < nb - 1, jnp.where(koff <= qoff, 1, 0), 0) > 0

    def masked(s):
        return jnp.concatenate([jnp.where(see_prev, s[:, :BLOCK], MASK_VALUE), s[:, BLOCK:2 * BLOCK],
                                jnp.where(see_next, s[:, 2 * BLOCK:3 * BLOCK], MASK_VALUE), s[:, 3 * BLOCK:]], axis=1)
    rowc = lax.broadcasted_iota(jnp.int32, (GQA_GROUP * BLOCK, 1), 0)

    def queries(h):
        parts = []
        for g in range(GQA_GROUP):
            c0 = (h * GQA_GROUP + g) // 2 * LANES
            qp = q_ref[:, c0:c0 + LANES]
            parts.append(jnp.where(lo, qp, 0.0) if g % 2 == 0 else jnp.where(lo, 0.0, qp))
        return jnp.concatenate(parts, axis=0).astype(BF16)

    rolled = {}

    def rows_of(refs, h):
        parts = []
        for r in refs:
            key = (id(r), h // 2)
            if key not in rolled:
                x = r[:, (h // 2) * LANES:(h // 2 + 1) * LANES]
                rolled[key] = (x, pltpu.roll(x, HEAD_DIM, 1))
            x, xr = rolled[key]
            lo_r = lax.broadcasted_iota(jnp.int32, x.shape, 1) < HEAD_DIM
            parts.append(jnp.where(lo_r, x, xr) if h % 2 == 0 else jnp.where(lo_r, xr, x))
        return jnp.concatenate(parts, axis=0).astype(BF16)

    def sinks(h):
        base = (hg * ATT_HEADS + h) * GQA_GROUP
        return LOG2E * jnp.where(rowc < BLOCK, sink_ref[base],
                                 jnp.where(rowc < 2 * BLOCK, sink_ref[base + 1],
                                           jnp.where(rowc < 3 * BLOCK, sink_ref[base + 2], sink_ref[base + 3])))

    for g0 in range(0, ATT_HEADS, ATT_GROUP):
        heads = range(g0, g0 + ATT_GROUP)
        ss = [masked(_dot_nt(queries(h), rows_of((kp_ref, kc_ref, kn_ref, kx_ref), h))) for h in heads]
        sks = [sinks(h) for h in heads]
        ms = [jnp.maximum(jnp.max(s, axis=-1, keepdims=True), sk) for s, sk in zip(ss, sks)]
        ps = [jnp.exp2(s - m) for s, m in zip(ss, ms)]
        dens = [jnp.sum(p, axis=-1, keepdims=True) + jnp.exp2(sk - m) for p, sk, m in zip(ps, sks, ms)]
        os_ = [_dot(p.astype(BF16), rows_of((vp_ref, vc_ref, vn_ref, vx_ref), h)) / den
               for h, p, den in zip(heads, ps, dens)]
        for h, o in zip(heads, os_):
            o_ref[:, 2 * h * LANES:(2 * h + 1) * LANES] = jnp.where(
                lo, o[0:BLOCK], o[BLOCK:2 * BLOCK]).astype(o_ref.dtype)
            o_ref[:, (2 * h + 1) * LANES:(2 * h + 2) * LANES] = jnp.where(
                lo, o[2 * BLOCK:3 * BLOCK], o[3 * BLOCK:4 * BLOCK]).astype(o_ref.dtype)


def _attention(z, sink, nbatch, t, nctx):
    nb = t // BLOCK
    rpb = (nctx + t) // BLOCK
    cpb = nctx // BLOCK

    def lat(off):
        def f(b, i, h):
            return b * rpb + cpb + jnp.clip(i + off, 0, nb - 1)
        return f

    kw = ATT_HEADS * HEAD_DIM
    qw = ATT_HEADS * GQA_GROUP * HEAD_DIM

    def kspec(off, col0):
        return pl.BlockSpec((BLOCK, kw), lambda b, i, h: (lat(off)(b, i, h), col0 // kw + h))

    def xspec(col0):
        return pl.BlockSpec((nctx, kw), lambda b, i, h: (b * (rpb * BLOCK // nctx), col0 // kw + h))

    kcol, vcol = Z_KC, Z_VC
    return pl.pallas_call(
        functools.partial(_attn_kernel, nb=nb),
        grid=(nbatch, nb, N_KV_HEADS // ATT_HEADS),
        in_specs=[
            pl.BlockSpec(memory_space=pltpu.SMEM),
            pl.BlockSpec((BLOCK, qw), lambda b, i, h: (lat(0)(b, i, h), Z_Q // qw + h)),
            kspec(-1, kcol), kspec(0, kcol), kspec(1, kcol),
            kspec(-1, vcol), kspec(0, vcol), kspec(1, vcol),
            xspec(kcol), xspec(vcol),
        ],
        out_specs=pl.BlockSpec((BLOCK, qw), lambda b, i, h: (b * nb + i, h)),
        out_shape=jax.ShapeDtypeStruct((nbatch * t, D_ATTN), BF16),
        compiler_params=_cparams(("parallel", "parallel", "parallel")),
        name="attention",
    )(sink, z, z, z, z, z, z, z, z, z)


def _head_ones():
    r = lax.broadcasted_iota(jnp.int32, (LANES, LANES), 0)
    c = lax.broadcasted_iota(jnp.int32, (LANES, LANES), 1)
    return jnp.where((r // HEAD_DIM) == (c // HEAD_DIM), 1.0, 0.0).astype(F32)


def _prep_kernel(zm_ref, zp_ref, zn_ref, lm_ref, lp_ref, ln_ref, mup_ref, mun_ref, lmup_ref, lmun_ref,
                 w0_ref, wupf_ref, wupb_ref, a0_ref, aupf_ref, aupb_ref, kk_ref, ka_ref, gup_ref,
                 r_o, v_o, lwf_o, lwb_o, kdf_o, kdb_o, nkk_o, bf_o, bb_o, gate_o, *, nblk):
    li = pl.program_id(0) % nblk
    keep_p = jnp.where((li == 0) | (li == 1), 0.0, 1.0)
    keep_n = jnp.where((li == 0) | (li == nblk - 1), 0.0, 1.0)
    rowi = lax.broadcasted_iota(jnp.int32, (ROWS, LANES), 0)

    def shift(refs, c0):
        m_ref, p_ref, n_ref, mp_ref, mn_ref = refs
        sl = slice(c0, c0 + LANES)
        zc = m_ref[:, sl]
        prev = jnp.where(rowi == 0, p_ref[7:8, sl] * keep_p, pltpu.roll(zc, 1, 0))
        nxt = jnp.where(rowi == ROWS - 1, n_ref[0:1, sl] * keep_n, pltpu.roll(zc, ROWS - 1, 0))
        return zc + mp_ref[:, sl] * (prev - zc) + mn_ref[:, sl] * (nxt - zc)

    shifted = functools.partial(shift, (zm_ref, zp_ref, zn_ref, mup_ref, mun_ref))
    lora = functools.partial(shift, (lm_ref, lp_ref, ln_ref, lmup_ref, lmun_ref))
    ones = _head_ones()
    th_w = jnp.tanh(lora(0))
    ad = lora(LANES)
    sg = jnp.concatenate([_sigmoid(lora(2 * LANES)), _sigmoid(lora(3 * LANES))], axis=1)

    def log_decay(pre):
        return -math.exp(-0.5) * _sigmoid(pre)

    th_b = th_w.astype(BF16)
    ad_b = ad.astype(BF16)
    sg_b = sg.astype(BF16)

    def lora_fine(xb, w):
        hi, lo = _split(w, 2)
        both = _dot(xb, jnp.concatenate([hi, lo], axis=1))
        return both[:, :LANES] + both[:, LANES:]

    for s in range(D_RWKV // LANES):
        sl = slice(s * LANES, (s + 1) * LANES)
        r = shifted(Z_R + s * LANES)
        k = shifted(Z_K + s * LANES)
        v = shifted(Z_V + s * LANES)
        r_o[:, sl] = r.astype(r_o.dtype)
        v_o[:, sl] = v.astype(v_o.dtype)
        kk = k * kk_ref[:, sl]
        ss = _mm(kk * kk, ones, 1)
        kk = kk * lax.rsqrt(jnp.maximum(ss, 1e-12))
        nkk_o[:, sl] = (-kk).astype(nkk_o.dtype)
        lwf_o[:, sl] = log_decay(w0_ref[0:1, sl] + lora_fine(th_b, wupf_ref[:, sl]))
        lwb_o[:, sl] = log_decay(w0_ref[1:2, sl] + lora_fine(th_b, wupb_ref[:, sl]))
        a_pre = _dot(ad_b, jnp.concatenate([aupf_ref[:, sl], aupb_ref[:, sl]], axis=1).astype(BF16))
        af = _sigmoid(a0_ref[0:1, sl] + a_pre[:, :LANES])
        ab = _sigmoid(a0_ref[1:2, sl] + a_pre[:, LANES:])
        ka = ka_ref[:, sl]
        kdf_o[:, sl] = (k * (1.0 + (af - 1.0) * ka)).astype(kdf_o.dtype)
        kdb_o[:, sl] = (k * (1.0 + (ab - 1.0) * ka)).astype(kdb_o.dtype)
        bf_o[:, sl] = (kk * af).astype(bf_o.dtype)
        bb_o[:, sl] = (kk * ab).astype(bb_o.dtype)
        gate_o[:, sl] = _dot(sg_b, gup_ref[:, sl].astype(BF16)).astype(gate_o.dtype)


def _rwkv_prep(z, mup, mun, lmup, lmun, w0, wupf, wupb, a0, aupf, aupb, k_k, k_a, gup, nblk):
    n = z.shape[0]
    nsub = ROWS // 8
    full = lambda a: pl.BlockSpec(a.shape, lambda i: (0,) * a.ndim)
    outs = [jax.ShapeDtypeStruct((n, D_RWKV), F32 if name in ("lwf", "lwb") else BF16)
            for name in ("r", "v", "lwf", "lwb", "kdf", "kdb", "nkk", "bf", "bb", "gate")]
    ospec = pl.BlockSpec((ROWS, D_RWKV), lambda i: (i, 0))
    return pl.pallas_call(
        functools.partial(_prep_kernel, nblk=nblk),
        grid=(n // ROWS,),
        in_specs=[
            pl.BlockSpec((ROWS, Z_RKV_W), lambda i: (i, 0)),
            pl.BlockSpec((8, Z_RKV_W), lambda i: (jnp.maximum(i * nsub - 1, 0), 0)),
            pl.BlockSpec((8, Z_RKV_W), lambda i: (jnp.minimum((i + 1) * nsub, n // 8 - 1), 0)),
            pl.BlockSpec((ROWS, Z_LORA_W), lambda i: (i, Z_LORA // Z_LORA_W)),
            pl.BlockSpec((8, Z_LORA_W), lambda i: (jnp.maximum(i * nsub - 1, 0), Z_LORA // Z_LORA_W)),
            pl.BlockSpec((8, Z_LORA_W), lambda i: (jnp.minimum((i + 1) * nsub, n // 8 - 1), Z_LORA // Z_LORA_W)),
            full(mup), full(mun), full(lmup), full(lmun),
            full(w0), full(wupf), full(wupb), full(a0), full(aupf), full(aupb),
            full(k_k), full(k_a), full(gup),
        ],
        out_specs=[ospec] * 10,
        out_shape=outs,
        compiler_params=_cparams(("parallel",)),
        name="rwkv_prep",
    )(z, z, z, z, z, z, mup, mun, lmup, lmun, w0, wupf, wupb, a0, aupf, aupb, k_k, k_a, gup)


P_GRAM, P_INV, P_SOLVE, P_READ, P_STATE, P_OUT, P_CARRY = 1, 1, 1, 1, 1, 1, 1


def _wkv_intra(probs, bd, eye_bd, head_a, head_a2):
    n = CHUNK
    zero = jnp.zeros((n, LANES), F32)
    each = lambda f, *ls: [f(*xs) for xs in zip(*ls)]

    def stack(x):
        return jnp.concatenate([jnp.where(head_a, x, 0.0), jnp.where(head_a, 0.0, x)], axis=0)

    def cumulative(p):
        inclb = p[6].astype(BF16)
        l0, l1, l2 = _split(p[1], 3)
        return _dot(inclb, l0) + (_dot(inclb, l1) + _dot(inclb, l2))

    cs = each(cumulative, probs)

    def factors(p, c):
        r, lw, k, v, a, b = p[:6]
        ctot = jnp.sum(lw, axis=0, keepdims=True)
        mid = 0.5 * ctot
        e_in = jnp.exp(c - mid)
        e_out = jnp.exp(mid - c)
        e_abs = jnp.exp(c)
        e_rest = jnp.exp(ctot - c)
        undo = jnp.exp(-lw)
        lhs = jnp.concatenate([stack(a * e_in * undo), stack(r * e_in)], axis=0)
        rhs = jnp.concatenate([b * e_out, k * e_out], axis=0)
        rest = jnp.concatenate([b * e_rest, k * e_rest], axis=0)
        return lhs, rhs, rest, a * e_abs * undo, r * e_abs, jnp.exp(ctot)

    fs = each(factors, probs, cs)
    grams = [_mm(f[0], f[1], P_GRAM, nt=True) for f in fs]
    aas = [jnp.where(p[7], g[:2 * n], 0.0) for p, g in zip(probs, grams)]
    ars = [jnp.where(p[8], g[2 * n:], 0.0) for p, g in zip(probs, grams)]
    a_abs = [jnp.where(bd, jnp.concatenate([aa[:n], pltpu.roll(aa[n:], n, 1)], axis=0), 0.0) for aa in aas]

    invs = [eye_bd + x for x in a_abs]
    pws = [_mm(x, x, P_INV) for x in a_abs]
    for _ in range(int(np.log2(n)) - 2):
        boths = [_mm(jnp.concatenate([pw, inv], axis=0), pw, P_INV) for pw, inv in zip(pws, invs)]
        pws = [x[:2 * n] for x in boths]
        invs = [inv + x[2 * n:] for inv, x in zip(invs, boths)]
    invs = [inv + _mm(inv, pw, P_INV) for inv, pw in zip(invs, pws)]

    akvs = [jnp.where(bd, _mm(aa, jnp.concatenate([zero, p[3]], axis=0), P_SOLVE), 0.0)
            for aa, p in zip(aas, probs)]
    wu_sts = [_mm(inv, jnp.concatenate([stack(f[3]), akv], axis=1), P_SOLVE)
              for inv, f, akv in zip(invs, fs, akvs)]
    m2s = [jnp.concatenate([x[:n] + x[n:], jnp.concatenate([zero, p[3]], axis=1)], axis=0)
           for x, p in zip(wu_sts, probs)]
    qy_sts = [_mm(ar, m2, P_READ) for ar, m2 in zip(ars, m2s)]
    ghs = [_mm(m2.T, f[2], P_STATE) for m2, f in zip(m2s, fs)]
    out = []
    for f, qy_st, gh in zip(fs, qy_sts, ghs):
        qy = jnp.where(head_a2, qy_st[:n], qy_st[n:])
        out.append((f[4] + qy[:, :LANES], qy[:, LANES:],
                    jnp.where(bd, gh[:LANES], 0.0) + eye_bd * f[5], jnp.where(bd, gh[LANES:], 0.0)))
    return out


def _wkv_kernel(rf, vf, af, lwf, kf, bf, rb, vb, ab, lwb, kb, bb, yf_o, yb_o, s_ref, q_ref, y0_ref, gm_ref, h_ref):
    j = pl.program_id(1)
    n = CHUNK
    nsub = ROWS // n
    cur = j % 2
    prv = 1 - cur

    @pl.when(j == 0)
    def _():
        s_ref[...] = jnp.zeros_like(s_ref)
        q_ref[1] = jnp.zeros(q_ref.shape[1:], F32)
        y0_ref[1] = jnp.zeros(y0_ref.shape[1:], F32)
        gm_ref[1] = jnp.zeros(gm_ref.shape[1:], F32)
        h_ref[1] = jnp.zeros(h_ref.shape[1:], F32)

    for step in range(nsub):
        for d, y_o in enumerate((yf_o, yb_o)):
            for p in range(WKV_PAIRS):
                u = step if d == 0 else nsub - 1 - step
                idx = (d * WKV_PAIRS + p) * nsub + u
                s0 = s_ref[d * WKV_PAIRS + p]
                y_o[u * n:(u + 1) * n, p * LANES:(p + 1) * LANES] = (
                    _mm(q_ref[prv, idx], s0, P_OUT, nt=True) + y0_ref[prv, idx]).astype(y_o.dtype)
                s_ref[d * WKV_PAIRS + p] = _mm(s0, gm_ref[prv, idx], P_CARRY) + h_ref[prv, idx]

    ti = lax.broadcasted_iota(jnp.int32, (n, n), 0)
    si = lax.broadcasted_iota(jnp.int32, (n, n), 1)
    t2 = lax.broadcasted_iota(jnp.int32, (2 * n, 2 * n), 0)
    s2 = lax.broadcasted_iota(jnp.int32, (2 * n, 2 * n), 1)
    tw, sw = t2 % n, s2 % n
    bd = (t2 // n) == (s2 // n)
    eye_bd = jnp.where(t2 == s2, 1.0, 0.0).astype(F32)
    head_a = lax.broadcasted_iota(jnp.int32, (n, LANES), 1) < HEAD_DIM
    head_a2 = lax.broadcasted_iota(jnp.int32, (n, 2 * LANES), 1) % LANES < HEAD_DIM
    masks = (
        (jnp.where(si <= ti, 1.0, 0.0).astype(F32), sw < tw, sw <= tw),
        (jnp.where(si >= ti, 1.0, 0.0).astype(F32), sw > tw, sw >= tw),
    )
    srcs = ((rf, lwf, kf, vf, af, bf), (rb, lwb, kb, vb, ab, bb))

    probs = [tuple(x[u * n:(u + 1) * n, p * LANES:(p + 1) * LANES].astype(F32) for x in srcs[d]) + masks[d]
             for d in range(2) for p in range(WKV_PAIRS) for u in range(nsub)]
    for idx, (q, y0, gm, h) in enumerate(_wkv_intra(probs, bd, eye_bd, head_a, head_a2)):
        q_ref[cur, idx] = q
        y0_ref[cur, idx] = y0
        gm_ref[cur, idx] = gm
        h_ref[cur, idx] = h


def _wkv(r, v, nkk, lwf, kdf, bfw, lwb, kdb, bbw, nbatch, nblk):
    n = r.shape[0]
    ngrp = D_RWKV // (WKV_PAIRS * LANES)
    nprob = 2 * WKV_PAIRS * (ROWS // CHUNK)
    last = nblk - 1
    width = WKV_PAIRS * LANES

    def bwd_blk(j):
        return jnp.where(j == 0, 0, nblk - j)

    def spec(blk):
        return pl.BlockSpec((ROWS, width), lambda g, j: ((g // ngrp) * nblk + blk(j), g % ngrp))

    fin = spec(lambda j: jnp.minimum(j, last))
    bin_ = spec(lambda j: bwd_blk(jnp.minimum(j, last)))
    fout = spec(lambda j: jnp.maximum(j - 1, 0))
    bout = spec(lambda j: bwd_blk(jnp.maximum(j - 1, 0)))
    out = jax.ShapeDtypeStruct((n, D_RWKV), BF16)
    return pl.pallas_call(
        _wkv_kernel,
        grid=(nbatch * ngrp, nblk + 1),
        in_specs=[fin] * 6 + [bin_] * 6,
        out_specs=[fout, bout],
        out_shape=[out, out],
        scratch_shapes=[pltpu.VMEM((2 * WKV_PAIRS, LANES, LANES), F32),
                        pltpu.VMEM((2, nprob, CHUNK, LANES), F32), pltpu.VMEM((2, nprob, CHUNK, LANES), F32),
                        pltpu.VMEM((2, nprob, LANES, LANES), F32), pltpu.VMEM((2, nprob, LANES, LANES), F32)],
        compiler_params=_cparams(("parallel", "arbitrary")),
        name="wkv",
    )(r, v, nkk, lwf, kdf, bfw, r, v, nkk, lwb, kdb, bbw)


def _rwkv_out_kernel(yf_ref, yb_ref, r_ref, v_ref, kdf_ref, kdb_ref, gate_ref, rk_ref, lg_ref, lb_ref, o_ref):
    ones = _head_ones().astype(BF16)
    mean_mat = (_head_ones() * (1.0 / HEAD_DIM)).astype(BF16)
    slabs = [slice(s * LANES, (s + 1) * LANES) for s in range(D_RWKV // LANES)]
    n = yf_ref.shape[0]

    def head_mean(y):
        hi, lo = _split(y, 2)
        both = _dot(jnp.concatenate([hi, lo], axis=0), mean_mat)
        return both[:n] + both[n:]

    ys = [yf_ref[:, sl].astype(F32) + yb_ref[:, sl].astype(F32) for sl in slabs]
    ycs = [y - head_mean(y) for y in ys]
    variances = [_dot((yc * yc).astype(BF16), mean_mat) for yc in ycs]
    sums = [_dot((r_ref[:, sl].astype(F32) * (kdf_ref[:, sl].astype(F32) + kdb_ref[:, sl].astype(F32))
                  * rk_ref[:, sl]).astype(BF16), ones) for sl in slabs]
    for sl, yc, var, sm in zip(slabs, ycs, variances, sums):
        yn = yc * lax.rsqrt(var + LNX_EPS) * lg_ref[:, sl] + lb_ref[:, sl]
        bonus = sm * v_ref[:, sl].astype(F32)
        o_ref[:, sl] = ((yn + bonus) * gate_ref[:, sl].astype(F32)).astype(o_ref.dtype)


def _rwkv_out(yf, yb, r, v, kdf, kdb, gate, r_k, lnx_g, lnx_b, nbatch, t, nblk):
    lpb = t // ROWS
    ispec = pl.BlockSpec((ROWS, D_RWKV), lambda i: ((i // lpb) * nblk + 1 + i % lpb, 0))
    pspec = pl.BlockSpec((1, D_RWKV), lambda i: (0, 0))
    return pl.pallas_call(
        _rwkv_out_kernel,
        grid=(nbatch * lpb,),
        in_specs=[ispec] * 7 + [pspec] * 3,
        out_specs=pl.BlockSpec((ROWS, D_RWKV), lambda i: (i, 0)),
        out_shape=jax.ShapeDtypeStruct((nbatch * t, D_RWKV), BF16),
        compiler_params=_cparams(("parallel",)),
        name="rwkv_out",
    )(yf, yb, r, v, kdf, kdb, gate, r_k, lnx_g, lnx_b)


def _outproj_kernel(x_ref, attn_ref, rw_ref, g1_ref, g_ref, sh_ref, sc_ref, wa_ref, wr_ref, o_ref, h_ref):
    half = x_ref.shape[0] // 2
    for rows in (slice(0, half), slice(half, 2 * half)):
        mix = _dot(attn_ref[rows, :], wa_ref[...]) + _dot(rw_ref[rows, :], wr_ref[...])
        x1 = x_ref[rows, :] + g1_ref[0] * mix
        o_ref[rows, :] = x1
        ms = jnp.mean(x1 * x1, axis=-1, keepdims=True)
        h = x1 * lax.rsqrt(ms + NORM_EPS) * g_ref[...]
        h_ref[rows, :] = (h * (1.0 + sc_ref[0]) + sh_ref[0]).astype(h_ref.dtype)


def _outproj(x2, attn, rw, g1, g, sh, sc, wa, wr, t):
    n = x2.shape[0]
    tm = 512
    modspec = pl.BlockSpec((1, 1, D_MODEL), lambda i: (i // (t // tm), 0, 0))
    rowspec = pl.BlockSpec((tm, D_MODEL), lambda i: (i, 0))
    return pl.pallas_call(
        _outproj_kernel,
        grid=(n // tm,),
        in_specs=[
            rowspec,
            pl.BlockSpec((tm, D_ATTN), lambda i: (i, 0)),
            pl.BlockSpec((tm, D_RWKV), lambda i: (i, 0)),
            modspec,
            pl.BlockSpec((1, D_MODEL), lambda i: (0, 0)),
            modspec, modspec,
            pl.BlockSpec((D_ATTN, D_MODEL), lambda i: (0, 0), pipeline_mode=pl.Buffered(1)),
            pl.BlockSpec((D_RWKV, D_MODEL), lambda i: (0, 0), pipeline_mode=pl.Buffered(1)),
        ],
        out_specs=[rowspec, rowspec],
        out_shape=[jax.ShapeDtypeStruct((n, D_MODEL), F32), jax.ShapeDtypeStruct((n, D_MODEL), BF16)],
        compiler_params=_cparams(("parallel",)),
        name="outproj",
    )(x2, attn, rw, g1, g, sh, sc, wa, wr)


def _ffn_kernel(x_ref, h_ref, g2_ref, fg_ref, w1_ref, w3_ref, w2_ref, o_ref, acc_ref):
    j = pl.program_id(1)

    @pl.when(j == 0)
    def _():
        acc_ref[...] = jnp.zeros_like(acc_ref)

    h = h_ref[...]
    a = _dot(h, w1_ref[...])
    b = _dot(h, w3_ref[...])
    u = a * _sigmoid(a) * b
    acc_ref[...] += _dot(u.astype(BF16), w2_ref[...])

    @pl.when(j == pl.num_programs(1) - 1)
    def _():
        y = x_ref[...] + g2_ref[0] * acc_ref[...]
        ms = jnp.mean(y * y, axis=-1, keepdims=True)
        o_ref[...] = y * lax.rsqrt(ms + NORM_EPS) * fg_ref[...]


def _ffn(x1, h2, g2, fg, w1, w3, w2, t):
    n = x1.shape[0]
    tm, tf = 512, 512
    modspec = pl.BlockSpec((1, 1, D_MODEL), lambda i, j: (i // (t // tm), 0, 0))
    vecspec = pl.BlockSpec((1, D_MODEL), lambda i, j: (0, 0))
    return pl.pallas_call(
        _ffn_kernel,
        grid=(n // tm, D_FF // tf),
        in_specs=[
            pl.BlockSpec((tm, D_MODEL), lambda i, j: (i, 0)),
            pl.BlockSpec((tm, D_MODEL), lambda i, j: (i, 0)),
            modspec, vecspec,
            pl.BlockSpec((D_MODEL, tf), lambda i, j: (0, j)),
            pl.BlockSpec((D_MODEL, tf), lambda i, j: (0, j)),
            pl.BlockSpec((tf, D_MODEL), lambda i, j: (j, 0)),
        ],
        out_specs=pl.BlockSpec((tm, D_MODEL), lambda i, j: (i, 0)),
        out_shape=jax.ShapeDtypeStruct((n, D_MODEL), F32),
        scratch_shapes=[pltpu.VMEM((tm, D_MODEL), F32)],
        compiler_params=_cparams(("parallel", "arbitrary")),
        name="ffn",
    )(x1, h2, g2, fg, w1, w3, w2)


def _z_weights_kernel(w_ref, o_ref):
    q0, k0, v0, r0 = 0, D_ATTN, D_ATTN + Z_KV_W, D_ATTN + 2 * Z_KV_W
    lora0 = r0 + Z_RKV_W
    n_lora = 4 * D_LORA + D_GATE_LORA
    cast = lambda x: x.astype(o_ref.dtype)
    o_ref[:, Z_R:Z_R + Z_RKV_W] = cast(w_ref[:, r0:lora0])
    o_ref[:, Z_Q:Z_Q + D_ATTN] = cast(w_ref[:, q0:q0 + D_ATTN] * (HEAD_DIM ** -0.5 * LOG2E))
    o_ref[:, Z_LORA:Z_LORA + Z_LORA_W] = jnp.zeros((o_ref.shape[0], Z_LORA_W), o_ref.dtype)
    o_ref[:, Z_LORA:Z_LORA + n_lora] = cast(w_ref[:, lora0:lora0 + n_lora])
    o_ref[:, Z_KC:Z_KC + Z_KV_W] = cast(w_ref[:, k0:k0 + Z_KV_W])
    o_ref[:, Z_VC:Z_VC + Z_KV_W] = cast(w_ref[:, v0:v0 + Z_KV_W])


def _z_weights(w_in):
    rows = 256
    return pl.pallas_call(
        _z_weights_kernel,
        grid=(D_MODEL // rows,),
        in_specs=[pl.BlockSpec((rows, w_in.shape[1]), lambda i: (i, 0))],
        out_specs=pl.BlockSpec((rows, NZ), lambda i: (i, 0)),
        out_shape=jax.ShapeDtypeStruct((D_MODEL, NZ), BF16),
        compiler_params=_cparams(("parallel",)),
        name="z_weights",
    )(w_in)


def _rope_tables(t):
    pos = np.arange(t)
    row = (pos // GRID_W).astype(np.float32)
    col = (pos % GRID_W).astype(np.float32)
    nf = HEAD_DIM // 4
    freqs = (np.float32(1.0) / np.float32(ROPE_BASE) ** (np.arange(nf, dtype=np.float32) / np.float32(nf))).astype(np.float32)
    d = np.arange(LANES) % HEAD_DIM
    use_row = (d // (HEAD_DIM // 2)) == 0
    first = (d % (HEAD_DIM // 2)) < nf
    ang = (np.where(use_row[None, :], row[:, None], col[:, None]) * freqs[d % nf][None, :]).astype(np.float32)
    sin = np.sin(ang)
    return jnp.asarray(np.cos(ang), F32), jnp.asarray(np.where(first[None, :], -sin, sin), F32)


def _pad_rows(w, top, total):
    return jnp.concatenate([jnp.zeros((top, w.shape[1]), w.dtype), w,
                            jnp.zeros((total - top - w.shape[0], w.shape[1]), w.dtype)], axis=0)


def kernel(x, c, ctx, c_ctx, ada_w, ada_b, norm1_g, norm2_g, w_in, attn_sink, ts_prev, ts_next, w0, w_up, a0, a_up,
           k_k, k_a, r_k, g_up, lnx_g, lnx_b, w_out, ffn_w1, ffn_w3, ffn_w2, final_norm_g):
    nbatch, t, d = x.shape
    nctx = ctx.shape[1]
    assert d == D_MODEL and nctx == ROWS and t % ROWS == 0 and nbatch < 8 and ada_w.shape[0] == 1
    nblk = (nctx + t) // ROWS

    cc = jnp.concatenate([c, c_ctx[None, :], jnp.zeros((8 - nbatch - 1, d), F32)], axis=0)
    mod = _adaln(cc, ada_w[0], ada_b[0])
    mods = mod[:nbatch + 1].reshape(nbatch + 1, 6, 1, d)
    sh1, sc1, g1, sh2, sc2, g2 = (mods[:, i] for i in range(6))

    x2 = x.reshape(nbatch * t, d)
    cos_t, sin_t = _rope_tables(t)
    z = _inproj(x2, ctx.reshape(nbatch * nctx, d), norm1_g, sh1, sc1, cos_t, sin_t, _z_weights(w_in[0]), nbatch, nblk)

    attn = _attention(z, attn_sink[0], nbatch, t, nctx)

    n_lora = ts_prev.shape[1] - Z_RKV_W
    pad_lora = lambda m: jnp.pad(m[:, Z_RKV_W:], ((0, 0), (0, Z_LORA_W - n_lora)))
    wupf = _pad_rows(w_up[0, 0], 0, LANES)
    wupb = _pad_rows(w_up[0, 1], D_LORA, LANES)
    aupf = _pad_rows(a_up[0, 0], 0, LANES)
    aupb = _pad_rows(a_up[0, 1], D_LORA, LANES)
    gup = _pad_rows(g_up[0], 0, 2 * LANES)
    r, v, lwf, lwb, kdf, kdb, nkk, bfw, bbw, gate = _rwkv_prep(
        z, ts_prev[:, :Z_RKV_W], ts_next[:, :Z_RKV_W], pad_lora(ts_prev), pad_lora(ts_next),
        w0[0], wupf, wupb, a0[0], aupf, aupb, k_k, k_a, gup, nblk)
    yf, yb = _wkv(r, v, nkk, lwf, kdf, bfw, lwb, kdb, bbw, nbatch, nblk)
    rw = _rwkv_out(yf, yb, r, v, kdf, kdb, gate, r_k, lnx_g, lnx_b, nbatch, t, nblk)

    wo = w_out[0].astype(BF16)
    x1, h2 = _outproj(x2, attn, rw, g1, norm2_g, sh2, sc2, wo[:D_ATTN], wo[D_ATTN:], t)
    out = _ffn(x1, h2, g2, final_norm_g.reshape(1, d),
               ffn_w1[0].astype(BF16), ffn_w3[0].astype(BF16), ffn_w2[0].astype(BF16), t)
    return out.reshape(nbatch, t, d)
```

```python
import functools
import math

import numpy as np
import jax
import jax.numpy as jnp
from jax import lax
from jax.experimental import pallas as pl
from jax.experimental.pallas import tpu as pltpu

F32 = jnp.float32
BF16 = jnp.bfloat16
HI = lax.Precision.HIGHEST

D_MODEL = 2048
HEAD_DIM = 64
N_HEADS_A = 16
N_KV_HEADS = 4
GQA_GROUP = 4
D_ATTN = 1024
D_RWKV = 1024
D_LORA = 64
D_GATE_LORA = 160
D_FF = 5632
GRID_W = 64
WINDOW = 128
BLOCK = 128
ROPE_BASE = 10000.0
NORM_EPS = 1e-6
LNX_EPS = 64e-5
MASK_VALUE = -1e30

LANES = 128
ROWS = 256
CHUNK = 64
WKV_PAIRS = 4
ATT_HEADS = 4
ATT_GROUP = 4
LOG2E = math.log2(math.e)
VMEM_LIMIT = 56 * 1024 * 1024

Z_R, Z_K, Z_V = 0, 1024, 2048
Z_RKV_W = 3072
Z_Q = 3072
Z_LORA = 4096
Z_LORA_W = 512
Z_KV_W = N_KV_HEADS * HEAD_DIM
Z_KC, Z_VC = 4608, 4864
NZ = 5120
ROPE_SLABS = tuple(range(Z_Q, Z_Q + D_ATTN, LANES)) + tuple(range(Z_KC, Z_VC, LANES))
PROJ_TN = 512


def _cparams(sem):
    return pltpu.CompilerParams(dimension_semantics=sem, vmem_limit_bytes=VMEM_LIMIT)


def _sigmoid(x):
    return 0.5 * jnp.tanh(0.5 * x) + 0.5


def _dot(a, b, precision=None):
    return jnp.dot(a, b, preferred_element_type=F32, precision=precision)


def _dot_nt(a, b, precision=None):
    return lax.dot_general(a, b, (((1,), (1,)), ((), ())), preferred_element_type=F32, precision=precision)


def _split(x, terms):
    out = []
    for _ in range(terms - 1):
        hi = x.astype(BF16)
        out.append(hi)
        x = x - hi.astype(F32)
    out.append(x.astype(BF16))
    return out


def _mm(a, b, passes, nt=False):
    d = _dot_nt if nt else _dot
    if passes == 1:
        return d(a.astype(BF16), b.astype(BF16))
    if passes == 3:
        ah, al = _split(a, 2)
        bh, bl = _split(b, 2)
        return d(ah, bh) + (d(ah, bl) + d(al, bh))
    return d(a, b, HI)


def _adaln_kernel(c_ref, w_ref, b_ref, o_ref):
    c = c_ref[...]
    s = c * _sigmoid(c)
    o_ref[...] = _mm(s, w_ref[...], 3) + b_ref[...]


def _adaln(cc, w, b):
    n = w.shape[1]
    tn = 1024
    return pl.pallas_call(
        _adaln_kernel,
        grid=(n // tn,),
        in_specs=[
            pl.BlockSpec((8, D_MODEL), lambda j: (0, 0)),
            pl.BlockSpec((D_MODEL, tn), lambda j: (0, j)),
            pl.BlockSpec((1, tn), lambda j: (0, j)),
        ],
        out_specs=pl.BlockSpec((8, tn), lambda j: (0, j)),
        out_shape=jax.ShapeDtypeStruct((8, n), F32),
        compiler_params=_cparams(("parallel",)),
        name="adaln",
    )(cc, w, b.reshape(1, n))


def _inproj_kernel(x_ref, ctx_ref, g_ref, sh_ref, sc_ref, cos_ref, sin_ref, w_ref, z_ref, shw_ref, *, nblk):
    li = pl.program_id(0) % nblk
    is_ctx = li == 0

    @pl.when(li <= 1)
    def _():
        shb = jnp.broadcast_to(sh_ref[0], (8, D_MODEL)).astype(BF16)
        for j in range(NZ // PROJ_TN):
            sl = slice(j * PROJ_TN, (j + 1) * PROJ_TN)
            shw_ref[:, sl] = _dot(shb, w_ref[:, sl])

    x = jnp.where(is_ctx, ctx_ref[...], x_ref[...])
    xb = (x * (g_ref[...] * (1.0 + sc_ref[0]))).astype(BF16)
    rs = lax.rsqrt(jnp.mean(x * x, axis=-1, keepdims=True) + NORM_EPS)
    cs = cos_ref[...]
    sn = sin_ref[...]
    first = (lax.broadcasted_iota(jnp.int32, (ROWS, LANES), 1) % 32) < 16
    for j in range(NZ // PROJ_TN):
        sl = slice(j * PROJ_TN, (j + 1) * PROJ_TN)
        zt = _dot(xb, w_ref[:, sl]) * rs + shw_ref[0:1, sl]
        for c0 in range(j * PROJ_TN, (j + 1) * PROJ_TN, LANES):
            zz = zt[:, c0 - j * PROJ_TN:c0 - j * PROJ_TN + LANES]
            if c0 in ROPE_SLABS:
                partner = jnp.where(first, pltpu.roll(zz, LANES - 16, 1), pltpu.roll(zz, 16, 1))
                zz = jnp.where(is_ctx, zz, zz * cs + partner * sn)
            z_ref[:, c0:c0 + LANES] = zz


def _inproj(x2, ctx2, g, sh, sc, cos_t, sin_t, wz, nbatch, nblk):
    n = nbatch * nblk * ROWS
    lpb = nblk - 1

    def sel(i):
        return jnp.where(i % nblk == 0, nbatch, i // nblk)

    return pl.pallas_call(
        functools.partial(_inproj_kernel, nblk=nblk),
        grid=(n // ROWS,),
        in_specs=[
            pl.BlockSpec((ROWS, D_MODEL), lambda i: ((i // nblk) * lpb + jnp.maximum(i % nblk - 1, 0), 0)),
            pl.BlockSpec((ROWS, D_MODEL), lambda i: (i // nblk, 0)),
            pl.BlockSpec((1, D_MODEL), lambda i: (0, 0)),
            pl.BlockSpec((1, 1, D_MODEL), lambda i: (sel(i), 0, 0)),
            pl.BlockSpec((1, 1, D_MODEL), lambda i: (sel(i), 0, 0)),
            pl.BlockSpec((ROWS, LANES), lambda i: (jnp.maximum(i % nblk - 1, 0), 0)),
            pl.BlockSpec((ROWS, LANES), lambda i: (jnp.maximum(i % nblk - 1, 0), 0)),
            pl.BlockSpec((D_MODEL, NZ), lambda i: (0, 0), pipeline_mode=pl.Buffered(1)),
        ],
        out_specs=pl.BlockSpec((ROWS, NZ), lambda i: (i, 0)),
        out_shape=jax.ShapeDtypeStruct((n, NZ), F32),
        scratch_shapes=[pltpu.VMEM((8, NZ), F32)],
        compiler_params=_cparams(("arbitrary",)),
        name="inproj",
    )(x2, ctx2, g, sh, sc, cos_t, sin_t, wz)


def _attn_kernel(sink_ref, q_ref, kp_ref, kc_ref, kn_ref, vp_ref, vc_ref, vn_ref, kx_ref, vx_ref, o_ref, *, nb):
    i = pl.program_id(1)
    hg = pl.program_id(2)
    nctx = kx_ref.shape[0]
    nk = 3 * BLOCK + nctx
    lane = lax.broadcasted_iota(jnp.int32, (BLOCK, LANES), 1)
    lo = lane < HEAD_DIM
    heads = range(ATT_HEADS)

    qoff = lax.broadcasted_iota(jnp.int32, (GQA_GROUP * BLOCK, BLOCK), 0) % BLOCK
    koff = lax.broadcasted_iota(jnp.int32, (GQA_GROUP * BLOCK, BLOCK), 1)
    see_prev = jnp.where(i > 0, jnp.where(koff >= qoff, 1, 0), 0) > 0
    see_next = jnp.where(i < nb - 1, jnp.where(koff <= qoff, 1, 0), 0) > 0

    def masked(s):
        return jnp.concatenate([jnp.where(see_prev, s[:, :BLOCK], MASK_VALUE), s[:, BLOCK:2 * BLOCK],
                                jnp.where(see_next, s[:, 2 * BLOCK:3 * BLOCK], MASK_VALUE), s[:, 3 * BLOCK:]], axis=1)
    rowc = lax.broadcasted_iota(jnp.int32, (GQA_GROUP * BLOCK, 1), 0)

    order = (0, 2, 1, 3)

    def queries(h):
        parts = []
        for g in order:
            c0 = (h * GQA_GROUP + g) // 2 * LANES
            qp = q_ref[:, c0:c0 + LANES]
            parts.append(jnp.where(lo, qp, 0.0) if g % 2 == 0 else jnp.where(lo, 0.0, qp))
        return jnp.concatenate(parts, axis=0).astype(BF16)

    rolled = {}

    def slab_of(r, h):
        key = (id(r), h // 2)
        if key not in rolled:
            x = r[:, (h // 2) * LANES:(h // 2 + 1) * LANES]
            rolled[key] = (x, pltpu.roll(x, HEAD_DIM, 1))
        x, xr = rolled[key]
        return (x, xr) if h % 2 == 0 else (xr, x)

    def keys_of(h):
        parts = []
        for r in (kp_ref, kc_ref, kn_ref, kx_ref):
            on_lo, on_hi = slab_of(r, h)
            parts.append(jnp.where(lax.broadcasted_iota(jnp.int32, on_lo.shape, 1) < HEAD_DIM, on_lo, on_hi))
        return jnp.concatenate(parts, axis=0).astype(BF16)

    def values_of(h):
        va, vb = [], []
        for r in (vp_ref, vc_ref, vn_ref, vx_ref):
            on_lo, on_hi = slab_of(r, h)
            lo_r = lax.broadcasted_iota(jnp.int32, on_lo.shape, 1) < HEAD_DIM
            va.append(jnp.where(lo_r, on_lo, 1.0))
            vb.append(jnp.where(lo_r, 1.0, on_hi))
        return jnp.concatenate(va, axis=0).astype(BF16), jnp.concatenate(vb, axis=0).astype(BF16)

    def sinks(h):
        base = (hg * ATT_HEADS + h) * GQA_GROUP
        return LOG2E * jnp.where(rowc < BLOCK, sink_ref[base + order[0]],
                                 jnp.where(rowc < 2 * BLOCK, sink_ref[base + order[1]],
                                           jnp.where(rowc < 3 * BLOCK, sink_ref[base + order[2]],
                                                     sink_ref[base + order[3]])))

    half = 2 * BLOCK
    for g0 in range(0, ATT_HEADS, ATT_GROUP):
        heads = range(g0, g0 + ATT_GROUP)
        ss = [masked(_dot_nt(queries(h), keys_of(h))) for h in heads]
        sks = [sinks(h) for h in heads]
        ms = [jnp.maximum(jnp.max(s, axis=-1, keepdims=True), sk) for s, sk in zip(ss, sks)]
        ps = [jnp.exp2(s - m).astype(BF16) for s, m in zip(ss, ms)]
        extra = [jnp.exp2(sk - m) for sk, m in zip(sks, ms)]
        vals = [values_of(h) for h in heads]
        oa = [_dot(p[:half], v[0]) for p, v in zip(ps, vals)]
        ob = [_dot(p[half:], v[1]) for p, v in zip(ps, vals)]
        for h, a, b, e in zip(heads, oa, ob, extra):
            a = a / (pltpu.roll(a, HEAD_DIM, 1) + e[:half])
            b = b / (pltpu.roll(b, HEAD_DIM, 1) + e[half:])
            o_ref[:, 2 * h * LANES:(2 * h + 1) * LANES] = jnp.where(lo, a[:BLOCK], b[:BLOCK]).astype(o_ref.dtype)
            o_ref[:, (2 * h + 1) * LANES:(2 * h + 2) * LANES] = jnp.where(lo, a[BLOCK:], b[BLOCK:]).astype(o_ref.dtype)


def _attention(z, sink, nbatch, t, nctx):
    nb = t // BLOCK
    rpb = (nctx + t) // BLOCK
    cpb = nctx // BLOCK

    def lat(off):
        def f(b, i, h):
            return b * rpb + cpb + jnp.clip(i + off, 0, nb - 1)
        return f

    kw = ATT_HEADS * HEAD_DIM
    qw = ATT_HEADS * GQA_GROUP * HEAD_DIM

    def kspec(off, col0):
        return pl.BlockSpec((BLOCK, kw), lambda b, i, h: (lat(off)(b, i, h), col0 // kw + h))

    def xspec(col0):
        return pl.BlockSpec((nctx, kw), lambda b, i, h: (b * (rpb * BLOCK // nctx), col0 // kw + h))

    kcol, vcol = Z_KC, Z_VC
    return pl.pallas_call(
        functools.partial(_attn_kernel, nb=nb),
        grid=(nbatch, nb, N_KV_HEADS // ATT_HEADS),
        in_specs=[
            pl.BlockSpec(memory_space=pltpu.SMEM),
            pl.BlockSpec((BLOCK, qw), lambda b, i, h: (lat(0)(b, i, h), Z_Q // qw + h)),
            kspec(-1, kcol), kspec(0, kcol), kspec(1, kcol),
            kspec(-1, vcol), kspec(0, vcol), kspec(1, vcol),
            xspec(kcol), xspec(vcol),
        ],
        out_specs=pl.BlockSpec((BLOCK, qw), lambda b, i, h: (b * nb + i, h)),
        out_shape=jax.ShapeDtypeStruct((nbatch * t, D_ATTN), BF16),
        compiler_params=_cparams(("parallel", "parallel", "parallel")),
        name="attention",
    )(sink, z, z, z, z, z, z, z, z, z)


def _head_ones():
    r = lax.broadcasted_iota(jnp.int32, (LANES, LANES), 0)
    c = lax.broadcasted_iota(jnp.int32, (LANES, LANES), 1)
    return jnp.where((r // HEAD_DIM) == (c // HEAD_DIM), 1.0, 0.0).astype(F32)


def _prep_kernel(zm_ref, zp_ref, zn_ref, lm_ref, lp_ref, ln_ref, mup_ref, mun_ref, lmup_ref, lmun_ref,
                 w0_ref, wupf_ref, wupb_ref, a0_ref, aupf_ref, aupb_ref, kk_ref, ka_ref, gup_ref,
                 r_o, v_o, lwf_o, lwb_o, kdf_o, kdb_o, nkk_o, bf_o, bb_o, gate_o, *, nblk):
    li = pl.program_id(0) % nblk
    keep_p = jnp.where((li == 0) | (li == 1), 0.0, 1.0)
    keep_n = jnp.where((li == 0) | (li == nblk - 1), 0.0, 1.0)
    row8 = lax.broadcasted_iota(jnp.int32, (8, LANES), 0)

    def shift(refs, c0):
        m_ref, p_ref, n_ref, mp_ref, mn_ref = refs
        sl = slice(c0, c0 + LANES)
        zc = m_ref[:, sl]
        down = pltpu.roll(zc, 1, 0)
        up = pltpu.roll(zc, ROWS - 1, 0)
        prev = jnp.concatenate([jnp.where(row8 == 0, p_ref[7:8, sl] * keep_p, down[:8]), down[8:]], axis=0)
        nxt = jnp.concatenate([up[:-8], jnp.where(row8 == 7, n_ref[0:1, sl] * keep_n, up[-8:])], axis=0)
        mp, mn = mp_ref[:, sl], mn_ref[:, sl]
        return zc * (1.0 - mp - mn) + mp * prev + mn * nxt

    shifted = functools.partial(shift, (zm_ref, zp_ref, zn_ref, mup_ref, mun_ref))
    lora = functools.partial(shift, (lm_ref, lp_ref, ln_ref, lmup_ref, lmun_ref))
    ones = _head_ones()
    th_w = jnp.tanh(lora(0))
    ad = lora(LANES)
    sg = jnp.concatenate([_sigmoid(lora(2 * LANES)), _sigmoid(lora(3 * LANES))], axis=1)

    def log_decay(pre):
        return -math.exp(-0.5) * _sigmoid(pre)

    th_b = th_w.astype(BF16)
    ad_b = ad.astype(BF16)
    sg_b = sg.astype(BF16)

    def lora_fine(xb, w):
        hi, lo = _split(w, 2)
        both = _dot(xb, jnp.concatenate([hi, lo], axis=1))
        return both[:, :LANES] + both[:, LANES:]

    for s in range(D_RWKV // LANES):
        sl = slice(s * LANES, (s + 1) * LANES)
        r = shifted(Z_R + s * LANES)
        k = shifted(Z_K + s * LANES)
        v = shifted(Z_V + s * LANES)
        r_o[:, sl] = r.astype(r_o.dtype)
        v_o[:, sl] = v.astype(v_o.dtype)
        kk = k * kk_ref[:, sl]
        ss = _mm(kk * kk, ones, 1)
        kk = kk * lax.rsqrt(jnp.maximum(ss, 1e-12))
        nkk_o[:, sl] = (-kk).astype(nkk_o.dtype)
        lwf_o[:, sl] = log_decay(w0_ref[0:1, sl] + lora_fine(th_b, wupf_ref[:, sl]))
        lwb_o[:, sl] = log_decay(w0_ref[1:2, sl] + lora_fine(th_b, wupb_ref[:, sl]))
        a_pre = _dot(ad_b, jnp.concatenate([aupf_ref[:, sl], aupb_ref[:, sl]], axis=1).astype(BF16))
        af = _sigmoid(a0_ref[0:1, sl] + a_pre[:, :LANES])
        ab = _sigmoid(a0_ref[1:2, sl] + a_pre[:, LANES:])
        ka = ka_ref[:, sl]
        kdf_o[:, sl] = (k * (1.0 + (af - 1.0) * ka)).astype(kdf_o.dtype)
        kdb_o[:, sl] = (k * (1.0 + (ab - 1.0) * ka)).astype(kdb_o.dtype)
        bf_o[:, sl] = (kk * af).astype(bf_o.dtype)
        bb_o[:, sl] = (kk * ab).astype(bb_o.dtype)
        gate_o[:, sl] = _dot(sg_b, gup_ref[:, sl].astype(BF16)).astype(gate_o.dtype)


def _rwkv_prep(z, mup, mun, lmup, lmun, w0, wupf, wupb, a0, aupf, aupb, k_k, k_a, gup, nblk):
    n = z.shape[0]
    nsub = ROWS // 8
    full = lambda a: pl.BlockSpec(a.shape, lambda i: (0,) * a.ndim)
    outs = [jax.ShapeDtypeStruct((n, D_RWKV), F32 if name in ("lwf", "lwb") else BF16)
            for name in ("r", "v", "lwf", "lwb", "kdf", "kdb", "nkk", "bf", "bb", "gate")]
    ospec = pl.BlockSpec((ROWS, D_RWKV), lambda i: (i, 0))
    return pl.pallas_call(
        functools.partial(_prep_kernel, nblk=nblk),
        grid=(n // ROWS,),
        in_specs=[
            pl.BlockSpec((ROWS, Z_RKV_W), lambda i: (i, 0)),
            pl.BlockSpec((8, Z_RKV_W), lambda i: (jnp.maximum(i * nsub - 1, 0), 0)),
            pl.BlockSpec((8, Z_RKV_W), lambda i: (jnp.minimum((i + 1) * nsub, n // 8 - 1), 0)),
            pl.BlockSpec((ROWS, Z_LORA_W), lambda i: (i, Z_LORA // Z_LORA_W)),
            pl.BlockSpec((8, Z_LORA_W), lambda i: (jnp.maximum(i * nsub - 1, 0), Z_LORA // Z_LORA_W)),
            pl.BlockSpec((8, Z_LORA_W), lambda i: (jnp.minimum((i + 1) * nsub, n // 8 - 1), Z_LORA // Z_LORA_W)),
            full(mup), full(mun), full(lmup), full(lmun),
            full(w0), full(wupf), full(wupb), full(a0), full(aupf), full(aupb),
            full(k_k), full(k_a), full(gup),
        ],
        out_specs=[ospec] * 10,
        out_shape=outs,
        compiler_params=_cparams(("parallel",)),
        name="rwkv_prep",
    )(z, z, z, z, z, z, mup, mun, lmup, lmun, w0, wupf, wupb, a0, aupf, aupb, k_k, k_a, gup)


P_GRAM, P_INV, P_SOLVE, P_READ, P_STATE, P_OUT, P_CARRY = 1, 1, 1, 1, 1, 1, 1


def _wkv_intra(probs, bd, eye_bd, head_a, head_a2):
    n = CHUNK
    zero = jnp.zeros((n, LANES), F32)
    each = lambda f, *ls: [f(*xs) for xs in zip(*ls)]

    def stack(x):
        return jnp.concatenate([jnp.where(head_a, x, 0.0), jnp.where(head_a, 0.0, x)], axis=0)

    def cumulative(p):
        inclb = p[6].astype(BF16)
        l0, l1, l2 = _split(p[1], 3)
        return _dot(inclb, l0) + (_dot(inclb, l1) + _dot(inclb, l2))

    cs = each(cumulative, probs)

    def factors(p, c):
        r, lw, k, v, a, b = p[:6]
        ctot = jnp.sum(lw, axis=0, keepdims=True)
        mid = 0.5 * ctot
        e_in = jnp.exp(c - mid)
        e_out = jnp.exp(mid - c)
        e_abs = jnp.exp(c)
        e_rest = jnp.exp(ctot - c)
        undo = jnp.exp(-lw)
        lhs = jnp.concatenate([stack(a * e_in * undo), stack(r * e_in)], axis=0)
        rhs = jnp.concatenate([b * e_out, k * e_out], axis=0)
        rest = jnp.concatenate([b * e_rest, k * e_rest], axis=0)
        return lhs, rhs, rest, a * e_abs * undo, r * e_abs, jnp.exp(ctot)

    fs = each(factors, probs, cs)
    grams = [_mm(f[0], f[1], P_GRAM, nt=True) for f in fs]
    aas = [jnp.where(p[7], g[:2 * n], 0.0) for p, g in zip(probs, grams)]
    ars = [jnp.where(p[8], g[2 * n:], 0.0) for p, g in zip(probs, grams)]
    a_abs = [jnp.where(bd, jnp.concatenate([aa[:n], pltpu.roll(aa[n:], n, 1)], axis=0), 0.0) for aa in aas]

    invs = [eye_bd + x for x in a_abs]
    pws = [_mm(x, x, P_INV) for x in a_abs]
    for _ in range(int(np.log2(n)) - 2):
        boths = [_mm(jnp.concatenate([pw, inv], axis=0), pw, P_INV) for pw, inv in zip(pws, invs)]
        pws = [x[:2 * n] for x in boths]
        invs = [inv + x[2 * n:] for inv, x in zip(invs, boths)]
    invs = [inv + _mm(inv, pw, P_INV) for inv, pw in zip(invs, pws)]

    akvs = [jnp.where(bd, _mm(aa, jnp.concatenate([zero, p[3]], axis=0), P_SOLVE), 0.0)
            for aa, p in zip(aas, probs)]
    wu_sts = [_mm(inv, jnp.concatenate([stack(f[3]), akv], axis=1), P_SOLVE)
              for inv, f, akv in zip(invs, fs, akvs)]
    m2s = [jnp.concatenate([x[:n] + x[n:], jnp.concatenate([zero, p[3]], axis=1)], axis=0)
           for x, p in zip(wu_sts, probs)]
    qy_sts = [_mm(ar, m2, P_READ) for ar, m2 in zip(ars, m2s)]
    ghs = [_mm(m2.T, f[2], P_STATE) for m2, f in zip(m2s, fs)]
    out = []
    for f, qy_st, gh in zip(fs, qy_sts, ghs):
        qy = jnp.where(head_a2, qy_st[:n], qy_st[n:])
        out.append((f[4] + qy[:, :LANES], qy[:, LANES:],
                    jnp.where(bd, gh[:LANES], 0.0) + eye_bd * f[5], jnp.where(bd, gh[LANES:], 0.0)))
    return out


def _wkv_kernel(rf, vf, af, lwf, kf, bf, rb, vb, ab, lwb, kb, bb, yf_o, yb_o, s_ref, q_ref, y0_ref, gm_ref, h_ref):
    j = pl.program_id(1)
    n = CHUNK
    nsub = ROWS // n
    cur = j % 2
    prv = 1 - cur

    @pl.when(j == 0)
    def _():
        s_ref[...] = jnp.zeros_like(s_ref)
        q_ref[1] = jnp.zeros(q_ref.shape[1:], F32)
        y0_ref[1] = jnp.zeros(y0_ref.shape[1:], F32)
        gm_ref[1] = jnp.zeros(gm_ref.shape[1:], F32)
        h_ref[1] = jnp.zeros(h_ref.shape[1:], F32)

    for step in range(nsub):
        for d, y_o in enumerate((yf_o, yb_o)):
            for p in range(WKV_PAIRS):
                u = step if d == 0 else nsub - 1 - step
                idx = (d * WKV_PAIRS + p) * nsub + u
                s0 = s_ref[d * WKV_PAIRS + p]
                y_o[u * n:(u + 1) * n, p * LANES:(p + 1) * LANES] = (
                    _mm(q_ref[prv, idx], s0, P_OUT, nt=True) + y0_ref[prv, idx]).astype(y_o.dtype)
                s_ref[d * WKV_PAIRS + p] = _mm(s0, gm_ref[prv, idx], P_CARRY) + h_ref[prv, idx]

    ti = lax.broadcasted_iota(jnp.int32, (n, n), 0)
    si = lax.broadcasted_iota(jnp.int32, (n, n), 1)
    t2 = lax.broadcasted_iota(jnp.int32, (2 * n, 2 * n), 0)
    s2 = lax.broadcasted_iota(jnp.int32, (2 * n, 2 * n), 1)
    tw, sw = t2 % n, s2 % n
    bd = (t2 // n) == (s2 // n)
    eye_bd = jnp.where(t2 == s2, 1.0, 0.0).astype(F32)
    head_a = lax.broadcasted_iota(jnp.int32, (n, LANES), 1) < HEAD_DIM
    head_a2 = lax.broadcasted_iota(jnp.int32, (n, 2 * LANES), 1) % LANES < HEAD_DIM
    masks = (
        (jnp.where(si <= ti, 1.0, 0.0).astype(F32), sw < tw, sw <= tw),
        (jnp.where(si >= ti, 1.0, 0.0).astype(F32), sw > tw, sw >= tw),
    )
    srcs = ((rf, lwf, kf, vf, af, bf), (rb, lwb, kb, vb, ab, bb))

    probs = [tuple(x[u * n:(u + 1) * n, p * LANES:(p + 1) * LANES].astype(F32) for x in srcs[d]) + masks[d]
             for d in range(2) for p in range(WKV_PAIRS) for u in range(nsub)]
    for idx, (q, y0, gm, h) in enumerate(_wkv_intra(probs, bd, eye_bd, head_a, head_a2)):
        q_ref[cur, idx] = q
        y0_ref[cur, idx] = y0
        gm_ref[cur, idx] = gm
        h_ref[cur, idx] = h


def _wkv(r, v, nkk, lwf, kdf, bfw, lwb, kdb, bbw, nbatch, nblk):
    n = r.shape[0]
    ngrp = D_RWKV // (WKV_PAIRS * LANES)
    nprob = 2 * WKV_PAIRS * (ROWS // CHUNK)
    last = nblk - 1
    width = WKV_PAIRS * LANES

    def bwd_blk(j):
        return jnp.where(j == 0, 0, nblk - j)

    def spec(blk):
        return pl.BlockSpec((ROWS, width), lambda g, j: ((g // ngrp) * nblk + blk(j), g % ngrp))

    fin = spec(lambda j: jnp.minimum(j, last))
    bin_ = spec(lambda j: bwd_blk(jnp.minimum(j, last)))
    fout = spec(lambda j: jnp.maximum(j - 1, 0))
    bout = spec(lambda j: bwd_blk(jnp.maximum(j - 1, 0)))
    out = jax.ShapeDtypeStruct((n, D_RWKV), BF16)
    return pl.pallas_call(
        _wkv_kernel,
        grid=(nbatch * ngrp, nblk + 1),
        in_specs=[fin] * 6 + [bin_] * 6,
        out_specs=[fout, bout],
        out_shape=[out, out],
        scratch_shapes=[pltpu.VMEM((2 * WKV_PAIRS, LANES, LANES), F32),
                        pltpu.VMEM((2, nprob, CHUNK, LANES), F32), pltpu.VMEM((2, nprob, CHUNK, LANES), F32),
                        pltpu.VMEM((2, nprob, LANES, LANES), F32), pltpu.VMEM((2, nprob, LANES, LANES), F32)],
        compiler_params=_cparams(("parallel", "arbitrary")),
        name="wkv",
    )(r, v, nkk, lwf, kdf, bfw, r, v, nkk, lwb, kdb, bbw)


def _rwkv_out_kernel(yf_ref, yb_ref, r_ref, v_ref, kdf_ref, kdb_ref, gate_ref, rk_ref, lg_ref, lb_ref, o_ref):
    ones = _head_ones().astype(BF16)
    mean_mat = (_head_ones() * (1.0 / HEAD_DIM)).astype(BF16)
    slabs = [slice(s * LANES, (s + 1) * LANES) for s in range(D_RWKV // LANES)]
    n = yf_ref.shape[0]

    def head_mean(y):
        hi, lo = _split(y, 2)
        both = _dot(jnp.concatenate([hi, lo], axis=0), mean_mat)
        return both[:n] + both[n:]

    ys = [yf_ref[:, sl].astype(F32) + yb_ref[:, sl].astype(F32) for sl in slabs]
    ycs = [y - head_mean(y) for y in ys]
    variances = [_dot((yc * yc).astype(BF16), mean_mat) for yc in ycs]
    sums = [_dot((r_ref[:, sl].astype(F32) * (kdf_ref[:, sl].astype(F32) + kdb_ref[:, sl].astype(F32))
                  * rk_ref[:, sl]).astype(BF16), ones) for sl in slabs]
    for sl, yc, var, sm in zip(slabs, ycs, variances, sums):
        yn = yc * lax.rsqrt(var + LNX_EPS) * lg_ref[:, sl] + lb_ref[:, sl]
        bonus = sm * v_ref[:, sl].astype(F32)
        o_ref[:, sl] = ((yn + bonus) * gate_ref[:, sl].astype(F32)).astype(o_ref.dtype)


def _rwkv_out(yf, yb, r, v, kdf, kdb, gate, r_k, lnx_g, lnx_b, nbatch, t, nblk):
    lpb = t // ROWS
    ispec = pl.BlockSpec((ROWS, D_RWKV), lambda i: ((i // lpb) * nblk + 1 + i % lpb, 0))
    pspec = pl.BlockSpec((1, D_RWKV), lambda i: (0, 0))
    return pl.pallas_call(
        _rwkv_out_kernel,
        grid=(nbatch * lpb,),
        in_specs=[ispec] * 7 + [pspec] * 3,
        out_specs=pl.BlockSpec((ROWS, D_RWKV), lambda i: (i, 0)),
        out_shape=jax.ShapeDtypeStruct((nbatch * t, D_RWKV), BF16),
        compiler_params=_cparams(("parallel",)),
        name="rwkv_out",
    )(yf, yb, r, v, kdf, kdb, gate, r_k, lnx_g, lnx_b)


def _outproj_kernel(x_ref, attn_ref, rw_ref, g1_ref, g_ref, sh_ref, sc_ref, wa_ref, wr_ref, o_ref, h_ref):
    half = x_ref.shape[0] // 2
    for rows in (slice(0, half), slice(half, 2 * half)):
        mix = _dot(attn_ref[rows, :], wa_ref[...]) + _dot(rw_ref[rows, :], wr_ref[...])
        x1 = x_ref[rows, :] + g1_ref[0] * mix
        o_ref[rows, :] = x1
        ms = jnp.mean(x1 * x1, axis=-1, keepdims=True)
        h = x1 * lax.rsqrt(ms + NORM_EPS) * g_ref[...]
        h_ref[rows, :] = (h * (1.0 + sc_ref[0]) + sh_ref[0]).astype(h_ref.dtype)


def _outproj(x2, attn, rw, g1, g, sh, sc, wa, wr, t):
    n = x2.shape[0]
    tm = 512
    modspec = pl.BlockSpec((1, 1, D_MODEL), lambda i: (i // (t // tm), 0, 0))
    rowspec = pl.BlockSpec((tm, D_MODEL), lambda i: (i, 0))
    return pl.pallas_call(
        _outproj_kernel,
        grid=(n // tm,),
        in_specs=[
            rowspec,
            pl.BlockSpec((tm, D_ATTN), lambda i: (i, 0)),
            pl.BlockSpec((tm, D_RWKV), lambda i: (i, 0)),
            modspec,
            pl.BlockSpec((1, D_MODEL), lambda i: (0, 0)),
            modspec, modspec,
            pl.BlockSpec((D_ATTN, D_MODEL), lambda i: (0, 0), pipeline_mode=pl.Buffered(1)),
            pl.BlockSpec((D_RWKV, D_MODEL), lambda i: (0, 0), pipeline_mode=pl.Buffered(1)),
        ],
        out_specs=[rowspec, rowspec],
        out_shape=[jax.ShapeDtypeStruct((n, D_MODEL), F32), jax.ShapeDtypeStruct((n, D_MODEL), BF16)],
        compiler_params=_cparams(("parallel",)),
        name="outproj",
    )(x2, attn, rw, g1, g, sh, sc, wa, wr)


def _ffn_kernel(x_ref, h_ref, g2_ref, fg_ref, w1_ref, w3_ref, w2_ref, o_ref, acc_ref):
    j = pl.program_id(1)

    @pl.when(j == 0)
    def _():
        acc_ref[...] = jnp.zeros_like(acc_ref)

    h = h_ref[...]
    a = _dot(h, w1_ref[...])
    b = _dot(h, w3_ref[...])
    u = a * _sigmoid(a) * b
    acc_ref[...] += _dot(u.astype(BF16), w2_ref[...])

    @pl.when(j == pl.num_programs(1) - 1)
    def _():
        y = x_ref[...] + g2_ref[0] * acc_ref[...]
        ms = jnp.mean(y * y, axis=-1, keepdims=True)
        o_ref[...] = y * lax.rsqrt(ms + NORM_EPS) * fg_ref[...]


def _ffn(x1, h2, g2, fg, w1, w3, w2, t):
    n = x1.shape[0]
    tm, tf = 512, 512
    modspec = pl.BlockSpec((1, 1, D_MODEL), lambda i, j: (i // (t // tm), 0, 0))
    vecspec = pl.BlockSpec((1, D_MODEL), lambda i, j: (0, 0))
    return pl.pallas_call(
        _ffn_kernel,
        grid=(n // tm, D_FF // tf),
        in_specs=[
            pl.BlockSpec((tm, D_MODEL), lambda i, j: (i, 0)),
            pl.BlockSpec((tm, D_MODEL), lambda i, j: (i, 0)),
            modspec, vecspec,
            pl.BlockSpec((D_MODEL, tf), lambda i, j: (0, j)),
            pl.BlockSpec((D_MODEL, tf), lambda i, j: (0, j)),
            pl.BlockSpec((tf, D_MODEL), lambda i, j: (j, 0)),
        ],
        out_specs=pl.BlockSpec((tm, D_MODEL), lambda i, j: (i, 0)),
        out_shape=jax.ShapeDtypeStruct((n, D_MODEL), F32),
        scratch_shapes=[pltpu.VMEM((tm, D_MODEL), F32)],
        compiler_params=_cparams(("parallel", "arbitrary")),
        name="ffn",
    )(x1, h2, g2, fg, w1, w3, w2)


def _z_weights_kernel(w_ref, o_ref):
    q0, k0, v0, r0 = 0, D_ATTN, D_ATTN + Z_KV_W, D_ATTN + 2 * Z_KV_W
    lora0 = r0 + Z_RKV_W
    n_lora = 4 * D_LORA + D_GATE_LORA
    cast = lambda x: x.astype(o_ref.dtype)
    o_ref[:, Z_R:Z_R + Z_RKV_W] = cast(w_ref[:, r0:lora0])
    o_ref[:, Z_Q:Z_Q + D_ATTN] = cast(w_ref[:, q0:q0 + D_ATTN] * (HEAD_DIM ** -0.5 * LOG2E))
    o_ref[:, Z_LORA:Z_LORA + Z_LORA_W] = jnp.zeros((o_ref.shape[0], Z_LORA_W), o_ref.dtype)
    o_ref[:, Z_LORA:Z_LORA + n_lora] = cast(w_ref[:, lora0:lora0 + n_lora])
    o_ref[:, Z_KC:Z_KC + Z_KV_W] = cast(w_ref[:, k0:k0 + Z_KV_W])
    o_ref[:, Z_VC:Z_VC + Z_KV_W] = cast(w_ref[:, v0:v0 + Z_KV_W])


def _z_weights(w_in):
    rows = 256
    return pl.pallas_call(
        _z_weights_kernel,
        grid=(D_MODEL // rows,),
        in_specs=[pl.BlockSpec((None, rows, w_in.shape[2]), lambda i: (0, i, 0))],
        out_specs=pl.BlockSpec((rows, NZ), lambda i: (i, 0)),
        out_shape=jax.ShapeDtypeStruct((D_MODEL, NZ), BF16),
        compiler_params=_cparams(("parallel",)),
        name="z_weights",
    )(w_in)


def _rope_tables(t):
    pos = np.arange(t)
    row = (pos // GRID_W).astype(np.float32)
    col = (pos % GRID_W).astype(np.float32)
    nf = HEAD_DIM // 4
    freqs = (np.float32(1.0) / np.float32(ROPE_BASE) ** (np.arange(nf, dtype=np.float32) / np.float32(nf))).astype(np.float32)
    d = np.arange(LANES) % HEAD_DIM
    use_row = (d // (HEAD_DIM // 2)) == 0
    first = (d % (HEAD_DIM // 2)) < nf
    ang = (np.where(use_row[None, :], row[:, None], col[:, None]) * freqs[d % nf][None, :]).astype(np.float32)
    sin = np.sin(ang)
    return jnp.asarray(np.cos(ang), F32), jnp.asarray(np.where(first[None, :], -sin, sin), F32)


def _pad_rows(w, top, total):
    return jnp.concatenate([jnp.zeros((top, w.shape[1]), w.dtype), w,
                            jnp.zeros((total - top - w.shape[0], w.shape[1]), w.dtype)], axis=0)


def kernel(x, c, ctx, c_ctx, ada_w, ada_b, norm1_g, norm2_g, w_in, attn_sink, ts_prev, ts_next, w0, w_up, a0, a_up,
           k_k, k_a, r_k, g_up, lnx_g, lnx_b, w_out, ffn_w1, ffn_w3, ffn_w2, final_norm_g):
    nbatch, t, d = x.shape
    nctx = ctx.shape[1]
    assert d == D_MODEL and nctx == ROWS and t % ROWS == 0 and nbatch < 8 and ada_w.shape[0] == 1
    nblk = (nctx + t) // ROWS

    cc = jnp.concatenate([c, c_ctx[None, :], jnp.zeros((8 - nbatch - 1, d), F32)], axis=0)
    mod = _adaln(cc, ada_w[0], ada_b[0])
    mods = mod[:nbatch + 1].reshape(nbatch + 1, 6, 1, d)
    sh1, sc1, g1, sh2, sc2, g2 = (mods[:, i] for i in range(6))

    x2 = x.reshape(nbatch * t, d)
    cos_t, sin_t = _rope_tables(t)
    z = _inproj(x2, ctx.reshape(nbatch * nctx, d), norm1_g, sh1, sc1, cos_t, sin_t, _z_weights(w_in), nbatch, nblk)

    attn = _attention(z, attn_sink[0], nbatch, t, nctx)

    n_lora = ts_prev.shape[1] - Z_RKV_W
    pad_lora = lambda m: jnp.pad(m[:, Z_RKV_W:], ((0, 0), (0, Z_LORA_W - n_lora)))
    wupf = _pad_rows(w_up[0, 0], 0, LANES)
    wupb = _pad_rows(w_up[0, 1], D_LORA, LANES)
    aupf = _pad_rows(a_up[0, 0], 0, LANES)
    aupb = _pad_rows(a_up[0, 1], D_LORA, LANES)
    gup = _pad_rows(g_up[0], 0, 2 * LANES)
    r, v, lwf, lwb, kdf, kdb, nkk, bfw, bbw, gate = _rwkv_prep(
        z, ts_prev[:, :Z_RKV_W], ts_next[:, :Z_RKV_W], pad_lora(ts_prev), pad_lora(ts_next),
        w0[0], wupf, wupb, a0[0], aupf, aupb, k_k, k_a, gup, nblk)
    yf, yb = _wkv(r, v, nkk, lwf, kdf, bfw, lwb, kdb, bbw, nbatch, nblk)
    rw = _rwkv_out(yf, yb, r, v, kdf, kdb, gate, r_k, lnx_g, lnx_b, nbatch, t, nblk)

    wo = w_out[0].astype(BF16)
    x1, h2 = _outproj(x2, attn, rw, g1, norm2_g, sh2, sc2, wo[:D_ATTN], wo[D_ATTN:], t)
    out = _ffn(x1, h2, g2, final_norm_g.reshape(1, d),
               ffn_w1[0].astype(BF16), ffn_w3[0].astype(BF16), ffn_w2[0].astype(BF16), t)
    return out.reshape(nbatch, t, d)
```

```python
import functools
import math

import numpy as np
import jax
import jax.numpy as jnp
from jax import lax
from jax.experimental import pallas as pl
from jax.experimental.pallas import tpu as pltpu

F32 = jnp.float32
BF16 = jnp.bfloat16
HI = lax.Precision.HIGHEST

D_MODEL = 2048
HEAD_DIM = 64
N_HEADS_A = 16
N_KV_HEADS = 4
GQA_GROUP = 4
D_ATTN = 1024
D_RWKV = 1024
D_LORA = 64
D_GATE_LORA = 160
D_FF = 5632
GRID_W = 64
WINDOW = 128
BLOCK = 128
ROPE_BASE = 10000.0
NORM_EPS = 1e-6
LNX_EPS = 64e-5
MASK_VALUE = -1e30

LANES = 128
ROWS = 256
CHUNK = 64
WKV_PAIRS = 4
ATT_HEADS = 4
ATT_GROUP = 4
LOG2E = math.log2(math.e)
VMEM_LIMIT = 56 * 1024 * 1024

Z_R, Z_K, Z_V = 0, 1024, 2048
Z_RKV_W = 3072
Z_Q = 3072
Z_LORA = 4096
Z_LORA_W = 512
Z_KV_W = N_KV_HEADS * HEAD_DIM
Z_KC, Z_VC = 4608, 4864
NZ = 5120
ROPE_SLABS = tuple(range(Z_Q, Z_Q + D_ATTN, LANES)) + tuple(range(Z_KC, Z_VC, LANES))
PROJ_TN = 512


def _cparams(sem):
    return pltpu.CompilerParams(dimension_semantics=sem, vmem_limit_bytes=VMEM_LIMIT)


def _sigmoid(x):
    return 0.5 * jnp.tanh(0.5 * x) + 0.5


def _dot(a, b, precision=None):
    return jnp.dot(a, b, preferred_element_type=F32, precision=precision)


def _dot_nt(a, b, precision=None):
    return lax.dot_general(a, b, (((1,), (1,)), ((), ())), preferred_element_type=F32, precision=precision)


def _split(x, terms):
    out = []
    for _ in range(terms - 1):
        hi = x.astype(BF16)
        out.append(hi)
        x = x - hi.astype(F32)
    out.append(x.astype(BF16))
    return out


def _mm(a, b, passes, nt=False):
    d = _dot_nt if nt else _dot
    if passes == 1:
        return d(a.astype(BF16), b.astype(BF16))
    if passes == 3:
        ah, al = _split(a, 2)
        bh, bl = _split(b, 2)
        return d(ah, bh) + (d(ah, bl) + d(al, bh))
    return d(a, b, HI)


def _adaln_kernel(c_ref, w_ref, b_ref, o_ref):
    c = c_ref[...]
    s = c * _sigmoid(c)
    o_ref[...] = _mm(s, w_ref[...], 3) + b_ref[...]


def _adaln(cc, w, b):
    n = w.shape[1]
    tn = 1024
    return pl.pallas_call(
        _adaln_kernel,
        grid=(n // tn,),
        in_specs=[
            pl.BlockSpec((8, D_MODEL), lambda j: (0, 0)),
            pl.BlockSpec((D_MODEL, tn), lambda j: (0, j)),
            pl.BlockSpec((1, tn), lambda j: (0, j)),
        ],
        out_specs=pl.BlockSpec((8, tn), lambda j: (0, j)),
        out_shape=jax.ShapeDtypeStruct((8, n), F32),
        compiler_params=_cparams(("parallel",)),
        name="adaln",
    )(cc, w, b.reshape(1, n))


def _inproj_kernel(x_ref, ctx_ref, g_ref, sh_ref, sc_ref, cos_ref, sin_ref, w_ref, z_ref, shw_ref, *, nblk):
    li = pl.program_id(0) % nblk
    is_ctx = li == 0

    @pl.when(li <= 1)
    def _():
        shb = jnp.broadcast_to(sh_ref[0], (8, D_MODEL)).astype(BF16)
        for j in range(NZ // PROJ_TN):
            sl = slice(j * PROJ_TN, (j + 1) * PROJ_TN)
            shw_ref[:, sl] = _dot(shb, w_ref[:, sl])

    x = jnp.where(is_ctx, ctx_ref[...], x_ref[...])
    xb = (x * (g_ref[...] * (1.0 + sc_ref[0]))).astype(BF16)
    rs = lax.rsqrt(jnp.mean(x * x, axis=-1, keepdims=True) + NORM_EPS)
    cs = cos_ref[...]
    sn = sin_ref[...]
    first = (lax.broadcasted_iota(jnp.int32, (ROWS, LANES), 1) % 32) < 16
    for j in range(NZ // PROJ_TN):
        sl = slice(j * PROJ_TN, (j + 1) * PROJ_TN)
        zt = _dot(xb, w_ref[:, sl]) * rs + shw_ref[0:1, sl]
        for c0 in range(j * PROJ_TN, (j + 1) * PROJ_TN, LANES):
            zz = zt[:, c0 - j * PROJ_TN:c0 - j * PROJ_TN + LANES]
            if c0 in ROPE_SLABS:
                partner = jnp.where(first, pltpu.roll(zz, LANES - 16, 1), pltpu.roll(zz, 16, 1))
                zz = jnp.where(is_ctx, zz, zz * cs + partner * sn)
            z_ref[:, c0:c0 + LANES] = zz


def _inproj(x2, ctx2, g, sh, sc, cos_t, sin_t, wz, nbatch, nblk):
    n = nbatch * nblk * ROWS
    lpb = nblk - 1

    def sel(i):
        return jnp.where(i % nblk == 0, nbatch, i // nblk)

    return pl.pallas_call(
        functools.partial(_inproj_kernel, nblk=nblk),
        grid=(n // ROWS,),
        in_specs=[
            pl.BlockSpec((ROWS, D_MODEL), lambda i: ((i // nblk) * lpb + jnp.maximum(i % nblk - 1, 0), 0)),
            pl.BlockSpec((ROWS, D_MODEL), lambda i: (i // nblk, 0)),
            pl.BlockSpec((1, D_MODEL), lambda i: (0, 0)),
            pl.BlockSpec((1, 1, D_MODEL), lambda i: (sel(i), 0, 0)),
            pl.BlockSpec((1, 1, D_MODEL), lambda i: (sel(i), 0, 0)),
            pl.BlockSpec((ROWS, LANES), lambda i: (jnp.maximum(i % nblk - 1, 0), 0)),
            pl.BlockSpec((ROWS, LANES), lambda i: (jnp.maximum(i % nblk - 1, 0), 0)),
            pl.BlockSpec((D_MODEL, NZ), lambda i: (0, 0), pipeline_mode=pl.Buffered(1)),
        ],
        out_specs=pl.BlockSpec((ROWS, NZ), lambda i: (i, 0)),
        out_shape=jax.ShapeDtypeStruct((n, NZ), F32),
        scratch_shapes=[pltpu.VMEM((8, NZ), F32)],
        compiler_params=_cparams(("arbitrary",)),
        name="inproj",
    )(x2, ctx2, g, sh, sc, cos_t, sin_t, wz)


def _attn_kernel(sink_ref, q_ref, kp_ref, kc_ref, kn_ref, vp_ref, vc_ref, vn_ref, kx_ref, vx_ref, o_ref, *, nb):
    i = pl.program_id(1)
    hg = pl.program_id(2)
    nctx = kx_ref.shape[0]
    nk = 3 * BLOCK + nctx
    lane = lax.broadcasted_iota(jnp.int32, (BLOCK, LANES), 1)
    lo = lane < HEAD_DIM
    heads = range(ATT_HEADS)

    qoff = lax.broadcasted_iota(jnp.int32, (GQA_GROUP * BLOCK, BLOCK), 0) % BLOCK
    koff = lax.broadcasted_iota(jnp.int32, (GQA_GROUP * BLOCK, BLOCK), 1)
    see_prev = jnp.where(i > 0, jnp.where(koff >= qoff, 1, 0), 0) > 0
    see_next = jnp.where(i < nb - 1, jnp.where(koff <= qoff, 1, 0), 0) > 0

    def masked(s):
        return jnp.concatenate([jnp.where(see_prev, s[:, :BLOCK], MASK_VALUE), s[:, BLOCK:2 * BLOCK],
                                jnp.where(see_next, s[:, 2 * BLOCK:3 * BLOCK], MASK_VALUE), s[:, 3 * BLOCK:]], axis=1)
    rowc = lax.broadcasted_iota(jnp.int32, (GQA_GROUP * BLOCK, 1), 0)

    order = (0, 2, 1, 3)

    def queries(h):
        parts = []
        for g in order:
            c0 = (h * GQA_GROUP + g) // 2 * LANES
            qp = q_ref[:, c0:c0 + LANES]
            parts.append(jnp.where(lo, qp, 0.0) if g % 2 == 0 else jnp.where(lo, 0.0, qp))
        return jnp.concatenate(parts, axis=0).astype(BF16)

    rolled = {}

    def slab_of(r, h):
        key = (id(r), h // 2)
        if key not in rolled:
            x = r[:, (h // 2) * LANES:(h // 2 + 1) * LANES]
            rolled[key] = (x, pltpu.roll(x, HEAD_DIM, 1))
        x, xr = rolled[key]
        return (x, xr) if h % 2 == 0 else (xr, x)

    def keys_of(h):
        parts = []
        for r in (kp_ref, kc_ref, kn_ref, kx_ref):
            on_lo, on_hi = slab_of(r, h)
            parts.append(jnp.where(lax.broadcasted_iota(jnp.int32, on_lo.shape, 1) < HEAD_DIM, on_lo, on_hi))
        return jnp.concatenate(parts, axis=0).astype(BF16)

    def values_of(h):
        va, vb = [], []
        for r in (vp_ref, vc_ref, vn_ref, vx_ref):
            on_lo, on_hi = slab_of(r, h)
            lo_r = lax.broadcasted_iota(jnp.int32, on_lo.shape, 1) < HEAD_DIM
            va.append(jnp.where(lo_r, on_lo, 1.0))
            vb.append(jnp.where(lo_r, 1.0, on_hi))
        return jnp.concatenate(va, axis=0).astype(BF16), jnp.concatenate(vb, axis=0).astype(BF16)

    def sinks(h):
        base = (hg * ATT_HEADS + h) * GQA_GROUP
        return LOG2E * jnp.where(rowc < BLOCK, sink_ref[base + order[0]],
                                 jnp.where(rowc < 2 * BLOCK, sink_ref[base + order[1]],
                                           jnp.where(rowc < 3 * BLOCK, sink_ref[base + order[2]],
                                                     sink_ref[base + order[3]])))

    half = 2 * BLOCK
    for g0 in range(0, ATT_HEADS, ATT_GROUP):
        heads = range(g0, g0 + ATT_GROUP)
        ss = [masked(_dot_nt(queries(h), keys_of(h))) for h in heads]
        sks = [sinks(h) for h in heads]
        ms = [jnp.maximum(jnp.max(s, axis=-1, keepdims=True), sk) for s, sk in zip(ss, sks)]
        ps = [jnp.exp2(s - m).astype(BF16) for s, m in zip(ss, ms)]
        extra = [jnp.exp2(sk - m) for sk, m in zip(sks, ms)]
        vals = [values_of(h) for h in heads]
        oa = [_dot(p[:half], v[0]) for p, v in zip(ps, vals)]
        ob = [_dot(p[half:], v[1]) for p, v in zip(ps, vals)]
        for h, a, b, e in zip(heads, oa, ob, extra):
            a = a / (pltpu.roll(a, HEAD_DIM, 1) + e[:half])
            b = b / (pltpu.roll(b, HEAD_DIM, 1) + e[half:])
            o_ref[:, 2 * h * LANES:(2 * h + 1) * LANES] = jnp.where(lo, a[:BLOCK], b[:BLOCK]).astype(o_ref.dtype)
            o_ref[:, (2 * h + 1) * LANES:(2 * h + 2) * LANES] = jnp.where(lo, a[BLOCK:], b[BLOCK:]).astype(o_ref.dtype)


def _attention(z, sink, nbatch, t, nctx):
    nb = t // BLOCK
    rpb = (nctx + t) // BLOCK
    cpb = nctx // BLOCK

    def lat(off):
        def f(b, i, h):
            return b * rpb + cpb + jnp.clip(i + off, 0, nb - 1)
        return f

    kw = ATT_HEADS * HEAD_DIM
    qw = ATT_HEADS * GQA_GROUP * HEAD_DIM

    def kspec(off, col0):
        return pl.BlockSpec((BLOCK, kw), lambda b, i, h: (lat(off)(b, i, h), col0 // kw + h))

    def xspec(col0):
        return pl.BlockSpec((nctx, kw), lambda b, i, h: (b * (rpb * BLOCK // nctx), col0 // kw + h))

    kcol, vcol = Z_KC, Z_VC
    return pl.pallas_call(
        functools.partial(_attn_kernel, nb=nb),
        grid=(nbatch, nb, N_KV_HEADS // ATT_HEADS),
        in_specs=[
            pl.BlockSpec(memory_space=pltpu.SMEM),
            pl.BlockSpec((BLOCK, qw), lambda b, i, h: (lat(0)(b, i, h), Z_Q // qw + h)),
            kspec(-1, kcol), kspec(0, kcol), kspec(1, kcol),
            kspec(-1, vcol), kspec(0, vcol), kspec(1, vcol),
            xspec(kcol), xspec(vcol),
        ],
        out_specs=pl.BlockSpec((BLOCK, qw), lambda b, i, h: (b * nb + i, h)),
        out_shape=jax.ShapeDtypeStruct((nbatch * t, D_ATTN), BF16),
        compiler_params=_cparams(("parallel", "parallel", "parallel")),
        name="attention",
    )(sink, z, z, z, z, z, z, z, z, z)


def _head_ones():
    r = lax.broadcasted_iota(jnp.int32, (LANES, LANES), 0)
    c = lax.broadcasted_iota(jnp.int32, (LANES, LANES), 1)
    return jnp.where((r // HEAD_DIM) == (c // HEAD_DIM), 1.0, 0.0).astype(F32)


def _prep_kernel(zm_ref, zp_ref, zn_ref, lm_ref, lp_ref, ln_ref, mup_ref, mun_ref, lmup_ref, lmun_ref,
                 w0_ref, wupf_ref, wupb_ref, a0_ref, aupf_ref, aupb_ref, kk_ref, ka_ref, gup_ref,
                 r_o, v_o, lwf_o, lwb_o, kdf_o, kdb_o, nkk_o, bf_o, bb_o, gate_o, *, nblk):
    li = pl.program_id(0) % nblk
    keep_p = jnp.where((li == 0) | (li == 1), 0.0, 1.0)
    keep_n = jnp.where((li == 0) | (li == nblk - 1), 0.0, 1.0)
    row8 = lax.broadcasted_iota(jnp.int32, (8, LANES), 0)

    def shift(refs, c0):
        m_ref, p_ref, n_ref, mp_ref, mn_ref = refs
        sl = slice(c0, c0 + LANES)
        zc = m_ref[:, sl]
        down = pltpu.roll(zc, 1, 0)
        up = pltpu.roll(zc, ROWS - 1, 0)
        prev = jnp.concatenate([jnp.where(row8 == 0, p_ref[7:8, sl] * keep_p, down[:8]), down[8:]], axis=0)
        nxt = jnp.concatenate([up[:-8], jnp.where(row8 == 7, n_ref[0:1, sl] * keep_n, up[-8:])], axis=0)
        mp, mn = mp_ref[:, sl], mn_ref[:, sl]
        return zc * (1.0 - mp - mn) + mp * prev + mn * nxt

    shifted = functools.partial(shift, (zm_ref, zp_ref, zn_ref, mup_ref, mun_ref))
    lora = functools.partial(shift, (lm_ref, lp_ref, ln_ref, lmup_ref, lmun_ref))
    ones = _head_ones()
    th_w = jnp.tanh(lora(0))
    ad = lora(LANES)
    sg = jnp.concatenate([_sigmoid(lora(2 * LANES)), _sigmoid(lora(3 * LANES))], axis=1)

    def log_decay(pre):
        return -math.exp(-0.5) * _sigmoid(pre)

    th_b = th_w.astype(BF16)
    ad_b = ad.astype(BF16)
    sg_b = sg.astype(BF16)

    def lora_fine(xb, w):
        hi, lo = _split(w, 2)
        both = _dot(xb, jnp.concatenate([hi, lo], axis=1))
        return both[:, :LANES] + both[:, LANES:]

    for s in range(D_RWKV // LANES):
        sl = slice(s * LANES, (s + 1) * LANES)
        r = shifted(Z_R + s * LANES)
        k = shifted(Z_K + s * LANES)
        v = shifted(Z_V + s * LANES)
        r_o[:, sl] = r.astype(r_o.dtype)
        v_o[:, sl] = v.astype(v_o.dtype)
        kk = k * kk_ref[:, sl]
        ss = _mm(kk * kk, ones, 1)
        kk = kk * lax.rsqrt(jnp.maximum(ss, 1e-12))
        nkk_o[:, sl] = (-kk).astype(nkk_o.dtype)
        lwf_o[:, sl] = log_decay(w0_ref[0:1, sl] + lora_fine(th_b, wupf_ref[:, sl]))
        lwb_o[:, sl] = log_decay(w0_ref[1:2, sl] + lora_fine(th_b, wupb_ref[:, sl]))
        a_pre = _dot(ad_b, jnp.concatenate([aupf_ref[:, sl], aupb_ref[:, sl]], axis=1).astype(BF16))
        af = _sigmoid(a0_ref[0:1, sl] + a_pre[:, :LANES])
        ab = _sigmoid(a0_ref[1:2, sl] + a_pre[:, LANES:])
        ka = ka_ref[:, sl]
        kdf_o[:, sl] = (k * (1.0 + (af - 1.0) * ka)).astype(kdf_o.dtype)
        kdb_o[:, sl] = (k * (1.0 + (ab - 1.0) * ka)).astype(kdb_o.dtype)
        bf_o[:, sl] = (kk * af).astype(bf_o.dtype)
        bb_o[:, sl] = (kk * ab).astype(bb_o.dtype)
        gate_o[:, sl] = _dot(sg_b, gup_ref[:, sl].astype(BF16)).astype(gate_o.dtype)


def _rwkv_prep(z, mup, mun, lmup, lmun, w0, wupf, wupb, a0, aupf, aupb, k_k, k_a, gup, nblk):
    n = z.shape[0]
    nsub = ROWS // 8
    full = lambda a: pl.BlockSpec(a.shape, lambda i: (0,) * a.ndim)
    outs = [jax.ShapeDtypeStruct((n, D_RWKV), F32 if name in ("lwf", "lwb") else BF16)
            for name in ("r", "v", "lwf", "lwb", "kdf", "kdb", "nkk", "bf", "bb", "gate")]
    ospec = pl.BlockSpec((ROWS, D_RWKV), lambda i: (i, 0))
    return pl.pallas_call(
        functools.partial(_prep_kernel, nblk=nblk),
        grid=(n // ROWS,),
        in_specs=[
            pl.BlockSpec((ROWS, Z_RKV_W), lambda i: (i, 0)),
            pl.BlockSpec((8, Z_RKV_W), lambda i: (jnp.maximum(i * nsub - 1, 0), 0)),
            pl.BlockSpec((8, Z_RKV_W), lambda i: (jnp.minimum((i + 1) * nsub, n // 8 - 1), 0)),
            pl.BlockSpec((ROWS, Z_LORA_W), lambda i: (i, Z_LORA // Z_LORA_W)),
            pl.BlockSpec((8, Z_LORA_W), lambda i: (jnp.maximum(i * nsub - 1, 0), Z_LORA // Z_LORA_W)),
            pl.BlockSpec((8, Z_LORA_W), lambda i: (jnp.minimum((i + 1) * nsub, n // 8 - 1), Z_LORA // Z_LORA_W)),
            full(mup), full(mun), full(lmup), full(lmun),
            full(w0), full(wupf), full(wupb), full(a0), full(aupf), full(aupb),
            full(k_k), full(k_a), full(gup),
        ],
        out_specs=[ospec] * 10,
        out_shape=outs,
        compiler_params=_cparams(("parallel",)),
        name="rwkv_prep",
    )(z, z, z, z, z, z, mup, mun, lmup, lmun, w0, wupf, wupb, a0, aupf, aupb, k_k, k_a, gup)


def _wkv_intra(probs, bd, eye_bd, head_a, head_a2):
    n = CHUNK
    zero = jnp.zeros((n, LANES), F32)
    each = lambda f, *ls: [f(*xs) for xs in zip(*ls)]

    def stack(x):
        return jnp.concatenate([jnp.where(head_a, x, 0.0), jnp.where(head_a, 0.0, x)], axis=0)

    def cumulative(p):
        inclb = p[6].astype(BF16)
        l0, l1, l2 = _split(p[1], 3)
        return _dot(inclb, l0) + (_dot(inclb, l1) + _dot(inclb, l2))

    cs = each(cumulative, probs)

    def factors(p, c):
        r, lw, k, v, a, b = p[:6]
        ctot = jnp.sum(lw, axis=0, keepdims=True)
        mid = 0.5 * ctot
        e_in = jnp.exp(c - mid)
        e_out = jnp.exp(mid - c)
        e_abs = jnp.exp(c)
        e_rest = jnp.exp(ctot - c)
        undo = jnp.exp(-lw)
        lhs = jnp.concatenate([stack(a * e_in * undo), stack(r * e_in)], axis=0).astype(BF16)
        rhs = jnp.concatenate([b * e_out, k * e_out], axis=0).astype(BF16)
        rest = jnp.concatenate([b * e_rest, k * e_rest], axis=0).astype(BF16)
        return lhs, rhs, rest, stack(a * e_abs * undo).astype(BF16), r * e_abs, jnp.exp(ctot)

    fs = each(factors, probs, cs)
    grams = [_dot_nt(f[0], f[1]) for f in fs]
    aas = [jnp.where(p[7], g[:2 * n], 0.0) for p, g in zip(probs, grams)]
    ars = [jnp.where(p[8], g[2 * n:], 0.0).astype(BF16) for p, g in zip(probs, grams)]
    a_abs = [jnp.where(bd, jnp.concatenate([aa[:n], pltpu.roll(aa[n:], n, 1)], axis=0), 0.0) for aa in aas]
    aas = [aa.astype(BF16) for aa in aas]

    invs = [eye_bd + x for x in a_abs]
    a_abs = [x.astype(BF16) for x in a_abs]
    pws = [_dot(x, x).astype(BF16) for x in a_abs]
    for _ in range(int(np.log2(n)) - 2):
        boths = [_dot(jnp.concatenate([pw, inv.astype(BF16)], axis=0), pw) for pw, inv in zip(pws, invs)]
        pws = [x[:2 * n].astype(BF16) for x in boths]
        invs = [inv + x[2 * n:] for inv, x in zip(invs, boths)]
    invs = [(inv + _dot(inv.astype(BF16), pw)).astype(BF16) for inv, pw in zip(invs, pws)]

    zvs = [jnp.concatenate([zero, p[3]], axis=0).astype(BF16) for p in probs]
    akvs = [jnp.where(bd, _dot(aa, zv), 0.0).astype(BF16) for aa, zv in zip(aas, zvs)]
    wu_sts = [_dot(inv, jnp.concatenate([f[3], akv], axis=1)) for inv, f, akv in zip(invs, fs, akvs)]
    m2s = [jnp.concatenate([x[:n] + x[n:], jnp.concatenate([zero, p[3]], axis=1)], axis=0)
           for x, p in zip(wu_sts, probs)]
    qy_sts = [_dot(ar, m2.astype(BF16)) for ar, m2 in zip(ars, m2s)]
    ghs = [_dot(m2.T.astype(BF16), f[2]) for m2, f in zip(m2s, fs)]
    out = []
    for f, qy_st, gh in zip(fs, qy_sts, ghs):
        qy = jnp.where(head_a2, qy_st[:n], qy_st[n:])
        out.append((f[4] + qy[:, :LANES], qy[:, LANES:],
                    jnp.where(bd, gh[:LANES], 0.0) + eye_bd * f[5], jnp.where(bd, gh[LANES:], 0.0)))
    return out


def _wkv_kernel(rf, vf, af, lwf, kf, bf, rb, vb, ab, lwb, kb, bb, yf_o, yb_o, s_ref, q_ref, y0_ref, gm_ref, h_ref):
    j = pl.program_id(1)
    n = CHUNK
    nsub = ROWS // n
    cur = j % 2
    prv = 1 - cur

    @pl.when(j == 0)
    def _():
        s_ref[...] = jnp.zeros_like(s_ref)
        q_ref[1] = jnp.zeros(q_ref.shape[1:], q_ref.dtype)
        y0_ref[1] = jnp.zeros(y0_ref.shape[1:], y0_ref.dtype)
        gm_ref[1] = jnp.zeros(gm_ref.shape[1:], gm_ref.dtype)
        h_ref[1] = jnp.zeros(h_ref.shape[1:], h_ref.dtype)

    for step in range(nsub):
        for d, y_o in enumerate((yf_o, yb_o)):
            for p in range(WKV_PAIRS):
                u = step if d == 0 else nsub - 1 - step
                idx = (d * WKV_PAIRS + p) * nsub + u
                s0 = s_ref[d * WKV_PAIRS + p].astype(BF16)
                y_o[u * n:(u + 1) * n, p * LANES:(p + 1) * LANES] = (
                    _dot_nt(q_ref[prv, idx], s0) + y0_ref[prv, idx]).astype(y_o.dtype)
                s_ref[d * WKV_PAIRS + p] = _dot(s0, gm_ref[prv, idx]) + h_ref[prv, idx]

    ti = lax.broadcasted_iota(jnp.int32, (n, n), 0)
    si = lax.broadcasted_iota(jnp.int32, (n, n), 1)
    t2 = lax.broadcasted_iota(jnp.int32, (2 * n, 2 * n), 0)
    s2 = lax.broadcasted_iota(jnp.int32, (2 * n, 2 * n), 1)
    tw, sw = t2 % n, s2 % n
    bd = (t2 // n) == (s2 // n)
    eye_bd = jnp.where(t2 == s2, 1.0, 0.0).astype(F32)
    head_a = lax.broadcasted_iota(jnp.int32, (n, LANES), 1) < HEAD_DIM
    head_a2 = lax.broadcasted_iota(jnp.int32, (n, 2 * LANES), 1) % LANES < HEAD_DIM
    masks = (
        (jnp.where(si <= ti, 1.0, 0.0).astype(F32), sw < tw, sw <= tw),
        (jnp.where(si >= ti, 1.0, 0.0).astype(F32), sw > tw, sw >= tw),
    )
    srcs = ((rf, lwf, kf, vf, af, bf), (rb, lwb, kb, vb, ab, bb))

    probs = [tuple(x[u * n:(u + 1) * n, p * LANES:(p + 1) * LANES].astype(F32) for x in srcs[d]) + masks[d]
             for d in range(2) for p in range(WKV_PAIRS) for u in range(nsub)]
    for idx, (q, y0, gm, h) in enumerate(_wkv_intra(probs, bd, eye_bd, head_a, head_a2)):
        q_ref[cur, idx] = q.astype(q_ref.dtype)
        y0_ref[cur, idx] = y0
        gm_ref[cur, idx] = gm.astype(gm_ref.dtype)
        h_ref[cur, idx] = h


def _wkv(r, v, nkk, lwf, kdf, bfw, lwb, kdb, bbw, nbatch, nblk):
    n = r.shape[0]
    ngrp = D_RWKV // (WKV_PAIRS * LANES)
    nprob = 2 * WKV_PAIRS * (ROWS // CHUNK)
    last = nblk - 1
    width = WKV_PAIRS * LANES

    def bwd_blk(j):
        return jnp.where(j == 0, 0, nblk - j)

    def spec(blk):
        return pl.BlockSpec((ROWS, width), lambda g, j: ((g // ngrp) * nblk + blk(j), g % ngrp))

    fin = spec(lambda j: jnp.minimum(j, last))
    bin_ = spec(lambda j: bwd_blk(jnp.minimum(j, last)))
    fout = spec(lambda j: jnp.maximum(j - 1, 0))
    bout = spec(lambda j: bwd_blk(jnp.maximum(j - 1, 0)))
    out = jax.ShapeDtypeStruct((n, D_RWKV), BF16)
    return pl.pallas_call(
        _wkv_kernel,
        grid=(nbatch * ngrp, nblk + 1),
        in_specs=[fin] * 6 + [bin_] * 6,
        out_specs=[fout, bout],
        out_shape=[out, out],
        scratch_shapes=[pltpu.VMEM((2 * WKV_PAIRS, LANES, LANES), F32),
                        pltpu.VMEM((2, nprob, CHUNK, LANES), BF16), pltpu.VMEM((2, nprob, CHUNK, LANES), F32),
                        pltpu.VMEM((2, nprob, LANES, LANES), BF16), pltpu.VMEM((2, nprob, LANES, LANES), F32)],
        compiler_params=_cparams(("parallel", "arbitrary")),
        name="wkv",
    )(r, v, nkk, lwf, kdf, bfw, r, v, nkk, lwb, kdb, bbw)


def _rwkv_mix(yf_ref, yb_ref, r_ref, v_ref, kdf_ref, kdb_ref, gate_ref, rk_ref, lg_ref, lb_ref):
    ones = _head_ones().astype(BF16)
    mean_mat = (_head_ones() * (1.0 / HEAD_DIM)).astype(BF16)
    slabs = [slice(s * LANES, (s + 1) * LANES) for s in range(D_RWKV // LANES)]
    n = yf_ref.shape[0]

    def head_mean(y):
        hi, lo = _split(y, 2)
        both = _dot(jnp.concatenate([hi, lo], axis=0), mean_mat)
        return both[:n] + both[n:]

    ys = [yf_ref[:, sl].astype(F32) + yb_ref[:, sl].astype(F32) for sl in slabs]
    ycs = [y - head_mean(y) for y in ys]
    variances = [_dot((yc * yc).astype(BF16), mean_mat) for yc in ycs]
    sums = [_dot((r_ref[:, sl].astype(F32) * (kdf_ref[:, sl].astype(F32) + kdb_ref[:, sl].astype(F32))
                  * rk_ref[:, sl]).astype(BF16), ones) for sl in slabs]
    out = []
    for sl, yc, var, sm in zip(slabs, ycs, variances, sums):
        yn = yc * lax.rsqrt(var + LNX_EPS) * lg_ref[:, sl] + lb_ref[:, sl]
        bonus = sm * v_ref[:, sl].astype(F32)
        out.append(((yn + bonus) * gate_ref[:, sl].astype(F32)).astype(BF16))
    return jnp.concatenate(out, axis=1)


def _outproj_kernel(x_ref, attn_ref, yf_ref, yb_ref, r_ref, v_ref, kdf_ref, kdb_ref, gate_ref, rk_ref, lg_ref, lb_ref,
                    g1_ref, g_ref, sh_ref, sc_ref, wa_ref, wr_ref, o_ref, h_ref):
    rw = _rwkv_mix(yf_ref, yb_ref, r_ref, v_ref, kdf_ref, kdb_ref, gate_ref, rk_ref, lg_ref, lb_ref)
    mix = _dot(attn_ref[...], wa_ref[...]) + _dot(rw, wr_ref[...])
    x1 = x_ref[...] + g1_ref[0] * mix
    o_ref[...] = x1
    ms = jnp.mean(x1 * x1, axis=-1, keepdims=True)
    h = x1 * lax.rsqrt(ms + NORM_EPS) * g_ref[...]
    h_ref[...] = (h * (1.0 + sc_ref[0]) + sh_ref[0]).astype(h_ref.dtype)


def _outproj(x2, attn, yf, yb, r, v, kdf, kdb, gate, r_k, lnx_g, lnx_b, g1, g, sh, sc, wa, wr, t, nblk):
    n = x2.shape[0]
    lpb = t // ROWS
    modspec = pl.BlockSpec((1, 1, D_MODEL), lambda i: (i // lpb, 0, 0))
    rowspec = pl.BlockSpec((ROWS, D_MODEL), lambda i: (i, 0))
    seqspec = pl.BlockSpec((ROWS, D_RWKV), lambda i: ((i // lpb) * nblk + 1 + i % lpb, 0))
    pspec = pl.BlockSpec((1, D_RWKV), lambda i: (0, 0))
    return pl.pallas_call(
        _outproj_kernel,
        grid=(n // ROWS,),
        in_specs=[
            rowspec,
            pl.BlockSpec((ROWS, D_ATTN), lambda i: (i, 0)),
        ] + [seqspec] * 7 + [pspec] * 3 + [
            modspec,
            pl.BlockSpec((1, D_MODEL), lambda i: (0, 0)),
            modspec, modspec,
            pl.BlockSpec((D_ATTN, D_MODEL), lambda i: (0, 0), pipeline_mode=pl.Buffered(1)),
            pl.BlockSpec((D_RWKV, D_MODEL), lambda i: (0, 0), pipeline_mode=pl.Buffered(1)),
        ],
        out_specs=[rowspec, rowspec],
        out_shape=[jax.ShapeDtypeStruct((n, D_MODEL), F32), jax.ShapeDtypeStruct((n, D_MODEL), BF16)],
        compiler_params=_cparams(("parallel",)),
        name="outproj",
    )(x2, attn, yf, yb, r, v, kdf, kdb, gate, r_k, lnx_g, lnx_b, g1, g, sh, sc, wa, wr)


def _ffn_kernel(x_ref, h_ref, g2_ref, fg_ref, w1_ref, w3_ref, w2_ref, o_ref, acc_ref):
    j = pl.program_id(1)

    @pl.when(j == 0)
    def _():
        acc_ref[...] = jnp.zeros_like(acc_ref)

    h = h_ref[...]
    a = _dot(h, w1_ref[...])
    b = _dot(h, w3_ref[...])
    u = a * _sigmoid(a) * b
    acc_ref[...] += _dot(u.astype(BF16), w2_ref[...])

    @pl.when(j == pl.num_programs(1) - 1)
    def _():
        y = x_ref[...] + g2_ref[0] * acc_ref[...]
        ms = jnp.mean(y * y, axis=-1, keepdims=True)
        o_ref[...] = y * lax.rsqrt(ms + NORM_EPS) * fg_ref[...]


def _ffn(x1, h2, g2, fg, w1, w3, w2, t):
    n = x1.shape[0]
    tm, tf = 512, 512
    modspec = pl.BlockSpec((1, 1, D_MODEL), lambda i, j: (i // (t // tm), 0, 0))
    vecspec = pl.BlockSpec((1, D_MODEL), lambda i, j: (0, 0))
    return pl.pallas_call(
        _ffn_kernel,
        grid=(n // tm, D_FF // tf),
        in_specs=[
            pl.BlockSpec((tm, D_MODEL), lambda i, j: (i, 0)),
            pl.BlockSpec((tm, D_MODEL), lambda i, j: (i, 0)),
            modspec, vecspec,
            pl.BlockSpec((D_MODEL, tf), lambda i, j: (0, j)),
            pl.BlockSpec((D_MODEL, tf), lambda i, j: (0, j)),
            pl.BlockSpec((tf, D_MODEL), lambda i, j: (j, 0)),
        ],
        out_specs=pl.BlockSpec((tm, D_MODEL), lambda i, j: (i, 0)),
        out_shape=jax.ShapeDtypeStruct((n, D_MODEL), F32),
        scratch_shapes=[pltpu.VMEM((tm, D_MODEL), F32)],
        compiler_params=_cparams(("parallel", "arbitrary")),
        name="ffn",
    )(x1, h2, g2, fg, w1, w3, w2)


def _z_weights_kernel(w_ref, o_ref):
    q0, k0, v0, r0 = 0, D_ATTN, D_ATTN + Z_KV_W, D_ATTN + 2 * Z_KV_W
    lora0 = r0 + Z_RKV_W
    n_lora = 4 * D_LORA + D_GATE_LORA
    cast = lambda x: x.astype(o_ref.dtype)
    o_ref[:, Z_R:Z_R + Z_RKV_W] = cast(w_ref[:, r0:lora0])
    o_ref[:, Z_Q:Z_Q + D_ATTN] = cast(w_ref[:, q0:q0 + D_ATTN] * (HEAD_DIM ** -0.5 * LOG2E))
    o_ref[:, Z_LORA:Z_LORA + Z_LORA_W] = jnp.zeros((o_ref.shape[0], Z_LORA_W), o_ref.dtype)
    o_ref[:, Z_LORA:Z_LORA + n_lora] = cast(w_ref[:, lora0:lora0 + n_lora])
    o_ref[:, Z_KC:Z_KC + Z_KV_W] = cast(w_ref[:, k0:k0 + Z_KV_W])
    o_ref[:, Z_VC:Z_VC + Z_KV_W] = cast(w_ref[:, v0:v0 + Z_KV_W])


def _z_weights(w_in):
    rows = 256
    return pl.pallas_call(
        _z_weights_kernel,
        grid=(D_MODEL // rows,),
        in_specs=[pl.BlockSpec((None, rows, w_in.shape[2]), lambda i: (0, i, 0))],
        out_specs=pl.BlockSpec((rows, NZ), lambda i: (i, 0)),
        out_shape=jax.ShapeDtypeStruct((D_MODEL, NZ), BF16),
        compiler_params=_cparams(("parallel",)),
        name="z_weights",
    )(w_in)


def _rope_tables(t):
    pos = np.arange(t)
    row = (pos // GRID_W).astype(np.float32)
    col = (pos % GRID_W).astype(np.float32)
    nf = HEAD_DIM // 4
    freqs = (np.float32(1.0) / np.float32(ROPE_BASE) ** (np.arange(nf, dtype=np.float32) / np.float32(nf))).astype(np.float32)
    d = np.arange(LANES) % HEAD_DIM
    use_row = (d // (HEAD_DIM // 2)) == 0
    first = (d % (HEAD_DIM // 2)) < nf
    ang = (np.where(use_row[None, :], row[:, None], col[:, None]) * freqs[d % nf][None, :]).astype(np.float32)
    sin = np.sin(ang)
    return jnp.asarray(np.cos(ang), F32), jnp.asarray(np.where(first[None, :], -sin, sin), F32)


def _pad_rows(w, top, total):
    return jnp.concatenate([jnp.zeros((top, w.shape[1]), w.dtype), w,
                            jnp.zeros((total - top - w.shape[0], w.shape[1]), w.dtype)], axis=0)


def kernel(x, c, ctx, c_ctx, ada_w, ada_b, norm1_g, norm2_g, w_in, attn_sink, ts_prev, ts_next, w0, w_up, a0, a_up,
           k_k, k_a, r_k, g_up, lnx_g, lnx_b, w_out, ffn_w1, ffn_w3, ffn_w2, final_norm_g):
    nbatch, t, d = x.shape
    nctx = ctx.shape[1]
    assert d == D_MODEL and nctx == ROWS and t % ROWS == 0 and nbatch < 8 and ada_w.shape[0] == 1
    nblk = (nctx + t) // ROWS

    cc = jnp.concatenate([c, c_ctx[None, :], jnp.zeros((8 - nbatch - 1, d), F32)], axis=0)
    mod = _adaln(cc, ada_w[0], ada_b[0])
    mods = mod[:nbatch + 1].reshape(nbatch + 1, 6, 1, d)
    sh1, sc1, g1, sh2, sc2, g2 = (mods[:, i] for i in range(6))

    x2 = x.reshape(nbatch * t, d)
    cos_t, sin_t = _rope_tables(t)
    z = _inproj(x2, ctx.reshape(nbatch * nctx, d), norm1_g, sh1, sc1, cos_t, sin_t, _z_weights(w_in), nbatch, nblk)

    attn = _attention(z, attn_sink[0], nbatch, t, nctx)

    n_lora = ts_prev.shape[1] - Z_RKV_W
    pad_lora = lambda m: jnp.pad(m[:, Z_RKV_W:], ((0, 0), (0, Z_LORA_W - n_lora)))
    wupf = _pad_rows(w_up[0, 0], 0, LANES)
    wupb = _pad_rows(w_up[0, 1], D_LORA, LANES)
    aupf = _pad_rows(a_up[0, 0], 0, LANES)
    aupb = _pad_rows(a_up[0, 1], D_LORA, LANES)
    gup = _pad_rows(g_up[0], 0, 2 * LANES)
    r, v, lwf, lwb, kdf, kdb, nkk, bfw, bbw, gate = _rwkv_prep(
        z, ts_prev[:, :Z_RKV_W], ts_next[:, :Z_RKV_W], pad_lora(ts_prev), pad_lora(ts_next),
        w0[0], wupf, wupb, a0[0], aupf, aupb, k_k, k_a, gup, nblk)
    yf, yb = _wkv(r, v, nkk, lwf, kdf, bfw, lwb, kdb, bbw, nbatch, nblk)

    wo = w_out[0].astype(BF16)
    x1, h2 = _outproj(x2, attn, yf, yb, r, v, kdf, kdb, gate, r_k, lnx_g, lnx_b,
                      g1, norm2_g, sh2, sc2, wo[:D_ATTN], wo[D_ATTN:], t, nblk)
    out = _ffn(x1, h2, g2, final_norm_g.reshape(1, d),
               ffn_w1[0].astype(BF16), ffn_w3[0].astype(BF16), ffn_w2[0].astype(BF16), t)
    return out.reshape(nbatch, t, d)
```

```python
import functools
import math

import numpy as np
import jax
import jax.numpy as jnp
from jax import lax
from jax.experimental import pallas as pl
from jax.experimental.pallas import tpu as pltpu

F32 = jnp.float32
BF16 = jnp.bfloat16

D_MODEL = 2048
HEAD_DIM = 64
N_KV_HEADS = 4
GQA_GROUP = 4
D_ATTN = 1024
D_RWKV = 1024
D_LORA = 64
D_GATE_LORA = 160
D_FF = 5632
GRID_W = 64
WINDOW = 128
BLOCK = 128
ROPE_BASE = 10000.0
NORM_EPS = 1e-6
LNX_EPS = 64e-5
MASK_VALUE = -1e30
KK_EPS = 1e-12
ROT_HALF = HEAD_DIM // 2
ROT_PAIR = HEAD_DIM // 4
LOG2E = math.log2(math.e)

LANES = 128
SUBLANES = 8
V7X_VMEM_BYTES = 64 * 2 ** 20
VMEM_LIMIT = V7X_VMEM_BYTES - 8 * 2 ** 20

ROWS = 256
CHUNK = 64
WKV_PAIRS = 4
ATT_HEADS = 4
ADALN_TN = 1024
PROJ_TN = 512
FFN_TM, FFN_TF = 512, 512
WPREP_ROWS = 256

Z_R, Z_K, Z_V = 0, 1024, 2048
Z_RKV_W = 3072
Z_Q = 3072
Z_LORA = 4096
Z_LORA_W = 512
Z_KV_W = N_KV_HEADS * HEAD_DIM
Z_KC, Z_VC = 4608, 4864
NZ = 5120
ROPE_SLABS = tuple(range(Z_Q, Z_Q + D_ATTN, LANES)) + tuple(range(Z_KC, Z_VC, LANES))


def _cparams(sem):
    return pltpu.CompilerParams(dimension_semantics=sem, vmem_limit_bytes=VMEM_LIMIT)


def _sigmoid(x):
    return 0.5 * jnp.tanh(0.5 * x) + 0.5


def _dot(a, b):
    return jnp.dot(a, b, preferred_element_type=F32)


def _dot_nt(a, b):
    return lax.dot_general(a, b, (((1,), (1,)), ((), ())), preferred_element_type=F32)


def _split(x, terms):
    out = []
    for _ in range(terms - 1):
        hi = x.astype(BF16)
        out.append(hi)
        x = x - hi.astype(F32)
    out.append(x.astype(BF16))
    return out


def _mm(a, b, nt=False):
    return (_dot_nt if nt else _dot)(a.astype(BF16), b.astype(BF16))


def _mm_fine(a, b):
    ah, al = _split(a, 2)
    bh, bl = _split(b, 2)
    return _dot(ah, bh) + (_dot(ah, bl) + _dot(al, bh))


def _adaln_kernel(c_ref, w_ref, b_ref, o_ref):
    c = c_ref[...]
    s = c * _sigmoid(c)
    o_ref[...] = _mm_fine(s, w_ref[...]) + b_ref[...]


def _adaln(cc, w, b):
    n = w.shape[1]
    tn = ADALN_TN
    return pl.pallas_call(
        _adaln_kernel,
        grid=(n // tn,),
        in_specs=[
            pl.BlockSpec((SUBLANES, D_MODEL), lambda j: (0, 0)),
            pl.BlockSpec((D_MODEL, tn), lambda j: (0, j)),
            pl.BlockSpec((1, tn), lambda j: (0, j)),
        ],
        out_specs=pl.BlockSpec((SUBLANES, tn), lambda j: (0, j)),
        out_shape=jax.ShapeDtypeStruct((SUBLANES, n), F32),
        compiler_params=_cparams(("parallel",)),
        name="adaln",
    )(cc, w, b.reshape(1, n))


def _inproj_kernel(x_ref, ctx_ref, g_ref, sh_ref, sc_ref, cos_ref, sin_ref, w_ref, z_ref, shw_ref, *, nblk):
    li = pl.program_id(0) % nblk
    is_ctx = li == 0

    @pl.when(li <= 1)
    def _():
        shb = jnp.broadcast_to(sh_ref[0], (SUBLANES, D_MODEL)).astype(BF16)
        for j in range(NZ // PROJ_TN):
            sl = slice(j * PROJ_TN, (j + 1) * PROJ_TN)
            shw_ref[:, sl] = _dot(shb, w_ref[:, sl])

    x = jnp.where(is_ctx, ctx_ref[...], x_ref[...])
    xb = (x * (g_ref[...] * (1.0 + sc_ref[0]))).astype(BF16)
    rs = lax.rsqrt(jnp.mean(x * x, axis=-1, keepdims=True) + NORM_EPS)
    cs = cos_ref[...]
    sn = sin_ref[...]
    first = (lax.broadcasted_iota(jnp.int32, (ROWS, LANES), 1) % ROT_HALF) < ROT_PAIR
    for j in range(NZ // PROJ_TN):
        sl = slice(j * PROJ_TN, (j + 1) * PROJ_TN)
        zt = _dot(xb, w_ref[:, sl]) * rs + shw_ref[0:1, sl]
        for c0 in range(j * PROJ_TN, (j + 1) * PROJ_TN, LANES):
            zz = zt[:, c0 - j * PROJ_TN:c0 - j * PROJ_TN + LANES]
            if c0 in ROPE_SLABS:
                partner = jnp.where(first, pltpu.roll(zz, LANES - ROT_PAIR, 1), pltpu.roll(zz, ROT_PAIR, 1))
                zz = jnp.where(is_ctx, zz, zz * cs + partner * sn)
            z_ref[:, c0:c0 + LANES] = zz


def _inproj(x2, ctx2, g, sh, sc, cos_t, sin_t, wz, nbatch, nblk):
    n = nbatch * nblk * ROWS
    lpb = nblk - 1

    def sel(i):
        return jnp.where(i % nblk == 0, nbatch, i // nblk)

    return pl.pallas_call(
        functools.partial(_inproj_kernel, nblk=nblk),
        grid=(n // ROWS,),
        in_specs=[
            pl.BlockSpec((ROWS, D_MODEL), lambda i: ((i // nblk) * lpb + jnp.maximum(i % nblk - 1, 0), 0)),
            pl.BlockSpec((ROWS, D_MODEL), lambda i: (i // nblk, 0)),
            pl.BlockSpec((1, D_MODEL), lambda i: (0, 0)),
            pl.BlockSpec((1, 1, D_MODEL), lambda i: (sel(i), 0, 0)),
            pl.BlockSpec((1, 1, D_MODEL), lambda i: (sel(i), 0, 0)),
            pl.BlockSpec((ROWS, LANES), lambda i: (jnp.maximum(i % nblk - 1, 0), 0)),
            pl.BlockSpec((ROWS, LANES), lambda i: (jnp.maximum(i % nblk - 1, 0), 0)),
            pl.BlockSpec((D_MODEL, NZ), lambda i: (0, 0), pipeline_mode=pl.Buffered(1)),
        ],
        out_specs=pl.BlockSpec((ROWS, NZ), lambda i: (i, 0)),
        out_shape=jax.ShapeDtypeStruct((n, NZ), F32),
        scratch_shapes=[pltpu.VMEM((SUBLANES, NZ), F32)],
        compiler_params=_cparams(("arbitrary",)),
        name="inproj",
    )(x2, ctx2, g, sh, sc, cos_t, sin_t, wz)


def _attn_kernel(sink_ref, q_ref, kp_ref, kc_ref, kn_ref, vp_ref, vc_ref, vn_ref, kx_ref, vx_ref, o_ref, *, nb):
    i = pl.program_id(1)
    hg = pl.program_id(2)
    lane = lax.broadcasted_iota(jnp.int32, (BLOCK, LANES), 1)
    lo = lane < HEAD_DIM
    heads = range(ATT_HEADS)

    qoff = lax.broadcasted_iota(jnp.int32, (GQA_GROUP * BLOCK, BLOCK), 0) % BLOCK
    koff = lax.broadcasted_iota(jnp.int32, (GQA_GROUP * BLOCK, BLOCK), 1)
    see_prev = jnp.where(i > 0, jnp.where(koff >= qoff, 1, 0), 0) > 0
    see_next = jnp.where(i < nb - 1, jnp.where(koff <= qoff, 1, 0), 0) > 0

    def masked(s):
        return jnp.concatenate([jnp.where(see_prev, s[:, :BLOCK], MASK_VALUE), s[:, BLOCK:2 * BLOCK],
                                jnp.where(see_next, s[:, 2 * BLOCK:3 * BLOCK], MASK_VALUE), s[:, 3 * BLOCK:]], axis=1)
    rowc = lax.broadcasted_iota(jnp.int32, (GQA_GROUP * BLOCK, 1), 0)

    order = (0, 2, 1, 3)

    def queries(h):
        parts = []
        for g in order:
            c0 = (h * GQA_GROUP + g) // 2 * LANES
            qp = q_ref[:, c0:c0 + LANES]
            parts.append(jnp.where(lo, qp, 0.0) if g % 2 == 0 else jnp.where(lo, 0.0, qp))
        return jnp.concatenate(parts, axis=0).astype(BF16)

    rolled = {}

    def slab_of(r, h):
        key = (id(r), h // 2)
        if key not in rolled:
            x = r[:, (h // 2) * LANES:(h // 2 + 1) * LANES]
            rolled[key] = (x, pltpu.roll(x, HEAD_DIM, 1))
        x, xr = rolled[key]
        return (x, xr) if h % 2 == 0 else (xr, x)

    def keys_of(h):
        parts = []
        for r in (kp_ref, kc_ref, kn_ref, kx_ref):
            on_lo, on_hi = slab_of(r, h)
            parts.append(jnp.where(lax.broadcasted_iota(jnp.int32, on_lo.shape, 1) < HEAD_DIM, on_lo, on_hi))
        return jnp.concatenate(parts, axis=0).astype(BF16)

    def values_of(h):
        va, vb = [], []
        for r in (vp_ref, vc_ref, vn_ref, vx_ref):
            on_lo, on_hi = slab_of(r, h)
            lo_r = lax.broadcasted_iota(jnp.int32, on_lo.shape, 1) < HEAD_DIM
            va.append(jnp.where(lo_r, on_lo, 1.0))
            vb.append(jnp.where(lo_r, 1.0, on_hi))
        return jnp.concatenate(va, axis=0).astype(BF16), jnp.concatenate(vb, axis=0).astype(BF16)

    def sinks(h):
        base = (hg * ATT_HEADS + h) * GQA_GROUP
        return LOG2E * jnp.where(rowc < BLOCK, sink_ref[base + order[0]],
                                 jnp.where(rowc < 2 * BLOCK, sink_ref[base + order[1]],
                                           jnp.where(rowc < 3 * BLOCK, sink_ref[base + order[2]],
                                                     sink_ref[base + order[3]])))

    half = 2 * BLOCK
    ss = [masked(_dot_nt(queries(h), keys_of(h))) for h in heads]
    sks = [sinks(h) for h in heads]
    ms = [jnp.maximum(jnp.max(s, axis=-1, keepdims=True), sk) for s, sk in zip(ss, sks)]
    ps = [jnp.exp2(s - m).astype(BF16) for s, m in zip(ss, ms)]
    extra = [jnp.exp2(sk - m) for sk, m in zip(sks, ms)]
    vals = [values_of(h) for h in heads]
    oa = [_dot(p[:half], v[0]) for p, v in zip(ps, vals)]
    ob = [_dot(p[half:], v[1]) for p, v in zip(ps, vals)]
    for h, a, b, e in zip(heads, oa, ob, extra):
        a = a / (pltpu.roll(a, HEAD_DIM, 1) + e[:half])
        b = b / (pltpu.roll(b, HEAD_DIM, 1) + e[half:])
        o_ref[:, 2 * h * LANES:(2 * h + 1) * LANES] = jnp.where(lo, a[:BLOCK], b[:BLOCK]).astype(o_ref.dtype)
        o_ref[:, (2 * h + 1) * LANES:(2 * h + 2) * LANES] = jnp.where(lo, a[BLOCK:], b[BLOCK:]).astype(o_ref.dtype)


def _attention(z, sink, nbatch, t, nctx):
    nb = t // BLOCK
    rpb = (nctx + t) // BLOCK
    cpb = nctx // BLOCK

    def lat(off):
        def f(b, i, h):
            return b * rpb + cpb + jnp.clip(i + off, 0, nb - 1)
        return f

    kw = ATT_HEADS * HEAD_DIM
    qw = ATT_HEADS * GQA_GROUP * HEAD_DIM

    def kspec(off, col0):
        return pl.BlockSpec((BLOCK, kw), lambda b, i, h: (lat(off)(b, i, h), col0 // kw + h))

    def xspec(col0):
        return pl.BlockSpec((nctx, kw), lambda b, i, h: (b * (rpb * BLOCK // nctx), col0 // kw + h))

    kcol, vcol = Z_KC, Z_VC
    return pl.pallas_call(
        functools.partial(_attn_kernel, nb=nb),
        grid=(nbatch, nb, N_KV_HEADS // ATT_HEADS),
        in_specs=[
            pl.BlockSpec(memory_space=pltpu.SMEM),
            pl.BlockSpec((BLOCK, qw), lambda b, i, h: (lat(0)(b, i, h), Z_Q // qw + h)),
            kspec(-1, kcol), kspec(0, kcol), kspec(1, kcol),
            kspec(-1, vcol), kspec(0, vcol), kspec(1, vcol),
            xspec(kcol), xspec(vcol),
        ],
        out_specs=pl.BlockSpec((BLOCK, qw), lambda b, i, h: (b * nb + i, h)),
        out_shape=jax.ShapeDtypeStruct((nbatch * t, D_ATTN), BF16),
        compiler_params=_cparams(("parallel", "parallel", "parallel")),
        name="attention",
    )(sink, z, z, z, z, z, z, z, z, z)


def _head_ones():
    r = lax.broadcasted_iota(jnp.int32, (LANES, LANES), 0)
    c = lax.broadcasted_iota(jnp.int32, (LANES, LANES), 1)
    return jnp.where((r // HEAD_DIM) == (c // HEAD_DIM), 1.0, 0.0).astype(F32)


def _prep_kernel(zm_ref, zp_ref, zn_ref, lm_ref, lp_ref, ln_ref, mup_ref, mun_ref, lmup_ref, lmun_ref,
                 w0_ref, wupf_ref, wupb_ref, a0_ref, aupf_ref, aupb_ref, kk_ref, ka_ref, gup_ref,
                 r_o, v_o, lwf_o, lwb_o, kdf_o, kdb_o, nkk_o, bf_o, bb_o, gate_o, *, nblk):
    li = pl.program_id(0) % nblk
    keep_p = jnp.where((li == 0) | (li == 1), 0.0, 1.0)
    keep_n = jnp.where((li == 0) | (li == nblk - 1), 0.0, 1.0)
    row8 = lax.broadcasted_iota(jnp.int32, (SUBLANES, LANES), 0)

    def shift(refs, c0):
        m_ref, p_ref, n_ref, mp_ref, mn_ref = refs
        sl = slice(c0, c0 + LANES)
        zc = m_ref[:, sl]
        down = pltpu.roll(zc, 1, 0)
        up = pltpu.roll(zc, ROWS - 1, 0)
        last = SUBLANES - 1
        prev = jnp.concatenate([jnp.where(row8 == 0, p_ref[last:, sl] * keep_p, down[:SUBLANES]), down[SUBLANES:]],
                               axis=0)
        nxt = jnp.concatenate([up[:-SUBLANES], jnp.where(row8 == last, n_ref[0:1, sl] * keep_n, up[-SUBLANES:])],
                              axis=0)
        mp, mn = mp_ref[:, sl], mn_ref[:, sl]
        return zc * (1.0 - mp - mn) + mp * prev + mn * nxt

    shifted = functools.partial(shift, (zm_ref, zp_ref, zn_ref, mup_ref, mun_ref))
    lora = functools.partial(shift, (lm_ref, lp_ref, ln_ref, lmup_ref, lmun_ref))
    ones = _head_ones()
    th_w = jnp.tanh(lora(0))
    ad = lora(LANES)
    sg = jnp.concatenate([_sigmoid(lora(2 * LANES)), _sigmoid(lora(3 * LANES))], axis=1)

    def log_decay(pre):
        return -math.exp(-0.5) * _sigmoid(pre)

    th_b = th_w.astype(BF16)
    ad_b = ad.astype(BF16)
    sg_b = sg.astype(BF16)

    def lora_fine(xb, w):
        hi, lo = _split(w, 2)
        both = _dot(xb, jnp.concatenate([hi, lo], axis=1))
        return both[:, :LANES] + both[:, LANES:]

    for s in range(D_RWKV // LANES):
        sl = slice(s * LANES, (s + 1) * LANES)
        r = shifted(Z_R + s * LANES)
        k = shifted(Z_K + s * LANES)
        v = shifted(Z_V + s * LANES)
        r_o[:, sl] = r.astype(r_o.dtype)
        v_o[:, sl] = v.astype(v_o.dtype)
        kk = k * kk_ref[:, sl]
        ss = _mm(kk * kk, ones)
        kk = kk * lax.rsqrt(jnp.maximum(ss, KK_EPS))
        nkk_o[:, sl] = (-kk).astype(nkk_o.dtype)
        lwf_o[:, sl] = log_decay(w0_ref[0:1, sl] + lora_fine(th_b, wupf_ref[:, sl]))
        lwb_o[:, sl] = log_decay(w0_ref[1:2, sl] + lora_fine(th_b, wupb_ref[:, sl]))
        a_pre = _dot(ad_b, jnp.concatenate([aupf_ref[:, sl], aupb_ref[:, sl]], axis=1).astype(BF16))
        af = _sigmoid(a0_ref[0:1, sl] + a_pre[:, :LANES])
        ab = _sigmoid(a0_ref[1:2, sl] + a_pre[:, LANES:])
        ka = ka_ref[:, sl]
        kdf_o[:, sl] = (k * (1.0 + (af - 1.0) * ka)).astype(kdf_o.dtype)
        kdb_o[:, sl] = (k * (1.0 + (ab - 1.0) * ka)).astype(kdb_o.dtype)
        bf_o[:, sl] = (kk * af).astype(bf_o.dtype)
        bb_o[:, sl] = (kk * ab).astype(bb_o.dtype)
        gate_o[:, sl] = _dot(sg_b, gup_ref[:, sl].astype(BF16)).astype(gate_o.dtype)


def _rwkv_prep(z, mup, mun, lmup, lmun, w0, wupf, wupb, a0, aupf, aupb, k_k, k_a, gup, nblk):
    n = z.shape[0]
    nsub = ROWS // SUBLANES
    full = lambda a: pl.BlockSpec(a.shape, lambda i: (0,) * a.ndim)
    outs = [jax.ShapeDtypeStruct((n, D_RWKV), F32 if name in ("lwf", "lwb") else BF16)
            for name in ("r", "v", "lwf", "lwb", "kdf", "kdb", "nkk", "bf", "bb", "gate")]
    ospec = pl.BlockSpec((ROWS, D_RWKV), lambda i: (i, 0))
    return pl.pallas_call(
        functools.partial(_prep_kernel, nblk=nblk),
        grid=(n // ROWS,),
        in_specs=[
            pl.BlockSpec((ROWS, Z_RKV_W), lambda i: (i, 0)),
            pl.BlockSpec((SUBLANES, Z_RKV_W), lambda i: (jnp.maximum(i * nsub - 1, 0), 0)),
            pl.BlockSpec((SUBLANES, Z_RKV_W), lambda i: (jnp.minimum((i + 1) * nsub, n // SUBLANES - 1), 0)),
            pl.BlockSpec((ROWS, Z_LORA_W), lambda i: (i, Z_LORA // Z_LORA_W)),
            pl.BlockSpec((SUBLANES, Z_LORA_W), lambda i: (jnp.maximum(i * nsub - 1, 0), Z_LORA // Z_LORA_W)),
            pl.BlockSpec((SUBLANES, Z_LORA_W),
                         lambda i: (jnp.minimum((i + 1) * nsub, n // SUBLANES - 1), Z_LORA // Z_LORA_W)),
            full(mup), full(mun), full(lmup), full(lmun),
            full(w0), full(wupf), full(wupb), full(a0), full(aupf), full(aupb),
            full(k_k), full(k_a), full(gup),
        ],
        out_specs=[ospec] * 10,
        out_shape=outs,
        compiler_params=_cparams(("parallel",)),
        name="rwkv_prep",
    )(z, z, z, z, z, z, mup, mun, lmup, lmun, w0, wupf, wupb, a0, aupf, aupb, k_k, k_a, gup)


def _wkv_intra(probs, bd, eye_bd, head_a, head_a2):
    n = CHUNK
    zero = jnp.zeros((n, LANES), F32)
    each = lambda f, *ls: [f(*xs) for xs in zip(*ls)]

    def stack(x):
        return jnp.concatenate([jnp.where(head_a, x, 0.0), jnp.where(head_a, 0.0, x)], axis=0)

    def cumulative(p):
        inclb = p[6].astype(BF16)
        l0, l1, l2 = _split(p[1], 3)
        return _dot(inclb, l0) + (_dot(inclb, l1) + _dot(inclb, l2))

    cs = each(cumulative, probs)

    def factors(p, c):
        r, lw, k, v, a, b = p[:6]
        ctot = jnp.sum(lw, axis=0, keepdims=True)
        mid = 0.5 * ctot
        e_in = jnp.exp(c - mid)
        e_out = jnp.exp(mid - c)
        e_abs = jnp.exp(c)
        e_rest = jnp.exp(ctot - c)
        undo = jnp.exp(-lw)
        lhs = jnp.concatenate([stack(a * e_in * undo), stack(r * e_in)], axis=0)
        rhs = jnp.concatenate([b * e_out, k * e_out], axis=0)
        rest = jnp.concatenate([b * e_rest, k * e_rest], axis=0)
        return lhs, rhs, rest, a * e_abs * undo, r * e_abs, jnp.exp(ctot)

    fs = each(factors, probs, cs)
    grams = [_mm(f[0], f[1], nt=True) for f in fs]
    aas = [jnp.where(p[7], g[:2 * n], 0.0) for p, g in zip(probs, grams)]
    ars = [jnp.where(p[8], g[2 * n:], 0.0) for p, g in zip(probs, grams)]
    a_abs = [jnp.where(bd, jnp.concatenate([aa[:n], pltpu.roll(aa[n:], n, 1)], axis=0), 0.0) for aa in aas]

    invs = [eye_bd + x for x in a_abs]
    pws = [_mm(x, x) for x in a_abs]
    for _ in range(int(np.log2(n)) - 2):
        boths = [_mm(jnp.concatenate([pw, inv], axis=0), pw) for pw, inv in zip(pws, invs)]
        pws = [x[:2 * n] for x in boths]
        invs = [inv + x[2 * n:] for inv, x in zip(invs, boths)]
    invs = [inv + _mm(inv, pw) for inv, pw in zip(invs, pws)]

    akvs = [jnp.where(bd, _mm(aa, jnp.concatenate([zero, p[3]], axis=0)), 0.0)
            for aa, p in zip(aas, probs)]
    wu_sts = [_mm(inv, jnp.concatenate([stack(f[3]), akv], axis=1))
              for inv, f, akv in zip(invs, fs, akvs)]
    m2s = [jnp.concatenate([x[:n] + x[n:], jnp.concatenate([zero, p[3]], axis=1)], axis=0)
           for x, p in zip(wu_sts, probs)]
    qy_sts = [_mm(ar, m2) for ar, m2 in zip(ars, m2s)]
    ghs = [_mm(m2.T, f[2]) for m2, f in zip(m2s, fs)]
    out = []
    for f, qy_st, gh in zip(fs, qy_sts, ghs):
        qy = jnp.where(head_a2, qy_st[:n], qy_st[n:])
        out.append((f[4] + qy[:, :LANES], qy[:, LANES:],
                    jnp.where(bd, gh[:LANES], 0.0) + eye_bd * f[5], jnp.where(bd, gh[LANES:], 0.0)))
    return out


def _wkv_kernel(rf, vf, af, lwf, kf, bf, rb, vb, ab, lwb, kb, bb, yf_o, yb_o, s_ref, q_ref, y0_ref, gm_ref, h_ref):
    j = pl.program_id(1)
    n = CHUNK
    nsub = ROWS // n
    cur = j % 2
    prv = 1 - cur

    @pl.when(j == 0)
    def _():
        s_ref[...] = jnp.zeros_like(s_ref)
        q_ref[1] = jnp.zeros(q_ref.shape[1:], F32)
        y0_ref[1] = jnp.zeros(y0_ref.shape[1:], F32)
        gm_ref[1] = jnp.zeros(gm_ref.shape[1:], F32)
        h_ref[1] = jnp.zeros(h_ref.shape[1:], F32)

    for step in range(nsub):
        for d, y_o in enumerate((yf_o, yb_o)):
            for p in range(WKV_PAIRS):
                u = step if d == 0 else nsub - 1 - step
                idx = (d * WKV_PAIRS + p) * nsub + u
                s0 = s_ref[d * WKV_PAIRS + p]
                y_o[u * n:(u + 1) * n, p * LANES:(p + 1) * LANES] = (
                    _mm(q_ref[prv, idx], s0, nt=True) + y0_ref[prv, idx]).astype(y_o.dtype)
                s_ref[d * WKV_PAIRS + p] = _mm(s0, gm_ref[prv, idx]) + h_ref[prv, idx]

    ti = lax.broadcasted_iota(jnp.int32, (n, n), 0)
    si = lax.broadcasted_iota(jnp.int32, (n, n), 1)
    t2 = lax.broadcasted_iota(jnp.int32, (2 * n, 2 * n), 0)
    s2 = lax.broadcasted_iota(jnp.int32, (2 * n, 2 * n), 1)
    tw, sw = t2 % n, s2 % n
    bd = (t2 // n) == (s2 // n)
    eye_bd = jnp.where(t2 == s2, 1.0, 0.0).astype(F32)
    head_a = lax.broadcasted_iota(jnp.int32, (n, LANES), 1) < HEAD_DIM
    head_a2 = lax.broadcasted_iota(jnp.int32, (n, 2 * LANES), 1) % LANES < HEAD_DIM
    masks = (
        (jnp.where(si <= ti, 1.0, 0.0).astype(F32), sw < tw, sw <= tw),
        (jnp.where(si >= ti, 1.0, 0.0).astype(F32), sw > tw, sw >= tw),
    )
    srcs = ((rf, lwf, kf, vf, af, bf), (rb, lwb, kb, vb, ab, bb))

    probs = [tuple(x[u * n:(u + 1) * n, p * LANES:(p + 1) * LANES].astype(F32) for x in srcs[d]) + masks[d]
             for d in range(2) for p in range(WKV_PAIRS) for u in range(nsub)]
    for idx, (q, y0, gm, h) in enumerate(_wkv_intra(probs, bd, eye_bd, head_a, head_a2)):
        q_ref[cur, idx] = q
        y0_ref[cur, idx] = y0
        gm_ref[cur, idx] = gm
        h_ref[cur, idx] = h


def _wkv(r, v, nkk, lwf, kdf, bfw, lwb, kdb, bbw, nbatch, nblk):
    n = r.shape[0]
    ngrp = D_RWKV // (WKV_PAIRS * LANES)
    nprob = 2 * WKV_PAIRS * (ROWS // CHUNK)
    last = nblk - 1
    width = WKV_PAIRS * LANES

    def bwd_blk(j):
        return jnp.where(j == 0, 0, nblk - j)

    def spec(blk):
        return pl.BlockSpec((ROWS, width), lambda g, j: ((g // ngrp) * nblk + blk(j), g % ngrp))

    fin = spec(lambda j: jnp.minimum(j, last))
    bin_ = spec(lambda j: bwd_blk(jnp.minimum(j, last)))
    fout = spec(lambda j: jnp.maximum(j - 1, 0))
    bout = spec(lambda j: bwd_blk(jnp.maximum(j - 1, 0)))
    out = jax.ShapeDtypeStruct((n, D_RWKV), BF16)
    return pl.pallas_call(
        _wkv_kernel,
        grid=(nbatch * ngrp, nblk + 1),
        in_specs=[fin] * 6 + [bin_] * 6,
        out_specs=[fout, bout],
        out_shape=[out, out],
        scratch_shapes=[pltpu.VMEM((2 * WKV_PAIRS, LANES, LANES), F32),
                        pltpu.VMEM((2, nprob, CHUNK, LANES), F32), pltpu.VMEM((2, nprob, CHUNK, LANES), F32),
                        pltpu.VMEM((2, nprob, LANES, LANES), F32), pltpu.VMEM((2, nprob, LANES, LANES), F32)],
        compiler_params=_cparams(("parallel", "arbitrary")),
        name="wkv",
    )(r, v, nkk, lwf, kdf, bfw, r, v, nkk, lwb, kdb, bbw)


def _rwkv_mix(yf_ref, yb_ref, r_ref, v_ref, kdf_ref, kdb_ref, gate_ref, rk_ref, lg_ref, lb_ref):
    ones = _head_ones().astype(BF16)
    mean_mat = (_head_ones() * (1.0 / HEAD_DIM)).astype(BF16)
    slabs = [slice(s * LANES, (s + 1) * LANES) for s in range(D_RWKV // LANES)]
    n = yf_ref.shape[0]

    def head_mean(y):
        hi, lo = _split(y, 2)
        both = _dot(jnp.concatenate([hi, lo], axis=0), mean_mat)
        return both[:n] + both[n:]

    ys = [yf_ref[:, sl].astype(F32) + yb_ref[:, sl].astype(F32) for sl in slabs]
    ycs = [y - head_mean(y) for y in ys]
    variances = [_dot((yc * yc).astype(BF16), mean_mat) for yc in ycs]
    sums = [_dot((r_ref[:, sl].astype(F32) * (kdf_ref[:, sl].astype(F32) + kdb_ref[:, sl].astype(F32))
                  * rk_ref[:, sl]).astype(BF16), ones) for sl in slabs]
    out = []
    for sl, yc, var, sm in zip(slabs, ycs, variances, sums):
        yn = yc * lax.rsqrt(var + LNX_EPS) * lg_ref[:, sl] + lb_ref[:, sl]
        bonus = sm * v_ref[:, sl].astype(F32)
        out.append(((yn + bonus) * gate_ref[:, sl].astype(F32)).astype(BF16))
    return jnp.concatenate(out, axis=1)


def _outproj_kernel(x_ref, attn_ref, yf_ref, yb_ref, r_ref, v_ref, kdf_ref, kdb_ref, gate_ref, rk_ref, lg_ref, lb_ref,
                    g1_ref, g_ref, sh_ref, sc_ref, wa_ref, wr_ref, o_ref, h_ref):
    rw = _rwkv_mix(yf_ref, yb_ref, r_ref, v_ref, kdf_ref, kdb_ref, gate_ref, rk_ref, lg_ref, lb_ref)
    mix = _dot(attn_ref[...], wa_ref[...]) + _dot(rw, wr_ref[...])
    x1 = x_ref[...] + g1_ref[0] * mix
    o_ref[...] = x1
    ms = jnp.mean(x1 * x1, axis=-1, keepdims=True)
    h = x1 * lax.rsqrt(ms + NORM_EPS) * g_ref[...]
    h_ref[...] = (h * (1.0 + sc_ref[0]) + sh_ref[0]).astype(h_ref.dtype)


def _outproj(x2, attn, yf, yb, r, v, kdf, kdb, gate, r_k, lnx_g, lnx_b, g1, g, sh, sc, wa, wr, t, nblk):
    n = x2.shape[0]
    lpb = t // ROWS
    modspec = pl.BlockSpec((1, 1, D_MODEL), lambda i: (i // lpb, 0, 0))
    rowspec = pl.BlockSpec((ROWS, D_MODEL), lambda i: (i, 0))
    seqspec = pl.BlockSpec((ROWS, D_RWKV), lambda i: ((i // lpb) * nblk + 1 + i % lpb, 0))
    pspec = pl.BlockSpec((1, D_RWKV), lambda i: (0, 0))
    return pl.pallas_call(
        _outproj_kernel,
        grid=(n // ROWS,),
        in_specs=[
            rowspec,
            pl.BlockSpec((ROWS, D_ATTN), lambda i: (i, 0)),
        ] + [seqspec] * 7 + [pspec] * 3 + [
            modspec,
            pl.BlockSpec((1, D_MODEL), lambda i: (0, 0)),
            modspec, modspec,
            pl.BlockSpec((D_ATTN, D_MODEL), lambda i: (0, 0), pipeline_mode=pl.Buffered(1)),
            pl.BlockSpec((D_RWKV, D_MODEL), lambda i: (0, 0), pipeline_mode=pl.Buffered(1)),
        ],
        out_specs=[rowspec, rowspec],
        out_shape=[jax.ShapeDtypeStruct((n, D_MODEL), F32), jax.ShapeDtypeStruct((n, D_MODEL), BF16)],
        compiler_params=_cparams(("parallel",)),
        name="outproj",
    )(x2, attn, yf, yb, r, v, kdf, kdb, gate, r_k, lnx_g, lnx_b, g1, g, sh, sc, wa, wr)


def _ffn_kernel(x_ref, h_ref, g2_ref, fg_ref, w1_ref, w3_ref, w2_ref, o_ref, acc_ref):
    j = pl.program_id(1)

    @pl.when(j == 0)
    def _():
        acc_ref[...] = jnp.zeros_like(acc_ref)

    h = h_ref[...]
    a = _dot(h, w1_ref[...])
    b = _dot(h, w3_ref[...])
    u = a * _sigmoid(a) * b
    acc_ref[...] += _dot(u.astype(BF16), w2_ref[...])

    @pl.when(j == pl.num_programs(1) - 1)
    def _():
        y = x_ref[...] + g2_ref[0] * acc_ref[...]
        ms = jnp.mean(y * y, axis=-1, keepdims=True)
        o_ref[...] = y * lax.rsqrt(ms + NORM_EPS) * fg_ref[...]


def _ffn(x1, h2, g2, fg, w1, w3, w2, t):
    n = x1.shape[0]
    tm, tf = FFN_TM, FFN_TF
    modspec = pl.BlockSpec((1, 1, D_MODEL), lambda i, j: (i // (t // tm), 0, 0))
    vecspec = pl.BlockSpec((1, D_MODEL), lambda i, j: (0, 0))
    return pl.pallas_call(
        _ffn_kernel,
        grid=(n // tm, D_FF // tf),
        in_specs=[
            pl.BlockSpec((tm, D_MODEL), lambda i, j: (i, 0)),
            pl.BlockSpec((tm, D_MODEL), lambda i, j: (i, 0)),
            modspec, vecspec,
            pl.BlockSpec((D_MODEL, tf), lambda i, j: (0, j)),
            pl.BlockSpec((D_MODEL, tf), lambda i, j: (0, j)),
            pl.BlockSpec((tf, D_MODEL), lambda i, j: (j, 0)),
        ],
        out_specs=pl.BlockSpec((tm, D_MODEL), lambda i, j: (i, 0)),
        out_shape=jax.ShapeDtypeStruct((n, D_MODEL), F32),
        scratch_shapes=[pltpu.VMEM((tm, D_MODEL), F32)],
        compiler_params=_cparams(("parallel", "arbitrary")),
        name="ffn",
    )(x1, h2, g2, fg, w1, w3, w2)


def _z_weights_kernel(w_ref, o_ref):
    q0, k0, v0, r0 = 0, D_ATTN, D_ATTN + Z_KV_W, D_ATTN + 2 * Z_KV_W
    lora0 = r0 + Z_RKV_W
    n_lora = 4 * D_LORA + D_GATE_LORA
    cast = lambda x: x.astype(o_ref.dtype)
    o_ref[:, Z_R:Z_R + Z_RKV_W] = cast(w_ref[:, r0:lora0])
    o_ref[:, Z_Q:Z_Q + D_ATTN] = cast(w_ref[:, q0:q0 + D_ATTN] * (HEAD_DIM ** -0.5 * LOG2E))
    o_ref[:, Z_LORA:Z_LORA + Z_LORA_W] = jnp.zeros((o_ref.shape[0], Z_LORA_W), o_ref.dtype)
    o_ref[:, Z_LORA:Z_LORA + n_lora] = cast(w_ref[:, lora0:lora0 + n_lora])
    o_ref[:, Z_KC:Z_KC + Z_KV_W] = cast(w_ref[:, k0:k0 + Z_KV_W])
    o_ref[:, Z_VC:Z_VC + Z_KV_W] = cast(w_ref[:, v0:v0 + Z_KV_W])


def _z_weights(w_in):
    rows = WPREP_ROWS
    return pl.pallas_call(
        _z_weights_kernel,
        grid=(D_MODEL // rows,),
        in_specs=[pl.BlockSpec((None, rows, w_in.shape[2]), lambda i: (0, i, 0))],
        out_specs=pl.BlockSpec((rows, NZ), lambda i: (i, 0)),
        out_shape=jax.ShapeDtypeStruct((D_MODEL, NZ), BF16),
        compiler_params=_cparams(("parallel",)),
        name="z_weights",
    )(w_in)


def _rope_tables(t):
    pos = np.arange(t)
    row = (pos // GRID_W).astype(np.float32)
    col = (pos % GRID_W).astype(np.float32)
    nf = HEAD_DIM // 4
    freqs = (np.float32(1.0) / np.float32(ROPE_BASE) ** (np.arange(nf, dtype=np.float32) / np.float32(nf))).astype(np.float32)
    d = np.arange(LANES) % HEAD_DIM
    use_row = (d // (HEAD_DIM // 2)) == 0
    first = (d % (HEAD_DIM // 2)) < nf
    ang = (np.where(use_row[None, :], row[:, None], col[:, None]) * freqs[d % nf][None, :]).astype(np.float32)
    sin = np.sin(ang)
    return jnp.asarray(np.cos(ang), F32), jnp.asarray(np.where(first[None, :], -sin, sin), F32)


def _pad_rows(w, top, total):
    return jnp.concatenate([jnp.zeros((top, w.shape[1]), w.dtype), w,
                            jnp.zeros((total - top - w.shape[0], w.shape[1]), w.dtype)], axis=0)


def kernel(x, c, ctx, c_ctx, ada_w, ada_b, norm1_g, norm2_g, w_in, attn_sink, ts_prev, ts_next, w0, w_up, a0, a_up,
           k_k, k_a, r_k, g_up, lnx_g, lnx_b, w_out, ffn_w1, ffn_w3, ffn_w2, final_norm_g):
    nbatch, t, d = x.shape
    nctx = ctx.shape[1]
    assert d == D_MODEL and nctx == ROWS and t % FFN_TM == 0 and nbatch < SUBLANES and ada_w.shape[0] == 1
    assert BLOCK == WINDOW and ATT_HEADS == N_KV_HEADS
    nblk = (nctx + t) // ROWS

    cc = jnp.concatenate([c, c_ctx[None, :], jnp.zeros((SUBLANES - nbatch - 1, d), F32)], axis=0)
    mod = _adaln(cc, ada_w[0], ada_b[0])
    mods = mod[:nbatch + 1].reshape(nbatch + 1, 6, 1, d)
    sh1, sc1, g1, sh2, sc2, g2 = (mods[:, i] for i in range(6))

    x2 = x.reshape(nbatch * t, d)
    cos_t, sin_t = _rope_tables(t)
    z = _inproj(x2, ctx.reshape(nbatch * nctx, d), norm1_g, sh1, sc1, cos_t, sin_t, _z_weights(w_in), nbatch, nblk)

    attn = _attention(z, attn_sink[0], nbatch, t, nctx)

    n_lora = ts_prev.shape[1] - Z_RKV_W
    pad_lora = lambda m: jnp.pad(m[:, Z_RKV_W:], ((0, 0), (0, Z_LORA_W - n_lora)))
    wupf = _pad_rows(w_up[0, 0], 0, LANES)
    wupb = _pad_rows(w_up[0, 1], D_LORA, LANES)
    aupf = _pad_rows(a_up[0, 0], 0, LANES)
    aupb = _pad_rows(a_up[0, 1], D_LORA, LANES)
    gup = _pad_rows(g_up[0], 0, 2 * LANES)
    r, v, lwf, lwb, kdf, kdb, nkk, bfw, bbw, gate = _rwkv_prep(
        z, ts_prev[:, :Z_RKV_W], ts_next[:, :Z_RKV_W], pad_lora(ts_prev), pad_lora(ts_next),
        w0[0], wupf, wupb, a0[0], aupf, aupb, k_k, k_a, gup, nblk)
    yf, yb = _wkv(r, v, nkk, lwf, kdf, bfw, lwb, kdb, bbw, nbatch, nblk)

    wo = w_out[0].astype(BF16)
    x1, h2 = _outproj(x2, attn, yf, yb, r, v, kdf, kdb, gate, r_k, lnx_g, lnx_b,
                      g1, norm2_g, sh2, sc2, wo[:D_ATTN], wo[D_ATTN:], t, nblk)
    out = _ffn(x1, h2, g2, final_norm_g.reshape(1, d),
               ffn_w1[0].astype(BF16), ffn_w3[0].astype(BF16), ffn_w2[0].astype(BF16), t)
    return out.reshape(nbatch, t, d)
```

```python
import functools
import math

import numpy as np
import jax
import jax.numpy as jnp
from jax import lax
from jax.experimental import pallas as pl
from jax.experimental.pallas import tpu as pltpu

F32 = jnp.float32
BF16 = jnp.bfloat16

D_MODEL = 2048
HEAD_DIM = 64
N_KV_HEADS = 4
GQA_GROUP = 4
D_ATTN = 1024
D_RWKV = 1024
D_LORA = 64
D_GATE_LORA = 160
D_FF = 5632
GRID_W = 64
WINDOW = 128
BLOCK = 128
ROPE_BASE = 10000.0
NORM_EPS = 1e-6
LNX_EPS = 64e-5
MASK_VALUE = -1e30
KK_EPS = 1e-12
ROT_HALF = HEAD_DIM // 2
ROT_PAIR = HEAD_DIM // 4
LOG2E = math.log2(math.e)

LANES = 128
SUBLANES = 8
HALO = 2 * SUBLANES
V7X_VMEM_BYTES = 64 * 2 ** 20
VMEM_LIMIT = V7X_VMEM_BYTES - 8 * 2 ** 20

ROWS = 256
CHUNK = 64
WKV_PAIRS = 4
ATT_HEADS = 4
ADALN_TN = 1024
PROJ_TN = 512
FFN_TM, FFN_TF = 512, 512
WPREP_ROWS = 256

Z_R, Z_K, Z_V = 0, 1024, 2048
Z_RKV_W = 3072
Z_Q = 3072
Z_LORA = 4096
Z_LORA_W = 512
Z_KV_W = N_KV_HEADS * HEAD_DIM
Z_KC, Z_VC = 4608, 4864
NZ = 5120
ROPE_SLABS = tuple(range(Z_Q, Z_Q + D_ATTN, LANES)) + tuple(range(Z_KC, Z_VC, LANES))


def _cparams(sem):
    return pltpu.CompilerParams(dimension_semantics=sem, vmem_limit_bytes=VMEM_LIMIT)


def _sigmoid(x):
    return 0.5 * jnp.tanh(0.5 * x) + 0.5


def _dot(a, b):
    return jnp.dot(a, b, preferred_element_type=F32)


def _dot_nt(a, b):
    return lax.dot_general(a, b, (((1,), (1,)), ((), ())), preferred_element_type=F32)


def _split(x, terms):
    out = []
    for _ in range(terms - 1):
        hi = x.astype(BF16)
        out.append(hi)
        x = x - hi.astype(F32)
    out.append(x.astype(BF16))
    return out


def _mm(a, b, nt=False):
    return (_dot_nt if nt else _dot)(a.astype(BF16), b.astype(BF16))


def _mm_fine(a, b):
    ah, al = _split(a, 2)
    bh, bl = _split(b, 2)
    return _dot(ah, bh) + (_dot(ah, bl) + _dot(al, bh))


def _adaln_kernel(c_ref, w_ref, b_ref, o_ref):
    c = c_ref[...]
    s = c * _sigmoid(c)
    o_ref[...] = _mm_fine(s, w_ref[...]) + b_ref[...]


def _adaln(cc, w, b):
    n = w.shape[1]
    tn = ADALN_TN
    return pl.pallas_call(
        _adaln_kernel,
        grid=(n // tn,),
        in_specs=[
            pl.BlockSpec((SUBLANES, D_MODEL), lambda j: (0, 0)),
            pl.BlockSpec((D_MODEL, tn), lambda j: (0, j)),
            pl.BlockSpec((1, tn), lambda j: (0, j)),
        ],
        out_specs=pl.BlockSpec((SUBLANES, tn), lambda j: (0, j)),
        out_shape=jax.ShapeDtypeStruct((SUBLANES, n), F32),
        compiler_params=_cparams(("parallel",)),
        name="adaln",
    )(cc, w, b.reshape(1, n))


def _inproj_kernel(x_ref, ctx_ref, g_ref, sh_ref, sc_ref, cos_ref, sin_ref, w_ref, z_ref, shw_ref, *, nblk):
    li = pl.program_id(0) % nblk
    is_ctx = li == 0

    @pl.when(li <= 1)
    def _():
        shb = jnp.broadcast_to(sh_ref[0], (SUBLANES, D_MODEL)).astype(BF16)
        for j in range(NZ // PROJ_TN):
            sl = slice(j * PROJ_TN, (j + 1) * PROJ_TN)
            shw_ref[:, sl] = _dot(shb, w_ref[:, sl])

    x = jnp.where(is_ctx, ctx_ref[...], x_ref[...])
    xb = (x * (g_ref[...] * (1.0 + sc_ref[0]))).astype(BF16)
    rs = lax.rsqrt(jnp.mean(x * x, axis=-1, keepdims=True) + NORM_EPS)
    cs = cos_ref[...]
    sn = sin_ref[...]
    first = (lax.broadcasted_iota(jnp.int32, (ROWS, LANES), 1) % ROT_HALF) < ROT_PAIR
    for j in range(NZ // PROJ_TN):
        sl = slice(j * PROJ_TN, (j + 1) * PROJ_TN)
        zt = _dot(xb, w_ref[:, sl]) * rs + shw_ref[0:1, sl]
        for c0 in range(j * PROJ_TN, (j + 1) * PROJ_TN, LANES):
            zz = zt[:, c0 - j * PROJ_TN:c0 - j * PROJ_TN + LANES]
            if c0 in ROPE_SLABS:
                partner = jnp.where(first, pltpu.roll(zz, LANES - ROT_PAIR, 1), pltpu.roll(zz, ROT_PAIR, 1))
                zz = jnp.where(is_ctx, zz, zz * cs + partner * sn)
            z_ref[:, c0:c0 + LANES] = zz.astype(z_ref.dtype)


def _inproj(x2, ctx2, g, sh, sc, cos_t, sin_t, wz, nbatch, nblk):
    n = nbatch * nblk * ROWS
    lpb = nblk - 1

    def sel(i):
        return jnp.where(i % nblk == 0, nbatch, i // nblk)

    return pl.pallas_call(
        functools.partial(_inproj_kernel, nblk=nblk),
        grid=(n // ROWS,),
        in_specs=[
            pl.BlockSpec((ROWS, D_MODEL), lambda i: ((i // nblk) * lpb + jnp.maximum(i % nblk - 1, 0), 0)),
            pl.BlockSpec((ROWS, D_MODEL), lambda i: (i // nblk, 0)),
            pl.BlockSpec((1, D_MODEL), lambda i: (0, 0)),
            pl.BlockSpec((1, 1, D_MODEL), lambda i: (sel(i), 0, 0)),
            pl.BlockSpec((1, 1, D_MODEL), lambda i: (sel(i), 0, 0)),
            pl.BlockSpec((ROWS, LANES), lambda i: (jnp.maximum(i % nblk - 1, 0), 0)),
            pl.BlockSpec((ROWS, LANES), lambda i: (jnp.maximum(i % nblk - 1, 0), 0)),
            pl.BlockSpec((D_MODEL, NZ), lambda i: (0, 0), pipeline_mode=pl.Buffered(1)),
        ],
        out_specs=pl.BlockSpec((ROWS, NZ), lambda i: (i, 0)),
        out_shape=jax.ShapeDtypeStruct((n, NZ), BF16),
        scratch_shapes=[pltpu.VMEM((SUBLANES, NZ), F32)],
        compiler_params=_cparams(("arbitrary",)),
        name="inproj",
    )(x2, ctx2, g, sh, sc, cos_t, sin_t, wz)


def _attn_kernel(sink_ref, q_ref, kp_ref, kc_ref, kn_ref, vp_ref, vc_ref, vn_ref, kx_ref, vx_ref, o_ref, *, nb):
    i = pl.program_id(1)
    hg = pl.program_id(2)
    lane = lax.broadcasted_iota(jnp.int32, (BLOCK, LANES), 1)
    lo = lane < HEAD_DIM
    heads = range(ATT_HEADS)

    qoff = lax.broadcasted_iota(jnp.int32, (GQA_GROUP * BLOCK, BLOCK), 0) % BLOCK
    koff = lax.broadcasted_iota(jnp.int32, (GQA_GROUP * BLOCK, BLOCK), 1)
    see_prev = jnp.where(i > 0, jnp.where(koff >= qoff, 1, 0), 0) > 0
    see_next = jnp.where(i < nb - 1, jnp.where(koff <= qoff, 1, 0), 0) > 0

    def masked(s):
        return jnp.concatenate([jnp.where(see_prev, s[:, :BLOCK], MASK_VALUE), s[:, BLOCK:2 * BLOCK],
                                jnp.where(see_next, s[:, 2 * BLOCK:3 * BLOCK], MASK_VALUE), s[:, 3 * BLOCK:]], axis=1)
    rowc = lax.broadcasted_iota(jnp.int32, (GQA_GROUP * BLOCK, 1), 0)

    order = (0, 2, 1, 3)

    def queries(h):
        parts = []
        for g in order:
            c0 = (h * GQA_GROUP + g) // 2 * LANES
            qp = q_ref[:, c0:c0 + LANES]
            parts.append(jnp.where(lo, qp, 0.0) if g % 2 == 0 else jnp.where(lo, 0.0, qp))
        return jnp.concatenate(parts, axis=0).astype(BF16)

    rolled = {}

    def slab_of(r, h):
        key = (id(r), h // 2)
        if key not in rolled:
            x = r[:, (h // 2) * LANES:(h // 2 + 1) * LANES].astype(F32)
            rolled[key] = (x, pltpu.roll(x, HEAD_DIM, 1))
        x, xr = rolled[key]
        return (x, xr) if h % 2 == 0 else (xr, x)

    def keys_of(h):
        parts = []
        for r in (kp_ref, kc_ref, kn_ref, kx_ref):
            on_lo, on_hi = slab_of(r, h)
            parts.append(jnp.where(lax.broadcasted_iota(jnp.int32, on_lo.shape, 1) < HEAD_DIM, on_lo, on_hi))
        return jnp.concatenate(parts, axis=0).astype(BF16)

    def values_of(h):
        va, vb = [], []
        for r in (vp_ref, vc_ref, vn_ref, vx_ref):
            on_lo, on_hi = slab_of(r, h)
            lo_r = lax.broadcasted_iota(jnp.int32, on_lo.shape, 1) < HEAD_DIM
            va.append(jnp.where(lo_r, on_lo, 1.0))
            vb.append(jnp.where(lo_r, 1.0, on_hi))
        return jnp.concatenate(va, axis=0).astype(BF16), jnp.concatenate(vb, axis=0).astype(BF16)

    def sinks(h):
        base = (hg * ATT_HEADS + h) * GQA_GROUP
        return LOG2E * jnp.where(rowc < BLOCK, sink_ref[base + order[0]],
                                 jnp.where(rowc < 2 * BLOCK, sink_ref[base + order[1]],
                                           jnp.where(rowc < 3 * BLOCK, sink_ref[base + order[2]],
                                                     sink_ref[base + order[3]])))

    half = 2 * BLOCK
    ss = [masked(_dot_nt(queries(h), keys_of(h))) for h in heads]
    sks = [sinks(h) for h in heads]
    ms = [jnp.maximum(jnp.max(s, axis=-1, keepdims=True), sk) for s, sk in zip(ss, sks)]
    ps = [jnp.exp2(s - m).astype(BF16) for s, m in zip(ss, ms)]
    extra = [jnp.exp2(sk - m) for sk, m in zip(sks, ms)]
    vals = [values_of(h) for h in heads]
    oa = [_dot(p[:half], v[0]) for p, v in zip(ps, vals)]
    ob = [_dot(p[half:], v[1]) for p, v in zip(ps, vals)]
    for h, a, b, e in zip(heads, oa, ob, extra):
        a = a / (pltpu.roll(a, HEAD_DIM, 1) + e[:half])
        b = b / (pltpu.roll(b, HEAD_DIM, 1) + e[half:])
        o_ref[:, 2 * h * LANES:(2 * h + 1) * LANES] = jnp.where(lo, a[:BLOCK], b[:BLOCK]).astype(o_ref.dtype)
        o_ref[:, (2 * h + 1) * LANES:(2 * h + 2) * LANES] = jnp.where(lo, a[BLOCK:], b[BLOCK:]).astype(o_ref.dtype)


def _attention(z, sink, nbatch, t, nctx):
    nb = t // BLOCK
    rpb = (nctx + t) // BLOCK
    cpb = nctx // BLOCK

    def lat(off):
        def f(b, i, h):
            return b * rpb + cpb + jnp.clip(i + off, 0, nb - 1)
        return f

    kw = ATT_HEADS * HEAD_DIM
    qw = ATT_HEADS * GQA_GROUP * HEAD_DIM

    def kspec(off, col0):
        return pl.BlockSpec((BLOCK, kw), lambda b, i, h: (lat(off)(b, i, h), col0 // kw + h))

    def xspec(col0):
        return pl.BlockSpec((nctx, kw), lambda b, i, h: (b * (rpb * BLOCK // nctx), col0 // kw + h))

    kcol, vcol = Z_KC, Z_VC
    return pl.pallas_call(
        functools.partial(_attn_kernel, nb=nb),
        grid=(nbatch, nb, N_KV_HEADS // ATT_HEADS),
        in_specs=[
            pl.BlockSpec(memory_space=pltpu.SMEM),
            pl.BlockSpec((BLOCK, qw), lambda b, i, h: (lat(0)(b, i, h), Z_Q // qw + h)),
            kspec(-1, kcol), kspec(0, kcol), kspec(1, kcol),
            kspec(-1, vcol), kspec(0, vcol), kspec(1, vcol),
            xspec(kcol), xspec(vcol),
        ],
        out_specs=pl.BlockSpec((BLOCK, qw), lambda b, i, h: (b * nb + i, h)),
        out_shape=jax.ShapeDtypeStruct((nbatch * t, D_ATTN), BF16),
        compiler_params=_cparams(("parallel", "parallel", "parallel")),
        name="attention",
    )(sink, z, z, z, z, z, z, z, z, z)


def _head_ones():
    r = lax.broadcasted_iota(jnp.int32, (LANES, LANES), 0)
    c = lax.broadcasted_iota(jnp.int32, (LANES, LANES), 1)
    return jnp.where((r // HEAD_DIM) == (c // HEAD_DIM), 1.0, 0.0).astype(F32)


def _prep_kernel(zm_ref, zp_ref, zn_ref, lm_ref, lp_ref, ln_ref, mup_ref, mun_ref, lmup_ref, lmun_ref,
                 w0_ref, wupf_ref, wupb_ref, a0_ref, aupf_ref, aupb_ref, kk_ref, ka_ref, gup_ref,
                 r_o, v_o, lwf_o, lwb_o, kdf_o, kdb_o, nkk_o, bf_o, bb_o, gate_o, *, nblk):
    li = pl.program_id(0) % nblk
    keep_p = jnp.where((li == 0) | (li == 1), 0.0, 1.0)
    keep_n = jnp.where((li == 0) | (li == nblk - 1), 0.0, 1.0)
    row8 = lax.broadcasted_iota(jnp.int32, (SUBLANES, LANES), 0)

    def shift(refs, c0):
        m_ref, p_ref, n_ref, mp_ref, mn_ref = refs
        sl = slice(c0, c0 + LANES)
        zc = m_ref[:, sl].astype(F32)
        down = pltpu.roll(zc, 1, 0)
        up = pltpu.roll(zc, ROWS - 1, 0)
        last = SUBLANES - 1
        before = p_ref[HALO - 1:, sl].astype(F32) * keep_p
        after = n_ref[0:1, sl].astype(F32) * keep_n
        prev = jnp.concatenate([jnp.where(row8 == 0, before, down[:SUBLANES]), down[SUBLANES:]], axis=0)
        nxt = jnp.concatenate([up[:-SUBLANES], jnp.where(row8 == last, after, up[-SUBLANES:])], axis=0)
        mp, mn = mp_ref[:, sl], mn_ref[:, sl]
        return zc * (1.0 - mp - mn) + mp * prev + mn * nxt

    shifted = functools.partial(shift, (zm_ref, zp_ref, zn_ref, mup_ref, mun_ref))
    lora = functools.partial(shift, (lm_ref, lp_ref, ln_ref, lmup_ref, lmun_ref))
    ones = _head_ones()
    th_w = jnp.tanh(lora(0))
    ad = lora(LANES)
    sg = jnp.concatenate([_sigmoid(lora(2 * LANES)), _sigmoid(lora(3 * LANES))], axis=1)

    def log_decay(pre):
        return -math.exp(-0.5) * _sigmoid(pre)

    th_b = th_w.astype(BF16)
    ad_b = ad.astype(BF16)
    sg_b = sg.astype(BF16)

    def lora_fine(xb, w):
        hi, lo = _split(w, 2)
        both = _dot(xb, jnp.concatenate([hi, lo], axis=1))
        return both[:, :LANES] + both[:, LANES:]

    for s in range(D_RWKV // LANES):
        sl = slice(s * LANES, (s + 1) * LANES)
        r = shifted(Z_R + s * LANES)
        k = shifted(Z_K + s * LANES)
        v = shifted(Z_V + s * LANES)
        r_o[:, sl] = r.astype(r_o.dtype)
        v_o[:, sl] = v.astype(v_o.dtype)
        kk = k * kk_ref[:, sl]
        ss = _mm(kk * kk, ones)
        kk = kk * lax.rsqrt(jnp.maximum(ss, KK_EPS))
        nkk_o[:, sl] = (-kk).astype(nkk_o.dtype)
        lwf_o[:, sl] = log_decay(w0_ref[0:1, sl] + lora_fine(th_b, wupf_ref[:, sl]))
        lwb_o[:, sl] = log_decay(w0_ref[1:2, sl] + lora_fine(th_b, wupb_ref[:, sl]))
        a_pre = _dot(ad_b, jnp.concatenate([aupf_ref[:, sl], aupb_ref[:, sl]], axis=1).astype(BF16))
        af = _sigmoid(a0_ref[0:1, sl] + a_pre[:, :LANES])
        ab = _sigmoid(a0_ref[1:2, sl] + a_pre[:, LANES:])
        ka = ka_ref[:, sl]
        kdf_o[:, sl] = (k * (1.0 + (af - 1.0) * ka)).astype(kdf_o.dtype)
        kdb_o[:, sl] = (k * (1.0 + (ab - 1.0) * ka)).astype(kdb_o.dtype)
        bf_o[:, sl] = (kk * af).astype(bf_o.dtype)
        bb_o[:, sl] = (kk * ab).astype(bb_o.dtype)
        gate_o[:, sl] = _dot(sg_b, gup_ref[:, sl].astype(BF16)).astype(gate_o.dtype)


def _rwkv_prep(z, mup, mun, lmup, lmun, w0, wupf, wupb, a0, aupf, aupb, k_k, k_a, gup, nblk):
    n = z.shape[0]
    nsub = ROWS // HALO
    full = lambda a: pl.BlockSpec(a.shape, lambda i: (0,) * a.ndim)
    outs = [jax.ShapeDtypeStruct((n, D_RWKV), F32 if name in ("lwf", "lwb") else BF16)
            for name in ("r", "v", "lwf", "lwb", "kdf", "kdb", "nkk", "bf", "bb", "gate")]
    ospec = pl.BlockSpec((ROWS, D_RWKV), lambda i: (i, 0))
    return pl.pallas_call(
        functools.partial(_prep_kernel, nblk=nblk),
        grid=(n // ROWS,),
        in_specs=[
            pl.BlockSpec((ROWS, Z_RKV_W), lambda i: (i, 0)),
            pl.BlockSpec((HALO, Z_RKV_W), lambda i: (jnp.maximum(i * nsub - 1, 0), 0)),
            pl.BlockSpec((HALO, Z_RKV_W), lambda i: (jnp.minimum((i + 1) * nsub, n // HALO - 1), 0)),
            pl.BlockSpec((ROWS, Z_LORA_W), lambda i: (i, Z_LORA // Z_LORA_W)),
            pl.BlockSpec((HALO, Z_LORA_W), lambda i: (jnp.maximum(i * nsub - 1, 0), Z_LORA // Z_LORA_W)),
            pl.BlockSpec((HALO, Z_LORA_W), lambda i: (jnp.minimum((i + 1) * nsub, n // HALO - 1), Z_LORA // Z_LORA_W)),
            full(mup), full(mun), full(lmup), full(lmun),
            full(w0), full(wupf), full(wupb), full(a0), full(aupf), full(aupb),
            full(k_k), full(k_a), full(gup),
        ],
        out_specs=[ospec] * 10,
        out_shape=outs,
        compiler_params=_cparams(("parallel",)),
        name="rwkv_prep",
    )(z, z, z, z, z, z, mup, mun, lmup, lmun, w0, wupf, wupb, a0, aupf, aupb, k_k, k_a, gup)


def _wkv_intra(probs, bd, eye_bd, head_a, head_a2):
    n = CHUNK
    zero = jnp.zeros((n, LANES), F32)
    each = lambda f, *ls: [f(*xs) for xs in zip(*ls)]

    def stack(x):
        return jnp.concatenate([jnp.where(head_a, x, 0.0), jnp.where(head_a, 0.0, x)], axis=0)

    def cumulative(p):
        inclb = p[6].astype(BF16)
        l0, l1, l2 = _split(p[1], 3)
        return _dot(inclb, l0) + (_dot(inclb, l1) + _dot(inclb, l2))

    cs = each(cumulative, probs)

    def factors(p, c):
        r, lw, k, v, a, b = p[:6]
        ctot = jnp.sum(lw, axis=0, keepdims=True)
        mid = 0.5 * ctot
        e_in = jnp.exp(c - mid)
        e_out = jnp.exp(mid - c)
        e_abs = jnp.exp(c)
        e_rest = jnp.exp(ctot - c)
        undo = jnp.exp(-lw)
        lhs = jnp.concatenate([stack(a * e_in * undo), stack(r * e_in)], axis=0)
        rhs = jnp.concatenate([b * e_out, k * e_out], axis=0)
        rest = jnp.concatenate([b * e_rest, k * e_rest], axis=0)
        return lhs, rhs, rest, a * e_abs * undo, r * e_abs, jnp.exp(ctot)

    fs = each(factors, probs, cs)
    grams = [_mm(f[0], f[1], nt=True) for f in fs]
    aas = [jnp.where(p[7], g[:2 * n], 0.0) for p, g in zip(probs, grams)]
    ars = [jnp.where(p[8], g[2 * n:], 0.0) for p, g in zip(probs, grams)]
    a_abs = [jnp.where(bd, jnp.concatenate([aa[:n], pltpu.roll(aa[n:], n, 1)], axis=0), 0.0) for aa in aas]

    invs = [eye_bd + x for x in a_abs]
    pws = [_mm(x, x) for x in a_abs]
    for _ in range(int(np.log2(n)) - 2):
        boths = [_mm(jnp.concatenate([pw, inv], axis=0), pw) for pw, inv in zip(pws, invs)]
        pws = [x[:2 * n] for x in boths]
        invs = [inv + x[2 * n:] for inv, x in zip(invs, boths)]
    invs = [inv + _mm(inv, pw) for inv, pw in zip(invs, pws)]

    akvs = [jnp.where(bd, _mm(aa, jnp.concatenate([zero, p[3]], axis=0)), 0.0)
            for aa, p in zip(aas, probs)]
    wu_sts = [_mm(inv, jnp.concatenate([stack(f[3]), akv], axis=1))
              for inv, f, akv in zip(invs, fs, akvs)]
    m2s = [jnp.concatenate([x[:n] + x[n:], jnp.concatenate([zero, p[3]], axis=1)], axis=0)
           for x, p in zip(wu_sts, probs)]
    qy_sts = [_mm(ar, m2) for ar, m2 in zip(ars, m2s)]
    ghs = [_mm(m2.T, f[2]) for m2, f in zip(m2s, fs)]
    out = []
    for f, qy_st, gh in zip(fs, qy_sts, ghs):
        qy = jnp.where(head_a2, qy_st[:n], qy_st[n:])
        out.append((f[4] + qy[:, :LANES], qy[:, LANES:],
                    jnp.where(bd, gh[:LANES], 0.0) + eye_bd * f[5], jnp.where(bd, gh[LANES:], 0.0)))
    return out


def _wkv_kernel(rf, vf, af, lwf, kf, bf, rb, vb, ab, lwb, kb, bb, yf_o, yb_o, s_ref, q_ref, y0_ref, gm_ref, h_ref):
    j = pl.program_id(1)
    n = CHUNK
    nsub = ROWS // n
    cur = j % 2
    prv = 1 - cur

    @pl.when(j == 0)
    def _():
        s_ref[...] = jnp.zeros_like(s_ref)
        q_ref[1] = jnp.zeros(q_ref.shape[1:], F32)
        y0_ref[1] = jnp.zeros(y0_ref.shape[1:], F32)
        gm_ref[1] = jnp.zeros(gm_ref.shape[1:], F32)
        h_ref[1] = jnp.zeros(h_ref.shape[1:], F32)

    for step in range(nsub):
        for d, y_o in enumerate((yf_o, yb_o)):
            for p in range(WKV_PAIRS):
                u = step if d == 0 else nsub - 1 - step
                idx = (d * WKV_PAIRS + p) * nsub + u
                s0 = s_ref[d * WKV_PAIRS + p]
                y_o[u * n:(u + 1) * n, p * LANES:(p + 1) * LANES] = (
                    _mm(q_ref[prv, idx], s0, nt=True) + y0_ref[prv, idx]).astype(y_o.dtype)
                s_ref[d * WKV_PAIRS + p] = _mm(s0, gm_ref[prv, idx]) + h_ref[prv, idx]

    ti = lax.broadcasted_iota(jnp.int32, (n, n), 0)
    si = lax.broadcasted_iota(jnp.int32, (n, n), 1)
    t2 = lax.broadcasted_iota(jnp.int32, (2 * n, 2 * n), 0)
    s2 = lax.broadcasted_iota(jnp.int32, (2 * n, 2 * n), 1)
    tw, sw = t2 % n, s2 % n
    bd = (t2 // n) == (s2 // n)
    eye_bd = jnp.where(t2 == s2, 1.0, 0.0).astype(F32)
    head_a = lax.broadcasted_iota(jnp.int32, (n, LANES), 1) < HEAD_DIM
    head_a2 = lax.broadcasted_iota(jnp.int32, (n, 2 * LANES), 1) % LANES < HEAD_DIM
    masks = (
        (jnp.where(si <= ti, 1.0, 0.0).astype(F32), sw < tw, sw <= tw),
        (jnp.where(si >= ti, 1.0, 0.0).astype(F32), sw > tw, sw >= tw),
    )
    srcs = ((rf, lwf, kf, vf, af, bf), (rb, lwb, kb, vb, ab, bb))

    probs = [tuple(x[u * n:(u + 1) * n, p * LANES:(p + 1) * LANES].astype(F32) for x in srcs[d]) + masks[d]
             for d in range(2) for p in range(WKV_PAIRS) for u in range(nsub)]
    for idx, (q, y0, gm, h) in enumerate(_wkv_intra(probs, bd, eye_bd, head_a, head_a2)):
        q_ref[cur, idx] = q
        y0_ref[cur, idx] = y0
        gm_ref[cur, idx] = gm
        h_ref[cur, idx] = h


def _wkv(r, v, nkk, lwf, kdf, bfw, lwb, kdb, bbw, nbatch, nblk):
    n = r.shape[0]
    ngrp = D_RWKV // (WKV_PAIRS * LANES)
    nprob = 2 * WKV_PAIRS * (ROWS // CHUNK)
    last = nblk - 1
    width = WKV_PAIRS * LANES

    def bwd_blk(j):
        return jnp.where(j == 0, 0, nblk - j)

    def spec(blk):
        return pl.BlockSpec((ROWS, width), lambda g, j: ((g // ngrp) * nblk + blk(j), g % ngrp))

    fin = spec(lambda j: jnp.minimum(j, last))
    bin_ = spec(lambda j: bwd_blk(jnp.minimum(j, last)))
    fout = spec(lambda j: jnp.maximum(j - 1, 0))
    bout = spec(lambda j: bwd_blk(jnp.maximum(j - 1, 0)))
    out = jax.ShapeDtypeStruct((n, D_RWKV), BF16)
    return pl.pallas_call(
        _wkv_kernel,
        grid=(nbatch * ngrp, nblk + 1),
        in_specs=[fin] * 6 + [bin_] * 6,
        out_specs=[fout, bout],
        out_shape=[out, out],
        scratch_shapes=[pltpu.VMEM((2 * WKV_PAIRS, LANES, LANES), F32),
                        pltpu.VMEM((2, nprob, CHUNK, LANES), F32), pltpu.VMEM((2, nprob, CHUNK, LANES), F32),
                        pltpu.VMEM((2, nprob, LANES, LANES), F32), pltpu.VMEM((2, nprob, LANES, LANES), F32)],
        compiler_params=_cparams(("parallel", "arbitrary")),
        name="wkv",
    )(r, v, nkk, lwf, kdf, bfw, r, v, nkk, lwb, kdb, bbw)


def _rwkv_mix(yf_ref, yb_ref, r_ref, v_ref, kdf_ref, kdb_ref, gate_ref, rk_ref, lg_ref, lb_ref):
    ones = _head_ones().astype(BF16)
    mean_mat = (_head_ones() * (1.0 / HEAD_DIM)).astype(BF16)
    slabs = [slice(s * LANES, (s + 1) * LANES) for s in range(D_RWKV // LANES)]
    n = yf_ref.shape[0]

    def head_mean(y):
        hi, lo = _split(y, 2)
        both = _dot(jnp.concatenate([hi, lo], axis=0), mean_mat)
        return both[:n] + both[n:]

    ys = [yf_ref[:, sl].astype(F32) + yb_ref[:, sl].astype(F32) for sl in slabs]
    ycs = [y - head_mean(y) for y in ys]
    variances = [_dot((yc * yc).astype(BF16), mean_mat) for yc in ycs]
    sums = [_dot((r_ref[:, sl].astype(F32) * (kdf_ref[:, sl].astype(F32) + kdb_ref[:, sl].astype(F32))
                  * rk_ref[:, sl]).astype(BF16), ones) for sl in slabs]
    out = []
    for sl, yc, var, sm in zip(slabs, ycs, variances, sums):
        yn = yc * lax.rsqrt(var + LNX_EPS) * lg_ref[:, sl] + lb_ref[:, sl]
        bonus = sm * v_ref[:, sl].astype(F32)
        out.append(((yn + bonus) * gate_ref[:, sl].astype(F32)).astype(BF16))
    return jnp.concatenate(out, axis=1)


def _outproj_kernel(x_ref, attn_ref, yf_ref, yb_ref, r_ref, v_ref, kdf_ref, kdb_ref, gate_ref, rk_ref, lg_ref, lb_ref,
                    g1_ref, g_ref, sh_ref, sc_ref, wa_ref, wr_ref, o_ref, h_ref):
    rw = _rwkv_mix(yf_ref, yb_ref, r_ref, v_ref, kdf_ref, kdb_ref, gate_ref, rk_ref, lg_ref, lb_ref)
    mix = _dot(attn_ref[...], wa_ref[...]) + _dot(rw, wr_ref[...])
    x1 = x_ref[...] + g1_ref[0] * mix
    o_ref[...] = x1
    ms = jnp.mean(x1 * x1, axis=-1, keepdims=True)
    h = x1 * lax.rsqrt(ms + NORM_EPS) * g_ref[...]
    h_ref[...] = (h * (1.0 + sc_ref[0]) + sh_ref[0]).astype(h_ref.dtype)


def _outproj(x2, attn, yf, yb, r, v, kdf, kdb, gate, r_k, lnx_g, lnx_b, g1, g, sh, sc, wa, wr, t, nblk):
    n = x2.shape[0]
    lpb = t // ROWS
    modspec = pl.BlockSpec((1, 1, D_MODEL), lambda i: (i // lpb, 0, 0))
    rowspec = pl.BlockSpec((ROWS, D_MODEL), lambda i: (i, 0))
    seqspec = pl.BlockSpec((ROWS, D_RWKV), lambda i: ((i // lpb) * nblk + 1 + i % lpb, 0))
    pspec = pl.BlockSpec((1, D_RWKV), lambda i: (0, 0))
    return pl.pallas_call(
        _outproj_kernel,
        grid=(n // ROWS,),
        in_specs=[
            rowspec,
            pl.BlockSpec((ROWS, D_ATTN), lambda i: (i, 0)),
        ] + [seqspec] * 7 + [pspec] * 3 + [
            modspec,
            pl.BlockSpec((1, D_MODEL), lambda i: (0, 0)),
            modspec, modspec,
            pl.BlockSpec((D_ATTN, D_MODEL), lambda i: (0, 0), pipeline_mode=pl.Buffered(1)),
            pl.BlockSpec((D_RWKV, D_MODEL), lambda i: (0, 0), pipeline_mode=pl.Buffered(1)),
        ],
        out_specs=[rowspec, rowspec],
        out_shape=[jax.ShapeDtypeStruct((n, D_MODEL), F32), jax.ShapeDtypeStruct((n, D_MODEL), BF16)],
        compiler_params=_cparams(("parallel",)),
        name="outproj",
    )(x2, attn, yf, yb, r, v, kdf, kdb, gate, r_k, lnx_g, lnx_b, g1, g, sh, sc, wa, wr)


def _ffn_kernel(x_ref, h_ref, g2_ref, fg_ref, w1_ref, w3_ref, w2_ref, o_ref, acc_ref):
    j = pl.program_id(1)

    @pl.when(j == 0)
    def _():
        acc_ref[...] = jnp.zeros_like(acc_ref)

    h = h_ref[...]
    a = _dot(h, w1_ref[...])
    b = _dot(h, w3_ref[...])
    u = a * _sigmoid(a) * b
    acc_ref[...] += _dot(u.astype(BF16), w2_ref[...])

    @pl.when(j == pl.num_programs(1) - 1)
    def _():
        y = x_ref[...] + g2_ref[0] * acc_ref[...]
        ms = jnp.mean(y * y, axis=-1, keepdims=True)
        o_ref[...] = y * lax.rsqrt(ms + NORM_EPS) * fg_ref[...]


def _ffn(x1, h2, g2, fg, w1, w3, w2, t):
    n = x1.shape[0]
    tm, tf = FFN_TM, FFN_TF
    modspec = pl.BlockSpec((1, 1, D_MODEL), lambda i, j: (i // (t // tm), 0, 0))
    vecspec = pl.BlockSpec((1, D_MODEL), lambda i, j: (0, 0))
    return pl.pallas_call(
        _ffn_kernel,
        grid=(n // tm, D_FF // tf),
        in_specs=[
            pl.BlockSpec((tm, D_MODEL), lambda i, j: (i, 0)),
            pl.BlockSpec((tm, D_MODEL), lambda i, j: (i, 0)),
            modspec, vecspec,
            pl.BlockSpec((D_MODEL, tf), lambda i, j: (0, j)),
            pl.BlockSpec((D_MODEL, tf), lambda i, j: (0, j)),
            pl.BlockSpec((tf, D_MODEL), lambda i, j: (j, 0)),
        ],
        out_specs=pl.BlockSpec((tm, D_MODEL), lambda i, j: (i, 0)),
        out_shape=jax.ShapeDtypeStruct((n, D_MODEL), F32),
        scratch_shapes=[pltpu.VMEM((tm, D_MODEL), F32)],
        compiler_params=_cparams(("parallel", "arbitrary")),
        name="ffn",
    )(x1, h2, g2, fg, w1, w3, w2)


def _z_weights_kernel(w_ref, o_ref):
    q0, k0, v0, r0 = 0, D_ATTN, D_ATTN + Z_KV_W, D_ATTN + 2 * Z_KV_W
    lora0 = r0 + Z_RKV_W
    n_lora = 4 * D_LORA + D_GATE_LORA
    cast = lambda x: x.astype(o_ref.dtype)
    o_ref[:, Z_R:Z_R + Z_RKV_W] = cast(w_ref[:, r0:lora0])
    o_ref[:, Z_Q:Z_Q + D_ATTN] = cast(w_ref[:, q0:q0 + D_ATTN] * (HEAD_DIM ** -0.5 * LOG2E))
    o_ref[:, Z_LORA:Z_LORA + Z_LORA_W] = jnp.zeros((o_ref.shape[0], Z_LORA_W), o_ref.dtype)
    o_ref[:, Z_LORA:Z_LORA + n_lora] = cast(w_ref[:, lora0:lora0 + n_lora])
    o_ref[:, Z_KC:Z_KC + Z_KV_W] = cast(w_ref[:, k0:k0 + Z_KV_W])
    o_ref[:, Z_VC:Z_VC + Z_KV_W] = cast(w_ref[:, v0:v0 + Z_KV_W])


def _z_weights(w_in):
    rows = WPREP_ROWS
    return pl.pallas_call(
        _z_weights_kernel,
        grid=(D_MODEL // rows,),
        in_specs=[pl.BlockSpec((None, rows, w_in.shape[2]), lambda i: (0, i, 0))],
        out_specs=pl.BlockSpec((rows, NZ), lambda i: (i, 0)),
        out_shape=jax.ShapeDtypeStruct((D_MODEL, NZ), BF16),
        compiler_params=_cparams(("parallel",)),
        name="z_weights",
    )(w_in)


def _rope_tables(t):
    pos = np.arange(t)
    row = (pos // GRID_W).astype(np.float32)
    col = (pos % GRID_W).astype(np.float32)
    nf = HEAD_DIM // 4
    freqs = (np.float32(1.0) / np.float32(ROPE_BASE) ** (np.arange(nf, dtype=np.float32) / np.float32(nf))).astype(np.float32)
    d = np.arange(LANES) % HEAD_DIM
    use_row = (d // (HEAD_DIM // 2)) == 0
    first = (d % (HEAD_DIM // 2)) < nf
    ang = (np.where(use_row[None, :], row[:, None], col[:, None]) * freqs[d % nf][None, :]).astype(np.float32)
    sin = np.sin(ang)
    return jnp.asarray(np.cos(ang), F32), jnp.asarray(np.where(first[None, :], -sin, sin), F32)


def _pad_rows(w, top, total):
    return jnp.concatenate([jnp.zeros((top, w.shape[1]), w.dtype), w,
                            jnp.zeros((total - top - w.shape[0], w.shape[1]), w.dtype)], axis=0)


def kernel(x, c, ctx, c_ctx, ada_w, ada_b, norm1_g, norm2_g, w_in, attn_sink, ts_prev, ts_next, w0, w_up, a0, a_up,
           k_k, k_a, r_k, g_up, lnx_g, lnx_b, w_out, ffn_w1, ffn_w3, ffn_w2, final_norm_g):
    nbatch, t, d = x.shape
    nctx = ctx.shape[1]
    assert d == D_MODEL and nctx == ROWS and t % FFN_TM == 0 and nbatch < SUBLANES and ada_w.shape[0] == 1
    assert BLOCK == WINDOW and ATT_HEADS == N_KV_HEADS
    nblk = (nctx + t) // ROWS

    cc = jnp.concatenate([c, c_ctx[None, :], jnp.zeros((SUBLANES - nbatch - 1, d), F32)], axis=0)
    mod = _adaln(cc, ada_w[0], ada_b[0])
    mods = mod[:nbatch + 1].reshape(nbatch + 1, 6, 1, d)
    sh1, sc1, g1, sh2, sc2, g2 = (mods[:, i] for i in range(6))

    x2 = x.reshape(nbatch * t, d)
    cos_t, sin_t = _rope_tables(t)
    z = _inproj(x2, ctx.reshape(nbatch * nctx, d), norm1_g, sh1, sc1, cos_t, sin_t, _z_weights(w_in), nbatch, nblk)

    attn = _attention(z, attn_sink[0], nbatch, t, nctx)

    n_lora = ts_prev.shape[1] - Z_RKV_W
    pad_lora = lambda m: jnp.pad(m[:, Z_RKV_W:], ((0, 0), (0, Z_LORA_W - n_lora)))
    wupf = _pad_rows(w_up[0, 0], 0, LANES)
    wupb = _pad_rows(w_up[0, 1], D_LORA, LANES)
    aupf = _pad_rows(a_up[0, 0], 0, LANES)
    aupb = _pad_rows(a_up[0, 1], D_LORA, LANES)
    gup = _pad_rows(g_up[0], 0, 2 * LANES)
    r, v, lwf, lwb, kdf, kdb, nkk, bfw, bbw, gate = _rwkv_prep(
        z, ts_prev[:, :Z_RKV_W], ts_next[:, :Z_RKV_W], pad_lora(ts_prev), pad_lora(ts_next),
        w0[0], wupf, wupb, a0[0], aupf, aupb, k_k, k_a, gup, nblk)
    yf, yb = _wkv(r, v, nkk, lwf, kdf, bfw, lwb, kdb, bbw, nbatch, nblk)

    wo = w_out[0].astype(BF16)
    x1, h2 = _outproj(x2, attn, yf, yb, r, v, kdf, kdb, gate, r_k, lnx_g, lnx_b,
                      g1, norm2_g, sh2, sc2, wo[:D_ATTN], wo[D_ATTN:], t, nblk)
    out = _ffn(x1, h2, g2, final_norm_g.reshape(1, d),
               ffn_w1[0].astype(BF16), ffn_w3[0].astype(BF16), ffn_w2[0].astype(BF16), t)
    return out.reshape(nbatch, t, d)
```

```python
import functools
import math

import numpy as np
import jax
import jax.numpy as jnp
from jax import lax
from jax.experimental import pallas as pl
from jax.experimental.pallas import tpu as pltpu

F32 = jnp.float32
BF16 = jnp.bfloat16

D_MODEL = 2048
HEAD_DIM = 64
N_KV_HEADS = 4
GQA_GROUP = 4
D_ATTN = 1024
D_RWKV = 1024
D_LORA = 64
D_GATE_LORA = 160
D_FF = 5632
GRID_W = 64
WINDOW = 128
BLOCK = 128
ROPE_BASE = 10000.0
NORM_EPS = 1e-6
LNX_EPS = 64e-5
MASK_VALUE = -1e30
KK_EPS = 1e-12
ROT_HALF = HEAD_DIM // 2
ROT_PAIR = HEAD_DIM // 4
LOG2E = math.log2(math.e)

LANES = 128
SUBLANES = 8
V7X_VMEM_BYTES = 64 * 2 ** 20
VMEM_LIMIT = V7X_VMEM_BYTES - 8 * 2 ** 20

ROWS = 256
CHUNK = 64
WKV_PAIRS = 4
ATT_HEADS = 4
ADALN_TN = 1024
PROJ_TN = 512
FFN_TM, FFN_TF = 512, 512
WPREP_ROWS = 256

Z_R, Z_K, Z_V = 0, 1024, 2048
Z_RKV_W = 3072
Z_Q = 3072
Z_LORA = 4096
Z_LORA_W = 512
Z_KV_W = N_KV_HEADS * HEAD_DIM
Z_KC, Z_VC = 4608, 4864
NZ = 5120
ROPE_SLABS = tuple(range(Z_Q, Z_Q + D_ATTN, LANES)) + tuple(range(Z_KC, Z_VC, LANES))


def _cparams(sem):
    return pltpu.CompilerParams(dimension_semantics=sem, vmem_limit_bytes=VMEM_LIMIT)


def _sigmoid(x):
    return 0.5 * jnp.tanh(0.5 * x) + 0.5


def _dot(a, b):
    return jnp.dot(a, b, preferred_element_type=F32)


def _dot_nt(a, b):
    return lax.dot_general(a, b, (((1,), (1,)), ((), ())), preferred_element_type=F32)


def _split(x, terms):
    out = []
    for _ in range(terms - 1):
        hi = x.astype(BF16)
        out.append(hi)
        x = x - hi.astype(F32)
    out.append(x.astype(BF16))
    return out


def _mm(a, b, nt=False):
    return (_dot_nt if nt else _dot)(a.astype(BF16), b.astype(BF16))


def _mm_fine(a, b):
    ah, al = _split(a, 2)
    bh, bl = _split(b, 2)
    return _dot(ah, bh) + (_dot(ah, bl) + _dot(al, bh))


def _adaln_kernel(c_ref, w_ref, b_ref, o_ref):
    c = c_ref[...]
    s = c * _sigmoid(c)
    o_ref[...] = _mm_fine(s, w_ref[...]) + b_ref[...]


def _adaln(cc, w, b):
    n = w.shape[1]
    tn = ADALN_TN
    return pl.pallas_call(
        _adaln_kernel,
        grid=(n // tn,),
        in_specs=[
            pl.BlockSpec((SUBLANES, D_MODEL), lambda j: (0, 0)),
            pl.BlockSpec((D_MODEL, tn), lambda j: (0, j)),
            pl.BlockSpec((1, tn), lambda j: (0, j)),
        ],
        out_specs=pl.BlockSpec((SUBLANES, tn), lambda j: (0, j)),
        out_shape=jax.ShapeDtypeStruct((SUBLANES, n), F32),
        compiler_params=_cparams(("parallel",)),
        name="adaln",
    )(cc, w, b.reshape(1, n))


def _inproj_kernel(x_ref, ctx_ref, g_ref, sh_ref, sc_ref, cos_ref, sin_ref, w_ref, z_ref, shw_ref, *, nblk):
    li = pl.program_id(0) % nblk
    is_ctx = li == 0

    @pl.when(li <= 1)
    def _():
        shb = jnp.broadcast_to(sh_ref[0], (SUBLANES, D_MODEL)).astype(BF16)
        for j in range(NZ // PROJ_TN):
            sl = slice(j * PROJ_TN, (j + 1) * PROJ_TN)
            shw_ref[:, sl] = _dot(shb, w_ref[:, sl])

    x = jnp.where(is_ctx, ctx_ref[...], x_ref[...])
    xb = (x * (g_ref[...] * (1.0 + sc_ref[0]))).astype(BF16)
    rs = lax.rsqrt(jnp.mean(x * x, axis=-1, keepdims=True) + NORM_EPS)
    cs = cos_ref[...]
    sn = sin_ref[...]
    first = (lax.broadcasted_iota(jnp.int32, (ROWS, LANES), 1) % ROT_HALF) < ROT_PAIR
    for j in range(NZ // PROJ_TN):
        sl = slice(j * PROJ_TN, (j + 1) * PROJ_TN)
        zt = _dot(xb, w_ref[:, sl]) * rs + shw_ref[0:1, sl]
        for c0 in range(j * PROJ_TN, (j + 1) * PROJ_TN, LANES):
            zz = zt[:, c0 - j * PROJ_TN:c0 - j * PROJ_TN + LANES]
            if c0 in ROPE_SLABS:
                partner = jnp.where(first, pltpu.roll(zz, LANES - ROT_PAIR, 1), pltpu.roll(zz, ROT_PAIR, 1))
                zz = jnp.where(is_ctx, zz, zz * cs + partner * sn)
            z_ref[:, c0:c0 + LANES] = zz


def _inproj(x2, ctx2, g, sh, sc, cos_t, sin_t, wz, nbatch, nblk):
    n = nbatch * nblk * ROWS
    lpb = nblk - 1

    def sel(i):
        return jnp.where(i % nblk == 0, nbatch, i // nblk)

    return pl.pallas_call(
        functools.partial(_inproj_kernel, nblk=nblk),
        grid=(n // ROWS,),
        in_specs=[
            pl.BlockSpec((ROWS, D_MODEL), lambda i: ((i // nblk) * lpb + jnp.maximum(i % nblk - 1, 0), 0)),
            pl.BlockSpec((ROWS, D_MODEL), lambda i: (i // nblk, 0)),
            pl.BlockSpec((1, D_MODEL), lambda i: (0, 0)),
            pl.BlockSpec((1, 1, D_MODEL), lambda i: (sel(i), 0, 0)),
            pl.BlockSpec((1, 1, D_MODEL), lambda i: (sel(i), 0, 0)),
            pl.BlockSpec((ROWS, LANES), lambda i: (jnp.maximum(i % nblk - 1, 0), 0)),
            pl.BlockSpec((ROWS, LANES), lambda i: (jnp.maximum(i % nblk - 1, 0), 0)),
            pl.BlockSpec((D_MODEL, NZ), lambda i: (0, 0), pipeline_mode=pl.Buffered(1)),
        ],
        out_specs=pl.BlockSpec((ROWS, NZ), lambda i: (i, 0)),
        out_shape=jax.ShapeDtypeStruct((n, NZ), F32),
        scratch_shapes=[pltpu.VMEM((SUBLANES, NZ), F32)],
        compiler_params=_cparams(("arbitrary",)),
        name="inproj",
    )(x2, ctx2, g, sh, sc, cos_t, sin_t, wz)


def _attn_kernel(sink_ref, q_ref, kp_ref, kc_ref, kn_ref, vp_ref, vc_ref, vn_ref, kx_ref, vx_ref, o_ref, *, nb):
    i = pl.program_id(1)
    hg = pl.program_id(2)
    lane = lax.broadcasted_iota(jnp.int32, (BLOCK, LANES), 1)
    lo = lane < HEAD_DIM
    heads = range(ATT_HEADS)

    qoff = lax.broadcasted_iota(jnp.int32, (GQA_GROUP * BLOCK, BLOCK), 0) % BLOCK
    koff = lax.broadcasted_iota(jnp.int32, (GQA_GROUP * BLOCK, BLOCK), 1)
    see_prev = jnp.where(i > 0, jnp.where(koff >= qoff, 1, 0), 0) > 0
    see_next = jnp.where(i < nb - 1, jnp.where(koff <= qoff, 1, 0), 0) > 0

    def masked(s):
        return jnp.concatenate([jnp.where(see_prev, s[:, :BLOCK], MASK_VALUE), s[:, BLOCK:2 * BLOCK],
                                jnp.where(see_next, s[:, 2 * BLOCK:3 * BLOCK], MASK_VALUE), s[:, 3 * BLOCK:]], axis=1)
    rowc = lax.broadcasted_iota(jnp.int32, (GQA_GROUP * BLOCK, 1), 0)

    order = (0, 2, 1, 3)

    def queries(h):
        parts = []
        for g in order:
            c0 = (h * GQA_GROUP + g) // 2 * LANES
            qp = q_ref[:, c0:c0 + LANES]
            parts.append(jnp.where(lo, qp, 0.0) if g % 2 == 0 else jnp.where(lo, 0.0, qp))
        return jnp.concatenate(parts, axis=0).astype(BF16)

    rolled = {}

    def slab_of(r, h):
        key = (id(r), h // 2)
        if key not in rolled:
            x = r[:, (h // 2) * LANES:(h // 2 + 1) * LANES]
            rolled[key] = (x, pltpu.roll(x, HEAD_DIM, 1))
        x, xr = rolled[key]
        return (x, xr) if h % 2 == 0 else (xr, x)

    def keys_of(h):
        parts = []
        for r in (kp_ref, kc_ref, kn_ref, kx_ref):
            on_lo, on_hi = slab_of(r, h)
            parts.append(jnp.where(lax.broadcasted_iota(jnp.int32, on_lo.shape, 1) < HEAD_DIM, on_lo, on_hi))
        return jnp.concatenate(parts, axis=0).astype(BF16)

    def values_of(h):
        va, vb = [], []
        for r in (vp_ref, vc_ref, vn_ref, vx_ref):
            on_lo, on_hi = slab_of(r, h)
            lo_r = lax.broadcasted_iota(jnp.int32, on_lo.shape, 1) < HEAD_DIM
            va.append(jnp.where(lo_r, on_lo, 1.0))
            vb.append(jnp.where(lo_r, 1.0, on_hi))
        return jnp.concatenate(va, axis=0).astype(BF16), jnp.concatenate(vb, axis=0).astype(BF16)

    def sinks(h):
        base = (hg * ATT_HEADS + h) * GQA_GROUP
        return LOG2E * jnp.where(rowc < BLOCK, sink_ref[base + order[0]],
                                 jnp.where(rowc < 2 * BLOCK, sink_ref[base + order[1]],
                                           jnp.where(rowc < 3 * BLOCK, sink_ref[base + order[2]],
                                                     sink_ref[base + order[3]])))

    half = 2 * BLOCK
    ss = [masked(_dot_nt(queries(h), keys_of(h))) for h in heads]
    sks = [sinks(h) for h in heads]
    ms = [jnp.maximum(jnp.max(s, axis=-1, keepdims=True), sk) for s, sk in zip(ss, sks)]
    ps = [jnp.exp2(s - m).astype(BF16) for s, m in zip(ss, ms)]
    extra = [jnp.exp2(sk - m) for sk, m in zip(sks, ms)]
    vals = [values_of(h) for h in heads]
    oa = [_dot(p[:half], v[0]) for p, v in zip(ps, vals)]
    ob = [_dot(p[half:], v[1]) for p, v in zip(ps, vals)]
    for h, a, b, e in zip(heads, oa, ob, extra):
        a = a / (pltpu.roll(a, HEAD_DIM, 1) + e[:half])
        b = b / (pltpu.roll(b, HEAD_DIM, 1) + e[half:])
        o_ref[:, 2 * h * LANES:(2 * h + 1) * LANES] = jnp.where(lo, a[:BLOCK], b[:BLOCK]).astype(o_ref.dtype)
        o_ref[:, (2 * h + 1) * LANES:(2 * h + 2) * LANES] = jnp.where(lo, a[BLOCK:], b[BLOCK:]).astype(o_ref.dtype)


def _attention(z, sink, nbatch, t, nctx):
    nb = t // BLOCK
    rpb = (nctx + t) // BLOCK
    cpb = nctx // BLOCK

    def lat(off):
        def f(b, i, h):
            return b * rpb + cpb + jnp.clip(i + off, 0, nb - 1)
        return f

    kw = ATT_HEADS * HEAD_DIM
    qw = ATT_HEADS * GQA_GROUP * HEAD_DIM

    def kspec(off, col0):
        return pl.BlockSpec((BLOCK, kw), lambda b, i, h: (lat(off)(b, i, h), col0 // kw + h))

    def xspec(col0):
        return pl.BlockSpec((nctx, kw), lambda b, i, h: (b * (rpb * BLOCK // nctx), col0 // kw + h))

    kcol, vcol = Z_KC, Z_VC
    return pl.pallas_call(
        functools.partial(_attn_kernel, nb=nb),
        grid=(nbatch, nb, N_KV_HEADS // ATT_HEADS),
        in_specs=[
            pl.BlockSpec(memory_space=pltpu.SMEM),
            pl.BlockSpec((BLOCK, qw), lambda b, i, h: (lat(0)(b, i, h), Z_Q // qw + h)),
            kspec(-1, kcol), kspec(0, kcol), kspec(1, kcol),
            kspec(-1, vcol), kspec(0, vcol), kspec(1, vcol),
            xspec(kcol), xspec(vcol),
        ],
        out_specs=pl.BlockSpec((BLOCK, qw), lambda b, i, h: (b * nb + i, h)),
        out_shape=jax.ShapeDtypeStruct((nbatch * t, D_ATTN), BF16),
        compiler_params=_cparams(("parallel", "parallel", "parallel")),
        name="attention",
    )(sink, z, z, z, z, z, z, z, z, z)


def _head_ones():
    r = lax.broadcasted_iota(jnp.int32, (LANES, LANES), 0)
    c = lax.broadcasted_iota(jnp.int32, (LANES, LANES), 1)
    return jnp.where((r // HEAD_DIM) == (c // HEAD_DIM), 1.0, 0.0).astype(F32)


def _prep_kernel(zm_ref, zp_ref, zn_ref, lm_ref, lp_ref, ln_ref, mup_ref, mun_ref, lmup_ref, lmun_ref,
                 w0_ref, wupf_ref, wupb_ref, a0_ref, aupf_ref, aupb_ref, kk_ref, ka_ref, gup_ref,
                 r_o, v_o, lwf_o, lwb_o, kdf_o, kdb_o, nkk_o, bf_o, bb_o, gate_o, *, nblk):
    li = pl.program_id(0) % nblk
    keep_p = jnp.where((li == 0) | (li == 1), 0.0, 1.0)
    keep_n = jnp.where((li == 0) | (li == nblk - 1), 0.0, 1.0)
    row8 = lax.broadcasted_iota(jnp.int32, (SUBLANES, LANES), 0)

    def shift(refs, c0):
        m_ref, p_ref, n_ref, mp_ref, mn_ref = refs
        sl = slice(c0, c0 + LANES)
        zc = m_ref[:, sl]
        down = pltpu.roll(zc, 1, 0)
        up = pltpu.roll(zc, ROWS - 1, 0)
        last = SUBLANES - 1
        prev = jnp.concatenate([jnp.where(row8 == 0, p_ref[last:, sl] * keep_p, down[:SUBLANES]), down[SUBLANES:]],
                               axis=0)
        nxt = jnp.concatenate([up[:-SUBLANES], jnp.where(row8 == last, n_ref[0:1, sl] * keep_n, up[-SUBLANES:])],
                              axis=0)
        mp, mn = mp_ref[:, sl], mn_ref[:, sl]
        return zc * (1.0 - mp - mn) + mp * prev + mn * nxt

    shifted = functools.partial(shift, (zm_ref, zp_ref, zn_ref, mup_ref, mun_ref))
    lora = functools.partial(shift, (lm_ref, lp_ref, ln_ref, lmup_ref, lmun_ref))
    ones = _head_ones()
    th_w = jnp.tanh(lora(0))
    ad = lora(LANES)
    sg = jnp.concatenate([_sigmoid(lora(2 * LANES)), _sigmoid(lora(3 * LANES))], axis=1)

    def log_decay(pre):
        return -math.exp(-0.5) * _sigmoid(pre)

    th_b = th_w.astype(BF16)
    ad_b = ad.astype(BF16)
    sg_b = sg.astype(BF16)

    def lora_fine(xb, w):
        hi, lo = _split(w, 2)
        both = _dot(xb, jnp.concatenate([hi, lo], axis=1))
        return both[:, :LANES] + both[:, LANES:]

    for s in range(D_RWKV // LANES):
        sl = slice(s * LANES, (s + 1) * LANES)
        r = shifted(Z_R + s * LANES)
        k = shifted(Z_K + s * LANES)
        v = shifted(Z_V + s * LANES)
        r_o[:, sl] = r.astype(r_o.dtype)
        v_o[:, sl] = v.astype(v_o.dtype)
        kk = k * kk_ref[:, sl]
        ss = _mm(kk * kk, ones)
        kk = kk * lax.rsqrt(jnp.maximum(ss, KK_EPS))
        nkk_o[:, sl] = (-kk).astype(nkk_o.dtype)
        lwf_o[:, sl] = log_decay(w0_ref[0:1, sl] + lora_fine(th_b, wupf_ref[:, sl]))
        lwb_o[:, sl] = log_decay(w0_ref[1:2, sl] + lora_fine(th_b, wupb_ref[:, sl]))
        a_pre = _dot(ad_b, jnp.concatenate([aupf_ref[:, sl], aupb_ref[:, sl]], axis=1).astype(BF16))
        af = _sigmoid(a0_ref[0:1, sl] + a_pre[:, :LANES])
        ab = _sigmoid(a0_ref[1:2, sl] + a_pre[:, LANES:])
        ka = ka_ref[:, sl]
        kdf_o[:, sl] = (k * (1.0 + (af - 1.0) * ka)).astype(kdf_o.dtype)
        kdb_o[:, sl] = (k * (1.0 + (ab - 1.0) * ka)).astype(kdb_o.dtype)
        bf_o[:, sl] = (kk * af).astype(bf_o.dtype)
        bb_o[:, sl] = (kk * ab).astype(bb_o.dtype)
        gate_o[:, sl] = _dot(sg_b, gup_ref[:, sl].astype(BF16)).astype(gate_o.dtype)


def _rwkv_prep(z, mup, mun, lmup, lmun, w0, wupf, wupb, a0, aupf, aupb, k_k, k_a, gup, nblk):
    n = z.shape[0]
    nsub = ROWS // SUBLANES
    full = lambda a: pl.BlockSpec(a.shape, lambda i: (0,) * a.ndim)
    outs = [jax.ShapeDtypeStruct((n, D_RWKV), F32 if name in ("lwf", "lwb") else BF16)
            for name in ("r", "v", "lwf", "lwb", "kdf", "kdb", "nkk", "bf", "bb", "gate")]
    ospec = pl.BlockSpec((ROWS, D_RWKV), lambda i: (i, 0))
    return pl.pallas_call(
        functools.partial(_prep_kernel, nblk=nblk),
        grid=(n // ROWS,),
        in_specs=[
            pl.BlockSpec((ROWS, Z_RKV_W), lambda i: (i, 0)),
            pl.BlockSpec((SUBLANES, Z_RKV_W), lambda i: (jnp.maximum(i * nsub - 1, 0), 0)),
            pl.BlockSpec((SUBLANES, Z_RKV_W), lambda i: (jnp.minimum((i + 1) * nsub, n // SUBLANES - 1), 0)),
            pl.BlockSpec((ROWS, Z_LORA_W), lambda i: (i, Z_LORA // Z_LORA_W)),
            pl.BlockSpec((SUBLANES, Z_LORA_W), lambda i: (jnp.maximum(i * nsub - 1, 0), Z_LORA // Z_LORA_W)),
            pl.BlockSpec((SUBLANES, Z_LORA_W),
                         lambda i: (jnp.minimum((i + 1) * nsub, n // SUBLANES - 1), Z_LORA // Z_LORA_W)),
            full(mup), full(mun), full(lmup), full(lmun),
            full(w0), full(wupf), full(wupb), full(a0), full(aupf), full(aupb),
            full(k_k), full(k_a), full(gup),
        ],
        out_specs=[ospec] * 10,
        out_shape=outs,
        compiler_params=_cparams(("parallel",)),
        name="rwkv_prep",
    )(z, z, z, z, z, z, mup, mun, lmup, lmun, w0, wupf, wupb, a0, aupf, aupb, k_k, k_a, gup)


def _wkv_intra(probs, bd, eye_bd, head_a, head_a2):
    n = CHUNK
    zero = jnp.zeros((n, LANES), F32)
    each = lambda f, *ls: [f(*xs) for xs in zip(*ls)]

    def stack(x):
        return jnp.concatenate([jnp.where(head_a, x, 0.0), jnp.where(head_a, 0.0, x)], axis=0)

    def cumulative(p):
        inclb = p[6].astype(BF16)
        l0, l1, l2 = _split(p[1], 3)
        return _dot(inclb, l0) + (_dot(inclb, l1) + _dot(inclb, l2))

    cs = each(cumulative, probs)

    def factors(p, c):
        r, lw, k, v, a, b = p[:6]
        ctot = jnp.sum(lw, axis=0, keepdims=True)
        mid = 0.5 * ctot
        e_in = jnp.exp(c - mid)
        e_out = jnp.exp(mid - c)
        e_abs = jnp.exp(c)
        e_rest = jnp.exp(ctot - c)
        undo = jnp.exp(-lw)
        lhs = jnp.concatenate([stack(a * e_in * undo), stack(r * e_in)], axis=0)
        rhs = jnp.concatenate([b * e_out, k * e_out], axis=0)
        rest = jnp.concatenate([b * e_rest, k * e_rest], axis=0)
        return lhs, rhs, rest, a * e_abs * undo, r * e_abs, jnp.exp(ctot)

    fs = each(factors, probs, cs)
    grams = [_mm(f[0], f[1], nt=True) for f in fs]
    aas = [jnp.where(p[7], g[:2 * n], 0.0) for p, g in zip(probs, grams)]
    ars = [jnp.where(p[8], g[2 * n:], 0.0) for p, g in zip(probs, grams)]
    a_abs = [jnp.where(bd, jnp.concatenate([aa[:n], pltpu.roll(aa[n:], n, 1)], axis=0), 0.0) for aa in aas]

    invs = [eye_bd + x for x in a_abs]
    pws = [_mm(x, x) for x in a_abs]
    for _ in range(int(np.log2(n)) - 2):
        boths = [_mm(jnp.concatenate([pw, inv], axis=0), pw) for pw, inv in zip(pws, invs)]
        pws = [x[:2 * n] for x in boths]
        invs = [inv + x[2 * n:] for inv, x in zip(invs, boths)]
    invs = [inv + _mm(inv, pw) for inv, pw in zip(invs, pws)]

    akvs = [jnp.where(bd, _mm(aa, jnp.concatenate([zero, p[3]], axis=0)), 0.0)
            for aa, p in zip(aas, probs)]
    wu_sts = [_mm(inv, jnp.concatenate([stack(f[3]), akv], axis=1))
              for inv, f, akv in zip(invs, fs, akvs)]
    m2s = [jnp.concatenate([x[:n] + x[n:], jnp.concatenate([zero, p[3]], axis=1)], axis=0)
           for x, p in zip(wu_sts, probs)]
    qy_sts = [_mm(ar, m2) for ar, m2 in zip(ars, m2s)]
    ghs = [_mm(m2.T, f[2]) for m2, f in zip(m2s, fs)]
    out = []
    for f, qy_st, gh in zip(fs, qy_sts, ghs):
        qy = jnp.where(head_a2, qy_st[:n], qy_st[n:])
        out.append((f[4] + qy[:, :LANES], qy[:, LANES:],
                    jnp.where(bd, gh[:LANES], 0.0), jnp.where(bd, gh[LANES:], 0.0), f[5]))
    return out


def _wkv_kernel(rf, vf, af, lwf, kf, bf, rb, vb, ab, lwb, kb, bb, yf_o, yb_o,
                s_ref, q_ref, y0_ref, gm_ref, h_ref, dec_ref):
    j = pl.program_id(1)
    n = CHUNK
    nsub = ROWS // n
    cur = j % 2
    prv = 1 - cur

    @pl.when(j == 0)
    def _():
        s_ref[...] = jnp.zeros_like(s_ref)
        q_ref[1] = jnp.zeros(q_ref.shape[1:], F32)
        y0_ref[1] = jnp.zeros(y0_ref.shape[1:], F32)
        gm_ref[1] = jnp.zeros(gm_ref.shape[1:], F32)
        h_ref[1] = jnp.zeros(h_ref.shape[1:], F32)
        dec_ref[1] = jnp.zeros(dec_ref.shape[1:], F32)

    for step in range(nsub):
        for d, y_o in enumerate((yf_o, yb_o)):
            for p in range(WKV_PAIRS):
                u = step if d == 0 else nsub - 1 - step
                idx = (d * WKV_PAIRS + p) * nsub + u
                s0 = s_ref[d * WKV_PAIRS + p]
                y_o[u * n:(u + 1) * n, p * LANES:(p + 1) * LANES] = (
                    _mm(q_ref[prv, idx], s0, nt=True) + y0_ref[prv, idx]).astype(y_o.dtype)
                s_ref[d * WKV_PAIRS + p] = (s0 * dec_ref[prv, idx][0:1, :]
                                            + (_mm(s0, gm_ref[prv, idx]) + h_ref[prv, idx]))

    ti = lax.broadcasted_iota(jnp.int32, (n, n), 0)
    si = lax.broadcasted_iota(jnp.int32, (n, n), 1)
    t2 = lax.broadcasted_iota(jnp.int32, (2 * n, 2 * n), 0)
    s2 = lax.broadcasted_iota(jnp.int32, (2 * n, 2 * n), 1)
    tw, sw = t2 % n, s2 % n
    bd = (t2 // n) == (s2 // n)
    eye_bd = jnp.where(t2 == s2, 1.0, 0.0).astype(F32)
    head_a = lax.broadcasted_iota(jnp.int32, (n, LANES), 1) < HEAD_DIM
    head_a2 = lax.broadcasted_iota(jnp.int32, (n, 2 * LANES), 1) % LANES < HEAD_DIM
    masks = (
        (jnp.where(si <= ti, 1.0, 0.0).astype(F32), sw < tw, sw <= tw),
        (jnp.where(si >= ti, 1.0, 0.0).astype(F32), sw > tw, sw >= tw),
    )
    srcs = ((rf, lwf, kf, vf, af, bf), (rb, lwb, kb, vb, ab, bb))

    probs = [tuple(x[u * n:(u + 1) * n, p * LANES:(p + 1) * LANES].astype(F32) for x in srcs[d]) + masks[d]
             for d in range(2) for p in range(WKV_PAIRS) for u in range(nsub)]
    for idx, (q, y0, gm, h, dec) in enumerate(_wkv_intra(probs, bd, eye_bd, head_a, head_a2)):
        q_ref[cur, idx] = q
        y0_ref[cur, idx] = y0
        gm_ref[cur, idx] = gm
        h_ref[cur, idx] = h
        dec_ref[cur, idx] = jnp.broadcast_to(dec, (SUBLANES, LANES))


def _wkv(r, v, nkk, lwf, kdf, bfw, lwb, kdb, bbw, nbatch, nblk):
    n = r.shape[0]
    ngrp = D_RWKV // (WKV_PAIRS * LANES)
    nprob = 2 * WKV_PAIRS * (ROWS // CHUNK)
    last = nblk - 1
    width = WKV_PAIRS * LANES

    def bwd_blk(j):
        return jnp.where(j == 0, 0, nblk - j)

    def spec(blk):
        return pl.BlockSpec((ROWS, width), lambda g, j: ((g // ngrp) * nblk + blk(j), g % ngrp))

    fin = spec(lambda j: jnp.minimum(j, last))
    bin_ = spec(lambda j: bwd_blk(jnp.minimum(j, last)))
    fout = spec(lambda j: jnp.maximum(j - 1, 0))
    bout = spec(lambda j: bwd_blk(jnp.maximum(j - 1, 0)))
    out = jax.ShapeDtypeStruct((n, D_RWKV), BF16)
    return pl.pallas_call(
        _wkv_kernel,
        grid=(nbatch * ngrp, nblk + 1),
        in_specs=[fin] * 6 + [bin_] * 6,
        out_specs=[fout, bout],
        out_shape=[out, out],
        scratch_shapes=[pltpu.VMEM((2 * WKV_PAIRS, LANES, LANES), F32),
                        pltpu.VMEM((2, nprob, CHUNK, LANES), F32), pltpu.VMEM((2, nprob, CHUNK, LANES), F32),
                        pltpu.VMEM((2, nprob, LANES, LANES), F32), pltpu.VMEM((2, nprob, LANES, LANES), F32),
                        pltpu.VMEM((2, nprob, SUBLANES, LANES), F32)],
        compiler_params=_cparams(("parallel", "arbitrary")),
        name="wkv",
    )(r, v, nkk, lwf, kdf, bfw, r, v, nkk, lwb, kdb, bbw)


def _rwkv_mix(yf_ref, yb_ref, r_ref, v_ref, kdf_ref, kdb_ref, gate_ref, rk_ref, lg_ref, lb_ref):
    ones = _head_ones().astype(BF16)
    mean_mat = (_head_ones() * (1.0 / HEAD_DIM)).astype(BF16)
    slabs = [slice(s * LANES, (s + 1) * LANES) for s in range(D_RWKV // LANES)]
    n = yf_ref.shape[0]

    def head_mean(y):
        hi, lo = _split(y, 2)
        both = _dot(jnp.concatenate([hi, lo], axis=0), mean_mat)
        return both[:n] + both[n:]

    ys = [yf_ref[:, sl].astype(F32) + yb_ref[:, sl].astype(F32) for sl in slabs]
    ycs = [y - head_mean(y) for y in ys]
    variances = [_dot((yc * yc).astype(BF16), mean_mat) for yc in ycs]
    sums = [_dot((r_ref[:, sl].astype(F32) * (kdf_ref[:, sl].astype(F32) + kdb_ref[:, sl].astype(F32))
                  * rk_ref[:, sl]).astype(BF16), ones) for sl in slabs]
    out = []
    for sl, yc, var, sm in zip(slabs, ycs, variances, sums):
        yn = yc * lax.rsqrt(var + LNX_EPS) * lg_ref[:, sl] + lb_ref[:, sl]
        bonus = sm * v_ref[:, sl].astype(F32)
        out.append(((yn + bonus) * gate_ref[:, sl].astype(F32)).astype(BF16))
    return jnp.concatenate(out, axis=1)


def _outproj_kernel(x_ref, attn_ref, yf_ref, yb_ref, r_ref, v_ref, kdf_ref, kdb_ref, gate_ref, rk_ref, lg_ref, lb_ref,
                    g1_ref, g_ref, sh_ref, sc_ref, wa_ref, wr_ref, o_ref, h_ref):
    rw = _rwkv_mix(yf_ref, yb_ref, r_ref, v_ref, kdf_ref, kdb_ref, gate_ref, rk_ref, lg_ref, lb_ref)
    mix = _dot(attn_ref[...], wa_ref[...]) + _dot(rw, wr_ref[...])
    x1 = x_ref[...] + g1_ref[0] * mix
    o_ref[...] = x1
    ms = jnp.mean(x1 * x1, axis=-1, keepdims=True)
    h = x1 * lax.rsqrt(ms + NORM_EPS) * g_ref[...]
    h_ref[...] = (h * (1.0 + sc_ref[0]) + sh_ref[0]).astype(h_ref.dtype)


def _outproj(x2, attn, yf, yb, r, v, kdf, kdb, gate, r_k, lnx_g, lnx_b, g1, g, sh, sc, wa, wr, t, nblk):
    n = x2.shape[0]
    lpb = t // ROWS
    modspec = pl.BlockSpec((1, 1, D_MODEL), lambda i: (i // lpb, 0, 0))
    rowspec = pl.BlockSpec((ROWS, D_MODEL), lambda i: (i, 0))
    seqspec = pl.BlockSpec((ROWS, D_RWKV), lambda i: ((i // lpb) * nblk + 1 + i % lpb, 0))
    pspec = pl.BlockSpec((1, D_RWKV), lambda i: (0, 0))
    return pl.pallas_call(
        _outproj_kernel,
        grid=(n // ROWS,),
        in_specs=[
            rowspec,
            pl.BlockSpec((ROWS, D_ATTN), lambda i: (i, 0)),
        ] + [seqspec] * 7 + [pspec] * 3 + [
            modspec,
            pl.BlockSpec((1, D_MODEL), lambda i: (0, 0)),
            modspec, modspec,
            pl.BlockSpec((D_ATTN, D_MODEL), lambda i: (0, 0), pipeline_mode=pl.Buffered(1)),
            pl.BlockSpec((D_RWKV, D_MODEL), lambda i: (0, 0), pipeline_mode=pl.Buffered(1)),
        ],
        out_specs=[rowspec, rowspec],
        out_shape=[jax.ShapeDtypeStruct((n, D_MODEL), F32), jax.ShapeDtypeStruct((n, D_MODEL), BF16)],
        compiler_params=_cparams(("parallel",)),
        name="outproj",
    )(x2, attn, yf, yb, r, v, kdf, kdb, gate, r_k, lnx_g, lnx_b, g1, g, sh, sc, wa, wr)


def _ffn_kernel(x_ref, h_ref, g2_ref, fg_ref, w1_ref, w3_ref, w2_ref, o_ref, acc_ref):
    j = pl.program_id(1)

    @pl.when(j == 0)
    def _():
        acc_ref[...] = jnp.zeros_like(acc_ref)

    h = h_ref[...]
    a = _dot(h, w1_ref[...])
    b = _dot(h, w3_ref[...])
    u = a * _sigmoid(a) * b
    acc_ref[...] += _dot(u.astype(BF16), w2_ref[...])

    @pl.when(j == pl.num_programs(1) - 1)
    def _():
        y = x_ref[...] + g2_ref[0] * acc_ref[...]
        ms = jnp.mean(y * y, axis=-1, keepdims=True)
        o_ref[...] = y * lax.rsqrt(ms + NORM_EPS) * fg_ref[...]


def _ffn(x1, h2, g2, fg, w1, w3, w2, t):
    n = x1.shape[0]
    tm, tf = FFN_TM, FFN_TF
    modspec = pl.BlockSpec((1, 1, D_MODEL), lambda i, j: (i // (t // tm), 0, 0))
    vecspec = pl.BlockSpec((1, D_MODEL), lambda i, j: (0, 0))
    return pl.pallas_call(
        _ffn_kernel,
        grid=(n // tm, D_FF // tf),
        in_specs=[
            pl.BlockSpec((tm, D_MODEL), lambda i, j: (i, 0)),
            pl.BlockSpec((tm, D_MODEL), lambda i, j: (i, 0)),
            modspec, vecspec,
            pl.BlockSpec((D_MODEL, tf), lambda i, j: (0, j)),
            pl.BlockSpec((D_MODEL, tf), lambda i, j: (0, j)),
            pl.BlockSpec((tf, D_MODEL), lambda i, j: (j, 0)),
        ],
        out_specs=pl.BlockSpec((tm, D_MODEL), lambda i, j: (i, 0)),
        out_shape=jax.ShapeDtypeStruct((n, D_MODEL), F32),
        scratch_shapes=[pltpu.VMEM((tm, D_MODEL), F32)],
        compiler_params=_cparams(("parallel", "arbitrary")),
        name="ffn",
    )(x1, h2, g2, fg, w1, w3, w2)


def _z_weights_kernel(w_ref, o_ref):
    q0, k0, v0, r0 = 0, D_ATTN, D_ATTN + Z_KV_W, D_ATTN + 2 * Z_KV_W
    lora0 = r0 + Z_RKV_W
    n_lora = 4 * D_LORA + D_GATE_LORA
    cast = lambda x: x.astype(o_ref.dtype)
    o_ref[:, Z_R:Z_R + Z_RKV_W] = cast(w_ref[:, r0:lora0])
    o_ref[:, Z_Q:Z_Q + D_ATTN] = cast(w_ref[:, q0:q0 + D_ATTN] * (HEAD_DIM ** -0.5 * LOG2E))
    o_ref[:, Z_LORA:Z_LORA + Z_LORA_W] = jnp.zeros((o_ref.shape[0], Z_LORA_W), o_ref.dtype)
    o_ref[:, Z_LORA:Z_LORA + n_lora] = cast(w_ref[:, lora0:lora0 + n_lora])
    o_ref[:, Z_KC:Z_KC + Z_KV_W] = cast(w_ref[:, k0:k0 + Z_KV_W])
    o_ref[:, Z_VC:Z_VC + Z_KV_W] = cast(w_ref[:, v0:v0 + Z_KV_W])


def _z_weights(w_in):
    rows = WPREP_ROWS
    return pl.pallas_call(
        _z_weights_kernel,
        grid=(D_MODEL // rows,),
        in_specs=[pl.BlockSpec((None, rows, w_in.shape[2]), lambda i: (0, i, 0))],
        out_specs=pl.BlockSpec((rows, NZ), lambda i: (i, 0)),
        out_shape=jax.ShapeDtypeStruct((D_MODEL, NZ), BF16),
        compiler_params=_cparams(("parallel",)),
        name="z_weights",
    )(w_in)


def _rope_tables(t):
    pos = np.arange(t)
    row = (pos // GRID_W).astype(np.float32)
    col = (pos % GRID_W).astype(np.float32)
    nf = HEAD_DIM // 4
    freqs = (np.float32(1.0) / np.float32(ROPE_BASE) ** (np.arange(nf, dtype=np.float32) / np.float32(nf))).astype(np.float32)
    d = np.arange(LANES) % HEAD_DIM
    use_row = (d // (HEAD_DIM // 2)) == 0
    first = (d % (HEAD_DIM // 2)) < nf
    ang = (np.where(use_row[None, :], row[:, None], col[:, None]) * freqs[d % nf][None, :]).astype(np.float32)
    sin = np.sin(ang)
    return jnp.asarray(np.cos(ang), F32), jnp.asarray(np.where(first[None, :], -sin, sin), F32)


def _pad_rows(w, top, total):
    return jnp.concatenate([jnp.zeros((top, w.shape[1]), w.dtype), w,
                            jnp.zeros((total - top - w.shape[0], w.shape[1]), w.dtype)], axis=0)


def kernel(x, c, ctx, c_ctx, ada_w, ada_b, norm1_g, norm2_g, w_in, attn_sink, ts_prev, ts_next, w0, w_up, a0, a_up,
           k_k, k_a, r_k, g_up, lnx_g, lnx_b, w_out, ffn_w1, ffn_w3, ffn_w2, final_norm_g):
    nbatch, t, d = x.shape
    nctx = ctx.shape[1]
    assert d == D_MODEL and nctx == ROWS and t % FFN_TM == 0 and nbatch < SUBLANES and ada_w.shape[0] == 1
    assert BLOCK == WINDOW and ATT_HEADS == N_KV_HEADS
    nblk = (nctx + t) // ROWS

    cc = jnp.concatenate([c, c_ctx[None, :], jnp.zeros((SUBLANES - nbatch - 1, d), F32)], axis=0)
    mod = _adaln(cc, ada_w[0], ada_b[0])
    mods = mod[:nbatch + 1].reshape(nbatch + 1, 6, 1, d)
    sh1, sc1, g1, sh2, sc2, g2 = (mods[:, i] for i in range(6))

    x2 = x.reshape(nbatch * t, d)
    cos_t, sin_t = _rope_tables(t)
    z = _inproj(x2, ctx.reshape(nbatch * nctx, d), norm1_g, sh1, sc1, cos_t, sin_t, _z_weights(w_in), nbatch, nblk)

    attn = _attention(z, attn_sink[0], nbatch, t, nctx)

    n_lora = ts_prev.shape[1] - Z_RKV_W
    pad_lora = lambda m: jnp.pad(m[:, Z_RKV_W:], ((0, 0), (0, Z_LORA_W - n_lora)))
    wupf = _pad_rows(w_up[0, 0], 0, LANES)
    wupb = _pad_rows(w_up[0, 1], D_LORA, LANES)
    aupf = _pad_rows(a_up[0, 0], 0, LANES)
    aupb = _pad_rows(a_up[0, 1], D_LORA, LANES)
    gup = _pad_rows(g_up[0], 0, 2 * LANES)
    r, v, lwf, lwb, kdf, kdb, nkk, bfw, bbw, gate = _rwkv_prep(
        z, ts_prev[:, :Z_RKV_W], ts_next[:, :Z_RKV_W], pad_lora(ts_prev), pad_lora(ts_next),
        w0[0], wupf, wupb, a0[0], aupf, aupb, k_k, k_a, gup, nblk)
    yf, yb = _wkv(r, v, nkk, lwf, kdf, bfw, lwb, kdb, bbw, nbatch, nblk)

    wo = w_out[0].astype(BF16)
    x1, h2 = _outproj(x2, attn, yf, yb, r, v, kdf, kdb, gate, r_k, lnx_g, lnx_b,
                      g1, norm2_g, sh2, sc2, wo[:D_ATTN], wo[D_ATTN:], t, nblk)
    out = _ffn(x1, h2, g2, final_norm_g.reshape(1, d),
               ffn_w1[0].astype(BF16), ffn_w3[0].astype(BF16), ffn_w2[0].astype(BF16), t)
    return out.reshape(nbatch, t, d)
```

```python
import functools
import math

import numpy as np
import jax
import jax.numpy as jnp
from jax import lax
from jax.experimental import pallas as pl
from jax.experimental.pallas import tpu as pltpu

F32 = jnp.float32
BF16 = jnp.bfloat16

D_MODEL = 2048
HEAD_DIM = 64
N_KV_HEADS = 4
GQA_GROUP = 4
D_ATTN = 1024
D_RWKV = 1024
D_LORA = 64
D_GATE_LORA = 160
D_FF = 5632
GRID_W = 64
WINDOW = 128
BLOCK = 128
ROPE_BASE = 10000.0
NORM_EPS = 1e-6
LNX_EPS = 64e-5
MASK_VALUE = -1e30
KK_EPS = 1e-12
ROT_HALF = HEAD_DIM // 2
ROT_PAIR = HEAD_DIM // 4
LOG2E = math.log2(math.e)

LANES = 128
SUBLANES = 8
V7X_VMEM_BYTES = 64 * 2 ** 20
VMEM_LIMIT = V7X_VMEM_BYTES - 8 * 2 ** 20

ROWS = 256
CHUNK = 64
WKV_PAIRS = 4
ATT_HEADS = 4
PREP_SLOTS = 3
ADALN_TN = 1024
PROJ_TN = 512
FFN_TM, FFN_TF = 512, 512
WPREP_ROWS = 256

Z_R, Z_K, Z_V = 0, 1024, 2048
Z_RKV_W = 3072
Z_Q = 3072
Z_LORA = 4096
Z_LORA_W = 512
Z_KV_W = N_KV_HEADS * HEAD_DIM
Z_KC, Z_VC = 4608, 4864
NZ = 5120
ROPE_SLABS = tuple(range(Z_Q, Z_Q + D_ATTN, LANES)) + tuple(range(Z_KC, Z_VC, LANES))


def _cparams(sem):
    return pltpu.CompilerParams(dimension_semantics=sem, vmem_limit_bytes=VMEM_LIMIT)


def _sigmoid(x):
    return 0.5 * jnp.tanh(0.5 * x) + 0.5


def _dot(a, b):
    return jnp.dot(a, b, preferred_element_type=F32)


def _dot_nt(a, b):
    return lax.dot_general(a, b, (((1,), (1,)), ((), ())), preferred_element_type=F32)


def _split(x, terms):
    out = []
    for _ in range(terms - 1):
        hi = x.astype(BF16)
        out.append(hi)
        x = x - hi.astype(F32)
    out.append(x.astype(BF16))
    return out


def _mm(a, b, nt=False):
    return (_dot_nt if nt else _dot)(a.astype(BF16), b.astype(BF16))


def _mm_fine(a, b):
    ah, al = _split(a, 2)
    bh, bl = _split(b, 2)
    return _dot(ah, bh) + (_dot(ah, bl) + _dot(al, bh))


def _adaln_kernel(c_ref, w_ref, b_ref, o_ref):
    c = c_ref[...]
    s = c * _sigmoid(c)
    o_ref[...] = _mm_fine(s, w_ref[...]) + b_ref[...]


def _adaln(cc, w, b):
    n = w.shape[1]
    tn = ADALN_TN
    return pl.pallas_call(
        _adaln_kernel,
        grid=(n // tn,),
        in_specs=[
            pl.BlockSpec((SUBLANES, D_MODEL), lambda j: (0, 0)),
            pl.BlockSpec((D_MODEL, tn), lambda j: (0, j)),
            pl.BlockSpec((1, tn), lambda j: (0, j)),
        ],
        out_specs=pl.BlockSpec((SUBLANES, tn), lambda j: (0, j)),
        out_shape=jax.ShapeDtypeStruct((SUBLANES, n), F32),
        compiler_params=_cparams(("parallel",)),
        name="adaln",
    )(cc, w, b.reshape(1, n))


def _inproj_kernel(x_ref, ctx_ref, g_ref, sh_ref, sc_ref, cos_ref, sin_ref, w_ref, z_ref, shw_ref, *, nblk):
    li = pl.program_id(0) % nblk
    is_ctx = li == 0

    @pl.when(li <= 1)
    def _():
        shb = jnp.broadcast_to(sh_ref[0], (SUBLANES, D_MODEL)).astype(BF16)
        for j in range(NZ // PROJ_TN):
            sl = slice(j * PROJ_TN, (j + 1) * PROJ_TN)
            shw_ref[:, sl] = _dot(shb, w_ref[:, sl])

    x = jnp.where(is_ctx, ctx_ref[...], x_ref[...])
    xb = (x * (g_ref[...] * (1.0 + sc_ref[0]))).astype(BF16)
    rs = lax.rsqrt(jnp.mean(x * x, axis=-1, keepdims=True) + NORM_EPS)
    cs = cos_ref[...]
    sn = sin_ref[...]
    first = (lax.broadcasted_iota(jnp.int32, (ROWS, LANES), 1) % ROT_HALF) < ROT_PAIR
    for j in range(NZ // PROJ_TN):
        sl = slice(j * PROJ_TN, (j + 1) * PROJ_TN)
        zt = _dot(xb, w_ref[:, sl]) * rs + shw_ref[0:1, sl]
        for c0 in range(j * PROJ_TN, (j + 1) * PROJ_TN, LANES):
            zz = zt[:, c0 - j * PROJ_TN:c0 - j * PROJ_TN + LANES]
            if c0 in ROPE_SLABS:
                partner = jnp.where(first, pltpu.roll(zz, LANES - ROT_PAIR, 1), pltpu.roll(zz, ROT_PAIR, 1))
                zz = jnp.where(is_ctx, zz, zz * cs + partner * sn)
            z_ref[:, c0:c0 + LANES] = zz


def _inproj(x2, ctx2, g, sh, sc, cos_t, sin_t, wz, nbatch, nblk):
    n = nbatch * nblk * ROWS
    lpb = nblk - 1

    def sel(i):
        return jnp.where(i % nblk == 0, nbatch, i // nblk)

    return pl.pallas_call(
        functools.partial(_inproj_kernel, nblk=nblk),
        grid=(n // ROWS,),
        in_specs=[
            pl.BlockSpec((ROWS, D_MODEL), lambda i: ((i // nblk) * lpb + jnp.maximum(i % nblk - 1, 0), 0)),
            pl.BlockSpec((ROWS, D_MODEL), lambda i: (i // nblk, 0)),
            pl.BlockSpec((1, D_MODEL), lambda i: (0, 0)),
            pl.BlockSpec((1, 1, D_MODEL), lambda i: (sel(i), 0, 0)),
            pl.BlockSpec((1, 1, D_MODEL), lambda i: (sel(i), 0, 0)),
            pl.BlockSpec((ROWS, LANES), lambda i: (jnp.maximum(i % nblk - 1, 0), 0)),
            pl.BlockSpec((ROWS, LANES), lambda i: (jnp.maximum(i % nblk - 1, 0), 0)),
            pl.BlockSpec((D_MODEL, NZ), lambda i: (0, 0), pipeline_mode=pl.Buffered(1)),
        ],
        out_specs=pl.BlockSpec((ROWS, NZ), lambda i: (i, 0)),
        out_shape=jax.ShapeDtypeStruct((n, NZ), F32),
        scratch_shapes=[pltpu.VMEM((SUBLANES, NZ), F32)],
        compiler_params=_cparams(("arbitrary",)),
        name="inproj",
    )(x2, ctx2, g, sh, sc, cos_t, sin_t, wz)


def _attn_kernel(sink_ref, q_ref, kp_ref, kc_ref, kn_ref, vp_ref, vc_ref, vn_ref, kx_ref, vx_ref, o_ref, *, nb):
    i = pl.program_id(1)
    hg = pl.program_id(2)
    lane = lax.broadcasted_iota(jnp.int32, (BLOCK, LANES), 1)
    lo = lane < HEAD_DIM
    heads = range(ATT_HEADS)

    qoff = lax.broadcasted_iota(jnp.int32, (GQA_GROUP * BLOCK, BLOCK), 0) % BLOCK
    koff = lax.broadcasted_iota(jnp.int32, (GQA_GROUP * BLOCK, BLOCK), 1)
    see_prev = jnp.where(i > 0, jnp.where(koff >= qoff, 1, 0), 0) > 0
    see_next = jnp.where(i < nb - 1, jnp.where(koff <= qoff, 1, 0), 0) > 0

    def masked(s):
        return jnp.concatenate([jnp.where(see_prev, s[:, :BLOCK], MASK_VALUE), s[:, BLOCK:2 * BLOCK],
                                jnp.where(see_next, s[:, 2 * BLOCK:3 * BLOCK], MASK_VALUE), s[:, 3 * BLOCK:]], axis=1)
    rowc = lax.broadcasted_iota(jnp.int32, (GQA_GROUP * BLOCK, 1), 0)

    order = (0, 2, 1, 3)

    def queries(h):
        parts = []
        for g in order:
            c0 = (h * GQA_GROUP + g) // 2 * LANES
            qp = q_ref[:, c0:c0 + LANES]
            parts.append(jnp.where(lo, qp, 0.0) if g % 2 == 0 else jnp.where(lo, 0.0, qp))
        return jnp.concatenate(parts, axis=0).astype(BF16)

    rolled = {}

    def slab_of(r, h):
        key = (id(r), h // 2)
        if key not in rolled:
            x = r[:, (h // 2) * LANES:(h // 2 + 1) * LANES]
            rolled[key] = (x, pltpu.roll(x, HEAD_DIM, 1))
        x, xr = rolled[key]
        return (x, xr) if h % 2 == 0 else (xr, x)

    def keys_of(h):
        parts = []
        for r in (kp_ref, kc_ref, kn_ref, kx_ref):
            on_lo, on_hi = slab_of(r, h)
            parts.append(jnp.where(lax.broadcasted_iota(jnp.int32, on_lo.shape, 1) < HEAD_DIM, on_lo, on_hi))
        return jnp.concatenate(parts, axis=0).astype(BF16)

    def values_of(h):
        va, vb = [], []
        for r in (vp_ref, vc_ref, vn_ref, vx_ref):
            on_lo, on_hi = slab_of(r, h)
            lo_r = lax.broadcasted_iota(jnp.int32, on_lo.shape, 1) < HEAD_DIM
            va.append(jnp.where(lo_r, on_lo, 1.0))
            vb.append(jnp.where(lo_r, 1.0, on_hi))
        return jnp.concatenate(va, axis=0).astype(BF16), jnp.concatenate(vb, axis=0).astype(BF16)

    def sinks(h):
        base = (hg * ATT_HEADS + h) * GQA_GROUP
        return LOG2E * jnp.where(rowc < BLOCK, sink_ref[base + order[0]],
                                 jnp.where(rowc < 2 * BLOCK, sink_ref[base + order[1]],
                                           jnp.where(rowc < 3 * BLOCK, sink_ref[base + order[2]],
                                                     sink_ref[base + order[3]])))

    half = 2 * BLOCK
    ss = [masked(_dot_nt(queries(h), keys_of(h))) for h in heads]
    sks = [sinks(h) for h in heads]
    ms = [jnp.maximum(jnp.max(s, axis=-1, keepdims=True), sk) for s, sk in zip(ss, sks)]
    ps = [jnp.exp2(s - m).astype(BF16) for s, m in zip(ss, ms)]
    extra = [jnp.exp2(sk - m) for sk, m in zip(sks, ms)]
    vals = [values_of(h) for h in heads]
    oa = [_dot(p[:half], v[0]) for p, v in zip(ps, vals)]
    ob = [_dot(p[half:], v[1]) for p, v in zip(ps, vals)]
    for h, a, b, e in zip(heads, oa, ob, extra):
        a = a / (pltpu.roll(a, HEAD_DIM, 1) + e[:half])
        b = b / (pltpu.roll(b, HEAD_DIM, 1) + e[half:])
        o_ref[:, 2 * h * LANES:(2 * h + 1) * LANES] = jnp.where(lo, a[:BLOCK], b[:BLOCK]).astype(o_ref.dtype)
        o_ref[:, (2 * h + 1) * LANES:(2 * h + 2) * LANES] = jnp.where(lo, a[BLOCK:], b[BLOCK:]).astype(o_ref.dtype)


def _attention(z, sink, nbatch, t, nctx):
    nb = t // BLOCK
    rpb = (nctx + t) // BLOCK
    cpb = nctx // BLOCK

    def lat(off):
        def f(b, i, h):
            return b * rpb + cpb + jnp.clip(i + off, 0, nb - 1)
        return f

    kw = ATT_HEADS * HEAD_DIM
    qw = ATT_HEADS * GQA_GROUP * HEAD_DIM

    def kspec(off, col0):
        return pl.BlockSpec((BLOCK, kw), lambda b, i, h: (lat(off)(b, i, h), col0 // kw + h))

    def xspec(col0):
        return pl.BlockSpec((nctx, kw), lambda b, i, h: (b * (rpb * BLOCK // nctx), col0 // kw + h))

    kcol, vcol = Z_KC, Z_VC
    return pl.pallas_call(
        functools.partial(_attn_kernel, nb=nb),
        grid=(nbatch, nb, N_KV_HEADS // ATT_HEADS),
        in_specs=[
            pl.BlockSpec(memory_space=pltpu.SMEM),
            pl.BlockSpec((BLOCK, qw), lambda b, i, h: (lat(0)(b, i, h), Z_Q // qw + h)),
            kspec(-1, kcol), kspec(0, kcol), kspec(1, kcol),
            kspec(-1, vcol), kspec(0, vcol), kspec(1, vcol),
            xspec(kcol), xspec(vcol),
        ],
        out_specs=pl.BlockSpec((BLOCK, qw), lambda b, i, h: (b * nb + i, h)),
        out_shape=jax.ShapeDtypeStruct((nbatch * t, D_ATTN), BF16),
        compiler_params=_cparams(("parallel", "parallel", "parallel")),
        name="attention",
    )(sink, z, z, z, z, z, z, z, z, z)


def _head_ones():
    r = lax.broadcasted_iota(jnp.int32, (LANES, LANES), 0)
    c = lax.broadcasted_iota(jnp.int32, (LANES, LANES), 1)
    return jnp.where((r // HEAD_DIM) == (c // HEAD_DIM), 1.0, 0.0).astype(F32)


def _prep_kernel(z_hbm, zp_ref, zn_ref, lm_ref, lp_ref, ln_ref, mup_ref, mun_ref, lmup_ref, lmun_ref,
                 w0_ref, wupf_ref, wupb_ref, a0_ref, aupf_ref, aupb_ref, kk_ref, ka_ref, gup_ref,
                 r_o, v_o, lwf_o, lwb_o, kdf_o, kdb_o, nkk_o, bf_o, bb_o, gate_o, zbuf, zsem, *, nblk, nsteps):
    i = pl.program_id(0)

    def z_copy(step):
        slot = step % PREP_SLOTS
        rows = pl.ds(pl.multiple_of(step * ROWS, ROWS), ROWS)
        return pltpu.make_async_copy(z_hbm.at[rows, pl.ds(0, Z_RKV_W)], zbuf.at[slot], zsem.at[slot])

    @pl.when(i == 0)
    def _():
        for s in range(PREP_SLOTS - 1):
            z_copy(s).start()

    @pl.when(i + PREP_SLOTS - 1 < nsteps)
    def _():
        z_copy(i + PREP_SLOTS - 1).start()

    z_copy(i).wait()
    zm_ref = zbuf.at[i % PREP_SLOTS]
    li = i % nblk
    keep_p = jnp.where((li == 0) | (li == 1), 0.0, 1.0)
    keep_n = jnp.where((li == 0) | (li == nblk - 1), 0.0, 1.0)
    row8 = lax.broadcasted_iota(jnp.int32, (SUBLANES, LANES), 0)

    def shift(refs, c0):
        m_ref, p_ref, n_ref, mp_ref, mn_ref = refs
        sl = slice(c0, c0 + LANES)
        zc = m_ref[:, sl]
        down = pltpu.roll(zc, 1, 0)
        up = pltpu.roll(zc, ROWS - 1, 0)
        last = SUBLANES - 1
        prev = jnp.concatenate([jnp.where(row8 == 0, p_ref[last:, sl] * keep_p, down[:SUBLANES]), down[SUBLANES:]],
                               axis=0)
        nxt = jnp.concatenate([up[:-SUBLANES], jnp.where(row8 == last, n_ref[0:1, sl] * keep_n, up[-SUBLANES:])],
                              axis=0)
        mp, mn = mp_ref[:, sl], mn_ref[:, sl]
        return zc * (1.0 - mp - mn) + mp * prev + mn * nxt

    shifted = functools.partial(shift, (zm_ref, zp_ref, zn_ref, mup_ref, mun_ref))
    lora = functools.partial(shift, (lm_ref, lp_ref, ln_ref, lmup_ref, lmun_ref))
    ones = _head_ones()
    th_w = jnp.tanh(lora(0))
    ad = lora(LANES)
    sg = jnp.concatenate([_sigmoid(lora(2 * LANES)), _sigmoid(lora(3 * LANES))], axis=1)

    def log_decay(pre):
        return -math.exp(-0.5) * _sigmoid(pre)

    th_b = th_w.astype(BF16)
    ad_b = ad.astype(BF16)
    sg_b = sg.astype(BF16)

    def lora_fine(xb, w):
        hi, lo = _split(w, 2)
        both = _dot(xb, jnp.concatenate([hi, lo], axis=1))
        return both[:, :LANES] + both[:, LANES:]

    for s in range(D_RWKV // LANES):
        sl = slice(s * LANES, (s + 1) * LANES)
        r = shifted(Z_R + s * LANES)
        k = shifted(Z_K + s * LANES)
        v = shifted(Z_V + s * LANES)
        r_o[:, sl] = r.astype(r_o.dtype)
        v_o[:, sl] = v.astype(v_o.dtype)
        kk = k * kk_ref[:, sl]
        ss = _mm(kk * kk, ones)
        kk = kk * lax.rsqrt(jnp.maximum(ss, KK_EPS))
        nkk_o[:, sl] = (-kk).astype(nkk_o.dtype)
        lwf_o[:, sl] = log_decay(w0_ref[0:1, sl] + lora_fine(th_b, wupf_ref[:, sl]))
        lwb_o[:, sl] = log_decay(w0_ref[1:2, sl] + lora_fine(th_b, wupb_ref[:, sl]))
        a_pre = _dot(ad_b, jnp.concatenate([aupf_ref[:, sl], aupb_ref[:, sl]], axis=1).astype(BF16))
        af = _sigmoid(a0_ref[0:1, sl] + a_pre[:, :LANES])
        ab = _sigmoid(a0_ref[1:2, sl] + a_pre[:, LANES:])
        ka = ka_ref[:, sl]
        kdf_o[:, sl] = (k * (1.0 + (af - 1.0) * ka)).astype(kdf_o.dtype)
        kdb_o[:, sl] = (k * (1.0 + (ab - 1.0) * ka)).astype(kdb_o.dtype)
        bf_o[:, sl] = (kk * af).astype(bf_o.dtype)
        bb_o[:, sl] = (kk * ab).astype(bb_o.dtype)
        gate_o[:, sl] = _dot(sg_b, gup_ref[:, sl].astype(BF16)).astype(gate_o.dtype)


def _rwkv_prep(z, mup, mun, lmup, lmun, w0, wupf, wupb, a0, aupf, aupb, k_k, k_a, gup, nblk):
    n = z.shape[0]
    nsub = ROWS // SUBLANES
    full = lambda a: pl.BlockSpec(a.shape, lambda i: (0,) * a.ndim)
    outs = [jax.ShapeDtypeStruct((n, D_RWKV), F32 if name in ("lwf", "lwb") else BF16)
            for name in ("r", "v", "lwf", "lwb", "kdf", "kdb", "nkk", "bf", "bb", "gate")]
    ospec = pl.BlockSpec((ROWS, D_RWKV), lambda i: (i, 0))
    return pl.pallas_call(
        functools.partial(_prep_kernel, nblk=nblk, nsteps=n // ROWS),
        grid=(n // ROWS,),
        in_specs=[
            pl.BlockSpec(memory_space=pl.ANY),
            pl.BlockSpec((SUBLANES, Z_RKV_W), lambda i: (jnp.maximum(i * nsub - 1, 0), 0)),
            pl.BlockSpec((SUBLANES, Z_RKV_W), lambda i: (jnp.minimum((i + 1) * nsub, n // SUBLANES - 1), 0)),
            pl.BlockSpec((ROWS, Z_LORA_W), lambda i: (i, Z_LORA // Z_LORA_W)),
            pl.BlockSpec((SUBLANES, Z_LORA_W), lambda i: (jnp.maximum(i * nsub - 1, 0), Z_LORA // Z_LORA_W)),
            pl.BlockSpec((SUBLANES, Z_LORA_W),
                         lambda i: (jnp.minimum((i + 1) * nsub, n // SUBLANES - 1), Z_LORA // Z_LORA_W)),
            full(mup), full(mun), full(lmup), full(lmun),
            full(w0), full(wupf), full(wupb), full(a0), full(aupf), full(aupb),
            full(k_k), full(k_a), full(gup),
        ],
        out_specs=[ospec] * 10,
        out_shape=outs,
        scratch_shapes=[pltpu.VMEM((PREP_SLOTS, ROWS, Z_RKV_W), F32), pltpu.SemaphoreType.DMA((PREP_SLOTS,))],
        compiler_params=_cparams(("arbitrary",)),
        name="rwkv_prep",
    )(z, z, z, z, z, z, mup, mun, lmup, lmun, w0, wupf, wupb, a0, aupf, aupb, k_k, k_a, gup)


def _wkv_intra(probs, bd, eye_bd, head_a, head_a2):
    n = CHUNK
    zero = jnp.zeros((n, LANES), F32)
    each = lambda f, *ls: [f(*xs) for xs in zip(*ls)]

    def stack(x):
        return jnp.concatenate([jnp.where(head_a, x, 0.0), jnp.where(head_a, 0.0, x)], axis=0)

    def cumulative(p):
        inclb = p[6].astype(BF16)
        l0, l1, l2 = _split(p[1], 3)
        return _dot(inclb, l0) + (_dot(inclb, l1) + _dot(inclb, l2))

    cs = each(cumulative, probs)

    def factors(p, c):
        r, lw, k, v, a, b = p[:6]
        ctot = jnp.sum(lw, axis=0, keepdims=True)
        mid = 0.5 * ctot
        e_in = jnp.exp(c - mid)
        e_out = jnp.exp(mid - c)
        e_abs = jnp.exp(c)
        e_rest = jnp.exp(ctot - c)
        undo = jnp.exp(-lw)
        lhs = jnp.concatenate([stack(a * e_in * undo), stack(r * e_in)], axis=0)
        rhs = jnp.concatenate([b * e_out, k * e_out], axis=0)
        rest = jnp.concatenate([b * e_rest, k * e_rest], axis=0)
        return lhs, rhs, rest, a * e_abs * undo, r * e_abs, jnp.exp(ctot)

    fs = each(factors, probs, cs)
    grams = [_mm(f[0], f[1], nt=True) for f in fs]
    aas = [jnp.where(p[7], g[:2 * n], 0.0) for p, g in zip(probs, grams)]
    ars = [jnp.where(p[8], g[2 * n:], 0.0) for p, g in zip(probs, grams)]
    a_abs = [jnp.where(bd, jnp.concatenate([aa[:n], pltpu.roll(aa[n:], n, 1)], axis=0), 0.0) for aa in aas]

    invs = [eye_bd + x for x in a_abs]
    pws = [_mm(x, x) for x in a_abs]
    for _ in range(int(np.log2(n)) - 2):
        boths = [_mm(jnp.concatenate([pw, inv], axis=0), pw) for pw, inv in zip(pws, invs)]
        pws = [x[:2 * n] for x in boths]
        invs = [inv + x[2 * n:] for inv, x in zip(invs, boths)]
    invs = [inv + _mm(inv, pw) for inv, pw in zip(invs, pws)]

    akvs = [jnp.where(bd, _mm(aa, jnp.concatenate([zero, p[3]], axis=0)), 0.0)
            for aa, p in zip(aas, probs)]
    wu_sts = [_mm(inv, jnp.concatenate([stack(f[3]), akv], axis=1))
              for inv, f, akv in zip(invs, fs, akvs)]
    m2s = [jnp.concatenate([x[:n] + x[n:], jnp.concatenate([zero, p[3]], axis=1)], axis=0)
           for x, p in zip(wu_sts, probs)]
    qy_sts = [_mm(ar, m2) for ar, m2 in zip(ars, m2s)]
    ghs = [_mm(m2.T, f[2]) for m2, f in zip(m2s, fs)]
    out = []
    for f, qy_st, gh in zip(fs, qy_sts, ghs):
        qy = jnp.where(head_a2, qy_st[:n], qy_st[n:])
        out.append((f[4] + qy[:, :LANES], qy[:, LANES:],
                    jnp.where(bd, gh[:LANES], 0.0), jnp.where(bd, gh[LANES:], 0.0), f[5]))
    return out


def _wkv_kernel(rf, vf, af, lwf, kf, bf, rb, vb, ab, lwb, kb, bb, yf_o, yb_o,
                s_ref, q_ref, y0_ref, gm_ref, h_ref, dec_ref):
    j = pl.program_id(1)
    n = CHUNK
    nsub = ROWS // n
    cur = j % 2
    prv = 1 - cur

    @pl.when(j == 0)
    def _():
        s_ref[...] = jnp.zeros_like(s_ref)
        q_ref[1] = jnp.zeros(q_ref.shape[1:], F32)
        y0_ref[1] = jnp.zeros(y0_ref.shape[1:], F32)
        gm_ref[1] = jnp.zeros(gm_ref.shape[1:], F32)
        h_ref[1] = jnp.zeros(h_ref.shape[1:], F32)
        dec_ref[1] = jnp.zeros(dec_ref.shape[1:], F32)

    for step in range(nsub):
        for d, y_o in enumerate((yf_o, yb_o)):
            for p in range(WKV_PAIRS):
                u = step if d == 0 else nsub - 1 - step
                idx = (d * WKV_PAIRS + p) * nsub + u
                s0 = s_ref[d * WKV_PAIRS + p]
                y_o[u * n:(u + 1) * n, p * LANES:(p + 1) * LANES] = (
                    _mm(q_ref[prv, idx], s0, nt=True) + y0_ref[prv, idx]).astype(y_o.dtype)
                s_ref[d * WKV_PAIRS + p] = (s0 * dec_ref[prv, idx][0:1, :]
                                            + (_mm(s0, gm_ref[prv, idx]) + h_ref[prv, idx]))

    ti = lax.broadcasted_iota(jnp.int32, (n, n), 0)
    si = lax.broadcasted_iota(jnp.int32, (n, n), 1)
    t2 = lax.broadcasted_iota(jnp.int32, (2 * n, 2 * n), 0)
    s2 = lax.broadcasted_iota(jnp.int32, (2 * n, 2 * n), 1)
    tw, sw = t2 % n, s2 % n
    bd = (t2 // n) == (s2 // n)
    eye_bd = jnp.where(t2 == s2, 1.0, 0.0).astype(F32)
    head_a = lax.broadcasted_iota(jnp.int32, (n, LANES), 1) < HEAD_DIM
    head_a2 = lax.broadcasted_iota(jnp.int32, (n, 2 * LANES), 1) % LANES < HEAD_DIM
    masks = (
        (jnp.where(si <= ti, 1.0, 0.0).astype(F32), sw < tw, sw <= tw),
        (jnp.where(si >= ti, 1.0, 0.0).astype(F32), sw > tw, sw >= tw),
    )
    srcs = ((rf, lwf, kf, vf, af, bf), (rb, lwb, kb, vb, ab, bb))

    probs = [tuple(x[u * n:(u + 1) * n, p * LANES:(p + 1) * LANES].astype(F32) for x in srcs[d]) + masks[d]
             for d in range(2) for p in range(WKV_PAIRS) for u in range(nsub)]
    for idx, (q, y0, gm, h, dec) in enumerate(_wkv_intra(probs, bd, eye_bd, head_a, head_a2)):
        q_ref[cur, idx] = q
        y0_ref[cur, idx] = y0
        gm_ref[cur, idx] = gm
        h_ref[cur, idx] = h
        dec_ref[cur, idx] = jnp.broadcast_to(dec, (SUBLANES, LANES))


def _wkv(r, v, nkk, lwf, kdf, bfw, lwb, kdb, bbw, nbatch, nblk):
    n = r.shape[0]
    ngrp = D_RWKV // (WKV_PAIRS * LANES)
    nprob = 2 * WKV_PAIRS * (ROWS // CHUNK)
    last = nblk - 1
    width = WKV_PAIRS * LANES

    def bwd_blk(j):
        return jnp.where(j == 0, 0, nblk - j)

    def spec(blk):
        return pl.BlockSpec((ROWS, width), lambda g, j: ((g // ngrp) * nblk + blk(j), g % ngrp))

    fin = spec(lambda j: jnp.minimum(j, last))
    bin_ = spec(lambda j: bwd_blk(jnp.minimum(j, last)))
    fout = spec(lambda j: jnp.maximum(j - 1, 0))
    bout = spec(lambda j: bwd_blk(jnp.maximum(j - 1, 0)))
    out = jax.ShapeDtypeStruct((n, D_RWKV), BF16)
    return pl.pallas_call(
        _wkv_kernel,
        grid=(nbatch * ngrp, nblk + 1),
        in_specs=[fin] * 6 + [bin_] * 6,
        out_specs=[fout, bout],
        out_shape=[out, out],
        scratch_shapes=[pltpu.VMEM((2 * WKV_PAIRS, LANES, LANES), F32),
                        pltpu.VMEM((2, nprob, CHUNK, LANES), F32), pltpu.VMEM((2, nprob, CHUNK, LANES), F32),
                        pltpu.VMEM((2, nprob, LANES, LANES), F32), pltpu.VMEM((2, nprob, LANES, LANES), F32),
                        pltpu.VMEM((2, nprob, SUBLANES, LANES), F32)],
        compiler_params=_cparams(("parallel", "arbitrary")),
        name="wkv",
    )(r, v, nkk, lwf, kdf, bfw, r, v, nkk, lwb, kdb, bbw)


def _rwkv_mix(yf_ref, yb_ref, r_ref, v_ref, kdf_ref, kdb_ref, gate_ref, rk_ref, lg_ref, lb_ref):
    ones = _head_ones().astype(BF16)
    mean_mat = (_head_ones() * (1.0 / HEAD_DIM)).astype(BF16)
    slabs = [slice(s * LANES, (s + 1) * LANES) for s in range(D_RWKV // LANES)]
    n = yf_ref.shape[0]

    def head_mean(y):
        hi, lo = _split(y, 2)
        both = _dot(jnp.concatenate([hi, lo], axis=0), mean_mat)
        return both[:n] + both[n:]

    ys = [yf_ref[:, sl].astype(F32) + yb_ref[:, sl].astype(F32) for sl in slabs]
    ycs = [y - head_mean(y) for y in ys]
    variances = [_dot((yc * yc).astype(BF16), mean_mat) for yc in ycs]
    sums = [_dot((r_ref[:, sl].astype(F32) * (kdf_ref[:, sl].astype(F32) + kdb_ref[:, sl].astype(F32))
                  * rk_ref[:, sl]).astype(BF16), ones) for sl in slabs]
    out = []
    for sl, yc, var, sm in zip(slabs, ycs, variances, sums):
        yn = yc * lax.rsqrt(var + LNX_EPS) * lg_ref[:, sl] + lb_ref[:, sl]
        bonus = sm * v_ref[:, sl].astype(F32)
        out.append(((yn + bonus) * gate_ref[:, sl].astype(F32)).astype(BF16))
    return jnp.concatenate(out, axis=1)


def _outproj_kernel(x_ref, attn_ref, yf_ref, yb_ref, r_ref, v_ref, kdf_ref, kdb_ref, gate_ref, rk_ref, lg_ref, lb_ref,
                    g1_ref, g_ref, sh_ref, sc_ref, wa_ref, wr_ref, o_ref, h_ref):
    rw = _rwkv_mix(yf_ref, yb_ref, r_ref, v_ref, kdf_ref, kdb_ref, gate_ref, rk_ref, lg_ref, lb_ref)
    mix = _dot(attn_ref[...], wa_ref[...]) + _dot(rw, wr_ref[...])
    x1 = x_ref[...] + g1_ref[0] * mix
    o_ref[...] = x1
    ms = jnp.mean(x1 * x1, axis=-1, keepdims=True)
    h = x1 * lax.rsqrt(ms + NORM_EPS) * g_ref[...]
    h_ref[...] = (h * (1.0 + sc_ref[0]) + sh_ref[0]).astype(h_ref.dtype)


def _outproj(x2, attn, yf, yb, r, v, kdf, kdb, gate, r_k, lnx_g, lnx_b, g1, g, sh, sc, wa, wr, t, nblk):
    n = x2.shape[0]
    lpb = t // ROWS
    modspec = pl.BlockSpec((1, 1, D_MODEL), lambda i: (i // lpb, 0, 0))
    rowspec = pl.BlockSpec((ROWS, D_MODEL), lambda i: (i, 0))
    seqspec = pl.BlockSpec((ROWS, D_RWKV), lambda i: ((i // lpb) * nblk + 1 + i % lpb, 0))
    pspec = pl.BlockSpec((1, D_RWKV), lambda i: (0, 0))
    return pl.pallas_call(
        _outproj_kernel,
        grid=(n // ROWS,),
        in_specs=[
            rowspec,
            pl.BlockSpec((ROWS, D_ATTN), lambda i: (i, 0)),
        ] + [seqspec] * 7 + [pspec] * 3 + [
            modspec,
            pl.BlockSpec((1, D_MODEL), lambda i: (0, 0)),
            modspec, modspec,
            pl.BlockSpec((D_ATTN, D_MODEL), lambda i: (0, 0), pipeline_mode=pl.Buffered(1)),
            pl.BlockSpec((D_RWKV, D_MODEL), lambda i: (0, 0), pipeline_mode=pl.Buffered(1)),
        ],
        out_specs=[rowspec, rowspec],
        out_shape=[jax.ShapeDtypeStruct((n, D_MODEL), F32), jax.ShapeDtypeStruct((n, D_MODEL), BF16)],
        compiler_params=_cparams(("parallel",)),
        name="outproj",
    )(x2, attn, yf, yb, r, v, kdf, kdb, gate, r_k, lnx_g, lnx_b, g1, g, sh, sc, wa, wr)


def _ffn_kernel(x_ref, h_ref, g2_ref, fg_ref, w1_ref, w3_ref, w2_ref, o_ref, acc_ref):
    j = pl.program_id(1)

    @pl.when(j == 0)
    def _():
        acc_ref[...] = jnp.zeros_like(acc_ref)

    h = h_ref[...]
    a = _dot(h, w1_ref[...])
    b = _dot(h, w3_ref[...])
    u = a * _sigmoid(a) * b
    acc_ref[...] += _dot(u.astype(BF16), w2_ref[...])

    @pl.when(j == pl.num_programs(1) - 1)
    def _():
        y = x_ref[...] + g2_ref[0] * acc_ref[...]
        ms = jnp.mean(y * y, axis=-1, keepdims=True)
        o_ref[...] = y * lax.rsqrt(ms + NORM_EPS) * fg_ref[...]


def _ffn(x1, h2, g2, fg, w1, w3, w2, t):
    n = x1.shape[0]
    tm, tf = FFN_TM, FFN_TF
    modspec = pl.BlockSpec((1, 1, D_MODEL), lambda i, j: (i // (t // tm), 0, 0))
    vecspec = pl.BlockSpec((1, D_MODEL), lambda i, j: (0, 0))
    return pl.pallas_call(
        _ffn_kernel,
        grid=(n // tm, D_FF // tf),
        in_specs=[
            pl.BlockSpec((tm, D_MODEL), lambda i, j: (i, 0)),
            pl.BlockSpec((tm, D_MODEL), lambda i, j: (i, 0)),
            modspec, vecspec,
            pl.BlockSpec((D_MODEL, tf), lambda i, j: (0, j)),
            pl.BlockSpec((D_MODEL, tf), lambda i, j: (0, j)),
            pl.BlockSpec((tf, D_MODEL), lambda i, j: (j, 0)),
        ],
        out_specs=pl.BlockSpec((tm, D_MODEL), lambda i, j: (i, 0)),
        out_shape=jax.ShapeDtypeStruct((n, D_MODEL), F32),
        scratch_shapes=[pltpu.VMEM((tm, D_MODEL), F32)],
        compiler_params=_cparams(("parallel", "arbitrary")),
        name="ffn",
    )(x1, h2, g2, fg, w1, w3, w2)


def _z_weights_kernel(w_ref, o_ref):
    q0, k0, v0, r0 = 0, D_ATTN, D_ATTN + Z_KV_W, D_ATTN + 2 * Z_KV_W
    lora0 = r0 + Z_RKV_W
    n_lora = 4 * D_LORA + D_GATE_LORA
    cast = lambda x: x.astype(o_ref.dtype)
    o_ref[:, Z_R:Z_R + Z_RKV_W] = cast(w_ref[:, r0:lora0])
    o_ref[:, Z_Q:Z_Q + D_ATTN] = cast(w_ref[:, q0:q0 + D_ATTN] * (HEAD_DIM ** -0.5 * LOG2E))
    o_ref[:, Z_LORA:Z_LORA + Z_LORA_W] = jnp.zeros((o_ref.shape[0], Z_LORA_W), o_ref.dtype)
    o_ref[:, Z_LORA:Z_LORA + n_lora] = cast(w_ref[:, lora0:lora0 + n_lora])
    o_ref[:, Z_KC:Z_KC + Z_KV_W] = cast(w_ref[:, k0:k0 + Z_KV_W])
    o_ref[:, Z_VC:Z_VC + Z_KV_W] = cast(w_ref[:, v0:v0 + Z_KV_W])


def _z_weights(w_in):
    rows = WPREP_ROWS
    return pl.pallas_call(
        _z_weights_kernel,
        grid=(D_MODEL // rows,),
        in_specs=[pl.BlockSpec((None, rows, w_in.shape[2]), lambda i: (0, i, 0))],
        out_specs=pl.BlockSpec((rows, NZ), lambda i: (i, 0)),
        out_shape=jax.ShapeDtypeStruct((D_MODEL, NZ), BF16),
        compiler_params=_cparams(("parallel",)),
        name="z_weights",
    )(w_in)


def _rope_tables(t):
    pos = np.arange(t)
    row = (pos // GRID_W).astype(np.float32)
    col = (pos % GRID_W).astype(np.float32)
    nf = HEAD_DIM // 4
    freqs = (np.float32(1.0) / np.float32(ROPE_BASE) ** (np.arange(nf, dtype=np.float32) / np.float32(nf))).astype(np.float32)
    d = np.arange(LANES) % HEAD_DIM
    use_row = (d // (HEAD_DIM // 2)) == 0
    first = (d % (HEAD_DIM // 2)) < nf
    ang = (np.where(use_row[None, :], row[:, None], col[:, None]) * freqs[d % nf][None, :]).astype(np.float32)
    sin = np.sin(ang)
    return jnp.asarray(np.cos(ang), F32), jnp.asarray(np.where(first[None, :], -sin, sin), F32)


def _pad_rows(w, top, total):
    return jnp.concatenate([jnp.zeros((top, w.shape[1]), w.dtype), w,
                            jnp.zeros((total - top - w.shape[0], w.shape[1]), w.dtype)], axis=0)


def kernel(x, c, ctx, c_ctx, ada_w, ada_b, norm1_g, norm2_g, w_in, attn_sink, ts_prev, ts_next, w0, w_up, a0, a_up,
           k_k, k_a, r_k, g_up, lnx_g, lnx_b, w_out, ffn_w1, ffn_w3, ffn_w2, final_norm_g):
    nbatch, t, d = x.shape
    nctx = ctx.shape[1]
    assert d == D_MODEL and nctx == ROWS and t % FFN_TM == 0 and nbatch < SUBLANES and ada_w.shape[0] == 1
    assert BLOCK == WINDOW and ATT_HEADS == N_KV_HEADS
    nblk = (nctx + t) // ROWS

    cc = jnp.concatenate([c, c_ctx[None, :], jnp.zeros((SUBLANES - nbatch - 1, d), F32)], axis=0)
    mod = _adaln(cc, ada_w[0], ada_b[0])
    mods = mod[:nbatch + 1].reshape(nbatch + 1, 6, 1, d)
    sh1, sc1, g1, sh2, sc2, g2 = (mods[:, i] for i in range(6))

    x2 = x.reshape(nbatch * t, d)
    cos_t, sin_t = _rope_tables(t)
    z = _inproj(x2, ctx.reshape(nbatch * nctx, d), norm1_g, sh1, sc1, cos_t, sin_t, _z_weights(w_in), nbatch, nblk)

    attn = _attention(z, attn_sink[0], nbatch, t, nctx)

    n_lora = ts_prev.shape[1] - Z_RKV_W
    pad_lora = lambda m: jnp.pad(m[:, Z_RKV_W:], ((0, 0), (0, Z_LORA_W - n_lora)))
    wupf = _pad_rows(w_up[0, 0], 0, LANES)
    wupb = _pad_rows(w_up[0, 1], D_LORA, LANES)
    aupf = _pad_rows(a_up[0, 0], 0, LANES)
    aupb = _pad_rows(a_up[0, 1], D_LORA, LANES)
    gup = _pad_rows(g_up[0], 0, 2 * LANES)
    r, v, lwf, lwb, kdf, kdb, nkk, bfw, bbw, gate = _rwkv_prep(
        z, ts_prev[:, :Z_RKV_W], ts_next[:, :Z_RKV_W], pad_lora(ts_prev), pad_lora(ts_next),
        w0[0], wupf, wupb, a0[0], aupf, aupb, k_k, k_a, gup, nblk)
    yf, yb = _wkv(r, v, nkk, lwf, kdf, bfw, lwb, kdb, bbw, nbatch, nblk)

    wo = w_out[0].astype(BF16)
    x1, h2 = _outproj(x2, attn, yf, yb, r, v, kdf, kdb, gate, r_k, lnx_g, lnx_b,
                      g1, norm2_g, sh2, sc2, wo[:D_ATTN], wo[D_ATTN:], t, nblk)
    out = _ffn(x1, h2, g2, final_norm_g.reshape(1, d),
               ffn_w1[0].astype(BF16), ffn_w3[0].astype(BF16), ffn_w2[0].astype(BF16), t)
    return out.reshape(nbatch, t, d)
```

```python
import functools
import math

import numpy as np
import jax
import jax.numpy as jnp
from jax import lax
from jax.experimental import pallas as pl
from jax.experimental.pallas import tpu as pltpu

F32 = jnp.float32
BF16 = jnp.bfloat16

D_MODEL = 2048
HEAD_DIM = 64
N_KV_HEADS = 4
GQA_GROUP = 4
D_ATTN = 1024
D_RWKV = 1024
D_LORA = 64
D_GATE_LORA = 160
D_FF = 5632
GRID_W = 64
WINDOW = 128
BLOCK = 128
ROPE_BASE = 10000.0
NORM_EPS = 1e-6
LNX_EPS = 64e-5
MASK_VALUE = -1e30
KK_EPS = 1e-12
ROT_HALF = HEAD_DIM // 2
ROT_PAIR = HEAD_DIM // 4
LOG2E = math.log2(math.e)

LANES = 128
SUBLANES = 8
V7X_VMEM_BYTES = 64 * 2 ** 20
VMEM_LIMIT = V7X_VMEM_BYTES - 8 * 2 ** 20

ROWS = 256
CHUNK = 64
WKV_PAIRS = 4
ATT_HEADS = 4
PREP_SLOTS = 3
ADALN_TN = 1024
PROJ_TN = 512
FFN_TM, FFN_TF = 512, 512
WPREP_ROWS = 256

Z_R, Z_K, Z_V = 0, 1024, 2048
Z_RKV_W = 3072
Z_Q = 3072
Z_LORA = 4096
Z_LORA_W = 512
Z_KV_W = N_KV_HEADS * HEAD_DIM
Z_KC, Z_VC = 4608, 4864
NZ = 5120
ROPE_SLABS = tuple(range(Z_Q, Z_Q + D_ATTN, LANES)) + tuple(range(Z_KC, Z_VC, LANES))


def _cparams(sem):
    return pltpu.CompilerParams(dimension_semantics=sem, vmem_limit_bytes=VMEM_LIMIT)


def _sigmoid(x):
    return 0.5 * jnp.tanh(0.5 * x) + 0.5


def _dot(a, b):
    return jnp.dot(a, b, preferred_element_type=F32)


def _dot_nt(a, b):
    return lax.dot_general(a, b, (((1,), (1,)), ((), ())), preferred_element_type=F32)


def _split(x, terms):
    out = []
    for _ in range(terms - 1):
        hi = x.astype(BF16)
        out.append(hi)
        x = x - hi.astype(F32)
    out.append(x.astype(BF16))
    return out


def _mm(a, b, nt=False):
    return (_dot_nt if nt else _dot)(a.astype(BF16), b.astype(BF16))


def _mm_fine(a, b):
    ah, al = _split(a, 2)
    bh, bl = _split(b, 2)
    return _dot(ah, bh) + (_dot(ah, bl) + _dot(al, bh))


def _adaln_kernel(c_ref, w_ref, b_ref, o_ref):
    c = c_ref[...]
    s = c * _sigmoid(c)
    o_ref[...] = _mm_fine(s, w_ref[...]) + b_ref[...]


def _adaln(cc, w, b):
    n = w.shape[1]
    tn = ADALN_TN
    return pl.pallas_call(
        _adaln_kernel,
        grid=(n // tn,),
        in_specs=[
            pl.BlockSpec((SUBLANES, D_MODEL), lambda j: (0, 0)),
            pl.BlockSpec((D_MODEL, tn), lambda j: (0, j)),
            pl.BlockSpec((1, tn), lambda j: (0, j)),
        ],
        out_specs=pl.BlockSpec((SUBLANES, tn), lambda j: (0, j)),
        out_shape=jax.ShapeDtypeStruct((SUBLANES, n), F32),
        compiler_params=_cparams(("parallel",)),
        name="adaln",
    )(cc, w, b.reshape(1, n))


def _inproj_kernel(x_ref, ctx_ref, g_ref, sh_ref, sc_ref, cos_ref, sin_ref, w_ref, z_ref, shw_ref, *, nblk):
    li = pl.program_id(0) % nblk
    is_ctx = li == 0

    @pl.when(li <= 1)
    def _():
        shb = jnp.broadcast_to(sh_ref[0], (SUBLANES, D_MODEL)).astype(BF16)
        for j in range(NZ // PROJ_TN):
            sl = slice(j * PROJ_TN, (j + 1) * PROJ_TN)
            shw_ref[:, sl] = _dot(shb, w_ref[:, sl])

    x = jnp.where(is_ctx, ctx_ref[...], x_ref[...])
    xb = (x * (g_ref[...] * (1.0 + sc_ref[0]))).astype(BF16)
    rs = lax.rsqrt(jnp.mean(x * x, axis=-1, keepdims=True) + NORM_EPS)
    cs = cos_ref[...]
    sn = sin_ref[...]
    first = (lax.broadcasted_iota(jnp.int32, (ROWS, LANES), 1) % ROT_HALF) < ROT_PAIR
    for j in range(NZ // PROJ_TN):
        sl = slice(j * PROJ_TN, (j + 1) * PROJ_TN)
        zt = _dot(xb, w_ref[:, sl]) * rs + shw_ref[0:1, sl]
        for c0 in range(j * PROJ_TN, (j + 1) * PROJ_TN, LANES):
            zz = zt[:, c0 - j * PROJ_TN:c0 - j * PROJ_TN + LANES]
            if c0 in ROPE_SLABS:
                partner = jnp.where(first, pltpu.roll(zz, LANES - ROT_PAIR, 1), pltpu.roll(zz, ROT_PAIR, 1))
                zz = jnp.where(is_ctx, zz, zz * cs + partner * sn)
            z_ref[:, c0:c0 + LANES] = zz


def _inproj(x2, ctx2, g, sh, sc, cos_t, sin_t, wz, nbatch, nblk):
    n = nbatch * nblk * ROWS
    lpb = nblk - 1

    def sel(i):
        return jnp.where(i % nblk == 0, nbatch, i // nblk)

    return pl.pallas_call(
        functools.partial(_inproj_kernel, nblk=nblk),
        grid=(n // ROWS,),
        in_specs=[
            pl.BlockSpec((ROWS, D_MODEL), lambda i: ((i // nblk) * lpb + jnp.maximum(i % nblk - 1, 0), 0)),
            pl.BlockSpec((ROWS, D_MODEL), lambda i: (i // nblk, 0)),
            pl.BlockSpec((1, D_MODEL), lambda i: (0, 0)),
            pl.BlockSpec((1, 1, D_MODEL), lambda i: (sel(i), 0, 0)),
            pl.BlockSpec((1, 1, D_MODEL), lambda i: (sel(i), 0, 0)),
            pl.BlockSpec((ROWS, LANES), lambda i: (jnp.maximum(i % nblk - 1, 0), 0)),
            pl.BlockSpec((ROWS, LANES), lambda i: (jnp.maximum(i % nblk - 1, 0), 0)),
            pl.BlockSpec((D_MODEL, NZ), lambda i: (0, 0), pipeline_mode=pl.Buffered(1)),
        ],
        out_specs=pl.BlockSpec((ROWS, NZ), lambda i: (i, 0)),
        out_shape=jax.ShapeDtypeStruct((n, NZ), F32),
        scratch_shapes=[pltpu.VMEM((SUBLANES, NZ), F32)],
        compiler_params=_cparams(("arbitrary",)),
        name="inproj",
    )(x2, ctx2, g, sh, sc, cos_t, sin_t, wz)


def _attn_kernel(sink_ref, q_ref, kp_ref, kc_ref, kn_ref, vp_ref, vc_ref, vn_ref, kx_ref, vx_ref, o_ref, *, nb):
    i = pl.program_id(1)
    hg = pl.program_id(2)
    lane = lax.broadcasted_iota(jnp.int32, (BLOCK, LANES), 1)
    lo = lane < HEAD_DIM
    heads = range(ATT_HEADS)

    qoff = lax.broadcasted_iota(jnp.int32, (GQA_GROUP * BLOCK, BLOCK), 0) % BLOCK
    koff = lax.broadcasted_iota(jnp.int32, (GQA_GROUP * BLOCK, BLOCK), 1)
    see_prev = jnp.where(i > 0, jnp.where(koff >= qoff, 1, 0), 0) > 0
    see_next = jnp.where(i < nb - 1, jnp.where(koff <= qoff, 1, 0), 0) > 0

    def masked(s):
        return jnp.concatenate([jnp.where(see_prev, s[:, :BLOCK], MASK_VALUE), s[:, BLOCK:2 * BLOCK],
                                jnp.where(see_next, s[:, 2 * BLOCK:3 * BLOCK], MASK_VALUE), s[:, 3 * BLOCK:]], axis=1)
    rowc = lax.broadcasted_iota(jnp.int32, (GQA_GROUP * BLOCK, 1), 0)

    order = (0, 2, 1, 3)

    def queries(h):
        parts = []
        for g in order:
            c0 = (h * GQA_GROUP + g) // 2 * LANES
            qp = q_ref[:, c0:c0 + LANES]
            parts.append(jnp.where(lo, qp, 0.0) if g % 2 == 0 else jnp.where(lo, 0.0, qp))
        return jnp.concatenate(parts, axis=0).astype(BF16)

    rolled = {}

    def slab_of(r, h):
        key = (id(r), h // 2)
        if key not in rolled:
            x = r[:, (h // 2) * LANES:(h // 2 + 1) * LANES]
            rolled[key] = (x, pltpu.roll(x, HEAD_DIM, 1))
        x, xr = rolled[key]
        return (x, xr) if h % 2 == 0 else (xr, x)

    def keys_of(h):
        parts = []
        for r in (kp_ref, kc_ref, kn_ref, kx_ref):
            on_lo, on_hi = slab_of(r, h)
            parts.append(jnp.where(lax.broadcasted_iota(jnp.int32, on_lo.shape, 1) < HEAD_DIM, on_lo, on_hi))
        return jnp.concatenate(parts, axis=0).astype(BF16)

    def values_of(h):
        va, vb = [], []
        for r in (vp_ref, vc_ref, vn_ref, vx_ref):
            on_lo, on_hi = slab_of(r, h)
            lo_r = lax.broadcasted_iota(jnp.int32, on_lo.shape, 1) < HEAD_DIM
            va.append(jnp.where(lo_r, on_lo, 1.0))
            vb.append(jnp.where(lo_r, 1.0, on_hi))
        return jnp.concatenate(va, axis=0).astype(BF16), jnp.concatenate(vb, axis=0).astype(BF16)

    def sinks(h):
        base = (hg * ATT_HEADS + h) * GQA_GROUP
        return LOG2E * jnp.where(rowc < BLOCK, sink_ref[base + order[0]],
                                 jnp.where(rowc < 2 * BLOCK, sink_ref[base + order[1]],
                                           jnp.where(rowc < 3 * BLOCK, sink_ref[base + order[2]],
                                                     sink_ref[base + order[3]])))

    half = 2 * BLOCK
    ss = [masked(_dot_nt(queries(h), keys_of(h))) for h in heads]
    sks = [sinks(h) for h in heads]
    ms = [jnp.maximum(jnp.max(s, axis=-1, keepdims=True), sk) for s, sk in zip(ss, sks)]
    ps = [jnp.exp2(s - m).astype(BF16) for s, m in zip(ss, ms)]
    extra = [jnp.exp2(sk - m) for sk, m in zip(sks, ms)]
    vals = [values_of(h) for h in heads]
    oa = [_dot(p[:half], v[0]) for p, v in zip(ps, vals)]
    ob = [_dot(p[half:], v[1]) for p, v in zip(ps, vals)]
    for h, a, b, e in zip(heads, oa, ob, extra):
        a = a / (pltpu.roll(a, HEAD_DIM, 1) + e[:half])
        b = b / (pltpu.roll(b, HEAD_DIM, 1) + e[half:])
        o_ref[:, 2 * h * LANES:(2 * h + 1) * LANES] = jnp.where(lo, a[:BLOCK], b[:BLOCK]).astype(o_ref.dtype)
        o_ref[:, (2 * h + 1) * LANES:(2 * h + 2) * LANES] = jnp.where(lo, a[BLOCK:], b[BLOCK:]).astype(o_ref.dtype)


def _attention(z, sink, nbatch, t, nctx):
    nb = t // BLOCK
    rpb = (nctx + t) // BLOCK
    cpb = nctx // BLOCK

    def lat(off):
        def f(b, i, h):
            return b * rpb + cpb + jnp.clip(i + off, 0, nb - 1)
        return f

    kw = ATT_HEADS * HEAD_DIM
    qw = ATT_HEADS * GQA_GROUP * HEAD_DIM

    def kspec(off, col0):
        return pl.BlockSpec((BLOCK, kw), lambda b, i, h: (lat(off)(b, i, h), col0 // kw + h))

    def xspec(col0):
        return pl.BlockSpec((nctx, kw), lambda b, i, h: (b * (rpb * BLOCK // nctx), col0 // kw + h))

    kcol, vcol = Z_KC, Z_VC
    return pl.pallas_call(
        functools.partial(_attn_kernel, nb=nb),
        grid=(nbatch, nb, N_KV_HEADS // ATT_HEADS),
        in_specs=[
            pl.BlockSpec(memory_space=pltpu.SMEM),
            pl.BlockSpec((BLOCK, qw), lambda b, i, h: (lat(0)(b, i, h), Z_Q // qw + h)),
            kspec(-1, kcol), kspec(0, kcol), kspec(1, kcol),
            kspec(-1, vcol), kspec(0, vcol), kspec(1, vcol),
            xspec(kcol), xspec(vcol),
        ],
        out_specs=pl.BlockSpec((BLOCK, qw), lambda b, i, h: (b * nb + i, h)),
        out_shape=jax.ShapeDtypeStruct((nbatch * t, D_ATTN), BF16),
        compiler_params=_cparams(("parallel", "parallel", "parallel")),
        name="attention",
    )(sink, z, z, z, z, z, z, z, z, z)


def _head_ones():
    r = lax.broadcasted_iota(jnp.int32, (LANES, LANES), 0)
    c = lax.broadcasted_iota(jnp.int32, (LANES, LANES), 1)
    return jnp.where((r // HEAD_DIM) == (c // HEAD_DIM), 1.0, 0.0).astype(F32)


def _prep_kernel(z_hbm, zp_ref, zn_ref, lm_ref, lp_ref, ln_ref, mup_ref, mun_ref, lmup_ref, lmun_ref,
                 w0_ref, wupf_ref, wupb_ref, a0_ref, aupf_ref, aupb_ref, kk_ref, ka_ref, gup_ref,
                 r_o, v_o, lwf_o, lwb_o, kdf_o, kdb_o, nkk_o, bf_o, bb_o, gate_o, zbuf, zsem, *, nblk, nsteps):
    i = pl.program_id(0)

    def z_copy(step):
        slot = step % PREP_SLOTS
        rows = pl.ds(pl.multiple_of(step * ROWS, ROWS), ROWS)
        return pltpu.make_async_copy(z_hbm.at[rows, pl.ds(0, Z_RKV_W)], zbuf.at[slot], zsem.at[slot])

    @pl.when(i == 0)
    def _():
        for s in range(PREP_SLOTS - 1):
            z_copy(s).start()

    @pl.when(i + PREP_SLOTS - 1 < nsteps)
    def _():
        z_copy(i + PREP_SLOTS - 1).start()

    z_copy(i).wait()
    zm_ref = zbuf.at[i % PREP_SLOTS]
    li = i % nblk
    keep_p = jnp.where((li == 0) | (li == 1), 0.0, 1.0)
    keep_n = jnp.where((li == 0) | (li == nblk - 1), 0.0, 1.0)
    row8 = lax.broadcasted_iota(jnp.int32, (SUBLANES, LANES), 0)

    def shift(refs, c0):
        m_ref, p_ref, n_ref, mp_ref, mn_ref = refs
        sl = slice(c0, c0 + LANES)
        zc = m_ref[:, sl]
        down = pltpu.roll(zc, 1, 0)
        up = pltpu.roll(zc, ROWS - 1, 0)
        last = SUBLANES - 1
        prev = jnp.concatenate([jnp.where(row8 == 0, p_ref[last:, sl] * keep_p, down[:SUBLANES]), down[SUBLANES:]],
                               axis=0)
        nxt = jnp.concatenate([up[:-SUBLANES], jnp.where(row8 == last, n_ref[0:1, sl] * keep_n, up[-SUBLANES:])],
                              axis=0)
        mp, mn = mp_ref[:, sl], mn_ref[:, sl]
        return zc * (1.0 - mp - mn) + mp * prev + mn * nxt

    shifted = functools.partial(shift, (zm_ref, zp_ref, zn_ref, mup_ref, mun_ref))
    lora = functools.partial(shift, (lm_ref, lp_ref, ln_ref, lmup_ref, lmun_ref))
    ones = _head_ones()
    th_w = jnp.tanh(lora(0))
    ad = lora(LANES)
    sg = jnp.concatenate([_sigmoid(lora(2 * LANES)), _sigmoid(lora(3 * LANES))], axis=1)

    def log_decay(pre):
        return -math.exp(-0.5) * _sigmoid(pre)

    th_b = th_w.astype(BF16)
    ad_b = ad.astype(BF16)
    sg_b = sg.astype(BF16)

    def lora_fine(xb, w):
        hi, lo = _split(w, 2)
        both = _dot(xb, jnp.concatenate([hi, lo], axis=1))
        return both[:, :LANES] + both[:, LANES:]

    for s in range(D_RWKV // LANES):
        sl = slice(s * LANES, (s + 1) * LANES)
        r = shifted(Z_R + s * LANES)
        k = shifted(Z_K + s * LANES)
        v = shifted(Z_V + s * LANES)
        r_o[:, sl] = r.astype(r_o.dtype)
        v_o[:, sl] = v.astype(v_o.dtype)
        kk = k * kk_ref[:, sl]
        ss = _mm(kk * kk, ones)
        kk = kk * lax.rsqrt(jnp.maximum(ss, KK_EPS))
        nkk_o[:, sl] = (-kk).astype(nkk_o.dtype)
        lwf_o[:, sl] = log_decay(w0_ref[0:1, sl] + lora_fine(th_b, wupf_ref[:, sl]))
        lwb_o[:, sl] = log_decay(w0_ref[1:2, sl] + lora_fine(th_b, wupb_ref[:, sl]))
        a_pre = _dot(ad_b, jnp.concatenate([aupf_ref[:, sl], aupb_ref[:, sl]], axis=1).astype(BF16))
        af = _sigmoid(a0_ref[0:1, sl] + a_pre[:, :LANES])
        ab = _sigmoid(a0_ref[1:2, sl] + a_pre[:, LANES:])
        ka = ka_ref[:, sl]
        kdf_o[:, sl] = (k * (1.0 + (af - 1.0) * ka)).astype(kdf_o.dtype)
        kdb_o[:, sl] = (k * (1.0 + (ab - 1.0) * ka)).astype(kdb_o.dtype)
        bf_o[:, sl] = (kk * af).astype(bf_o.dtype)
        bb_o[:, sl] = (kk * ab).astype(bb_o.dtype)
        gate_o[:, sl] = _dot(sg_b, gup_ref[:, sl].astype(BF16)).astype(gate_o.dtype)


def _rwkv_prep(z, mup, mun, lmup, lmun, w0, wupf, wupb, a0, aupf, aupb, k_k, k_a, gup, nblk):
    n = z.shape[0]
    nsub = ROWS // SUBLANES
    full = lambda a: pl.BlockSpec(a.shape, lambda i: (0,) * a.ndim)
    outs = [jax.ShapeDtypeStruct((n, D_RWKV), F32 if name in ("lwf", "lwb") else BF16)
            for name in ("r", "v", "lwf", "lwb", "kdf", "kdb", "nkk", "bf", "bb", "gate")]
    ospec = pl.BlockSpec((ROWS, D_RWKV), lambda i: (i, 0))
    return pl.pallas_call(
        functools.partial(_prep_kernel, nblk=nblk, nsteps=n // ROWS),
        grid=(n // ROWS,),
        in_specs=[
            pl.BlockSpec(memory_space=pl.ANY),
            pl.BlockSpec((SUBLANES, Z_RKV_W), lambda i: (jnp.maximum(i * nsub - 1, 0), 0)),
            pl.BlockSpec((SUBLANES, Z_RKV_W), lambda i: (jnp.minimum((i + 1) * nsub, n // SUBLANES - 1), 0)),
            pl.BlockSpec((ROWS, Z_LORA_W), lambda i: (i, Z_LORA // Z_LORA_W)),
            pl.BlockSpec((SUBLANES, Z_LORA_W), lambda i: (jnp.maximum(i * nsub - 1, 0), Z_LORA // Z_LORA_W)),
            pl.BlockSpec((SUBLANES, Z_LORA_W),
                         lambda i: (jnp.minimum((i + 1) * nsub, n // SUBLANES - 1), Z_LORA // Z_LORA_W)),
            full(mup), full(mun), full(lmup), full(lmun),
            full(w0), full(wupf), full(wupb), full(a0), full(aupf), full(aupb),
            full(k_k), full(k_a), full(gup),
        ],
        out_specs=[ospec] * 10,
        out_shape=outs,
        scratch_shapes=[pltpu.VMEM((PREP_SLOTS, ROWS, Z_RKV_W), F32), pltpu.SemaphoreType.DMA((PREP_SLOTS,))],
        compiler_params=_cparams(("arbitrary",)),
        name="rwkv_prep",
    )(z, z, z, z, z, z, mup, mun, lmup, lmun, w0, wupf, wupb, a0, aupf, aupb, k_k, k_a, gup)


def _wkv_intra(probs, bd, eye_bd, head_a, head_a2):
    n = CHUNK
    zero = jnp.zeros((n, LANES), F32)
    each = lambda f, *ls: [f(*xs) for xs in zip(*ls)]

    def stack(x):
        return jnp.concatenate([jnp.where(head_a, x, 0.0), jnp.where(head_a, 0.0, x)], axis=0)

    def cumulative(p):
        inclb = p[6].astype(BF16)
        l0, l1, l2 = _split(p[1], 3)
        return _dot(inclb, l0) + (_dot(inclb, l1) + _dot(inclb, l2))

    cs = each(cumulative, probs)

    def factors(p, c):
        r, lw, k, v, a, b = p[:6]
        ctot = jnp.sum(lw, axis=0, keepdims=True)
        mid = 0.5 * ctot
        e_in = jnp.exp(c - mid)
        e_out = jnp.exp(mid - c)
        e_abs = jnp.exp(c)
        e_rest = jnp.exp(ctot - c)
        undo = jnp.exp(-lw)
        lhs = jnp.concatenate([stack(a * e_in * undo), stack(r * e_in)], axis=0)
        rhs = jnp.concatenate([b * e_out, k * e_out], axis=0)
        rest = jnp.concatenate([b * e_rest, k * e_rest], axis=0)
        return lhs, rhs, rest, a * e_abs * undo, r * e_abs, jnp.exp(ctot)

    fs = each(factors, probs, cs)
    grams = [_mm(f[0], f[1], nt=True) for f in fs]
    aas = [jnp.where(p[7], g[:2 * n], 0.0) for p, g in zip(probs, grams)]
    ars = [jnp.where(p[8], g[2 * n:], 0.0) for p, g in zip(probs, grams)]
    a_abs = [jnp.where(bd, jnp.concatenate([aa[:n], pltpu.roll(aa[n:], n, 1)], axis=0), 0.0) for aa in aas]

    invs = [eye_bd + x for x in a_abs]
    pws = [_mm(x, x) for x in a_abs]
    for _ in range(int(np.log2(n)) - 2):
        boths = [_mm(jnp.concatenate([pw, inv], axis=0), pw) for pw, inv in zip(pws, invs)]
        pws = [x[:2 * n] for x in boths]
        invs = [inv + x[2 * n:] for inv, x in zip(invs, boths)]
    invs = [inv + _mm(inv, pw) for inv, pw in zip(invs, pws)]

    akvs = [jnp.where(bd, _mm(aa, jnp.concatenate([zero, p[3]], axis=0)), 0.0)
            for aa, p in zip(aas, probs)]
    wu_sts = [_mm(inv, jnp.concatenate([stack(f[3]), akv], axis=1))
              for inv, f, akv in zip(invs, fs, akvs)]
    m2s = [jnp.concatenate([x[:n] + x[n:], jnp.concatenate([zero, p[3]], axis=1)], axis=0)
           for x, p in zip(wu_sts, probs)]
    qy_sts = [_mm(ar, m2) for ar, m2 in zip(ars, m2s)]
    ghs = [_mm(m2.T, f[2]) for m2, f in zip(m2s, fs)]
    out = []
    for f, qy_st, gh in zip(fs, qy_sts, ghs):
        qy = jnp.where(head_a2, qy_st[:n], qy_st[n:])
        out.append((f[4] + qy[:, :LANES], qy[:, LANES:],
                    jnp.where(bd, gh[:LANES], 0.0), jnp.where(bd, gh[LANES:], 0.0), f[5]))
    return out


def _wkv_kernel(rf, vf, af, lwf, kf, bf, rb, vb, ab, lwb, kb, bb, yf_o, yb_o,
                s_ref, q_ref, y0_ref, gm_ref, h_ref, dec_ref):
    j = pl.program_id(1)
    n = CHUNK
    nsub = ROWS // n
    cur = j % 2
    prv = 1 - cur

    @pl.when(j == 0)
    def _():
        s_ref[...] = jnp.zeros_like(s_ref)
        q_ref[1] = jnp.zeros(q_ref.shape[1:], F32)
        y0_ref[1] = jnp.zeros(y0_ref.shape[1:], F32)
        gm_ref[1] = jnp.zeros(gm_ref.shape[1:], F32)
        h_ref[1] = jnp.zeros(h_ref.shape[1:], F32)
        dec_ref[1] = jnp.zeros(dec_ref.shape[1:], F32)

    for step in range(nsub):
        for d, y_o in enumerate((yf_o, yb_o)):
            for p in range(WKV_PAIRS):
                u = step if d == 0 else nsub - 1 - step
                idx = (d * WKV_PAIRS + p) * nsub + u
                s0 = s_ref[d * WKV_PAIRS + p]
                y_o[u * n:(u + 1) * n, p * LANES:(p + 1) * LANES] = (
                    _mm(q_ref[prv, idx], s0, nt=True) + y0_ref[prv, idx]).astype(y_o.dtype)
                s_ref[d * WKV_PAIRS + p] = (s0 * dec_ref[prv, idx][0:1, :]
                                            + (_mm(s0, gm_ref[prv, idx]) + h_ref[prv, idx]))

    ti = lax.broadcasted_iota(jnp.int32, (n, n), 0)
    si = lax.broadcasted_iota(jnp.int32, (n, n), 1)
    t2 = lax.broadcasted_iota(jnp.int32, (2 * n, 2 * n), 0)
    s2 = lax.broadcasted_iota(jnp.int32, (2 * n, 2 * n), 1)
    tw, sw = t2 % n, s2 % n
    bd = (t2 // n) == (s2 // n)
    eye_bd = jnp.where(t2 == s2, 1.0, 0.0).astype(F32)
    head_a = lax.broadcasted_iota(jnp.int32, (n, LANES), 1) < HEAD_DIM
    head_a2 = lax.broadcasted_iota(jnp.int32, (n, 2 * LANES), 1) % LANES < HEAD_DIM
    masks = (
        (jnp.where(si <= ti, 1.0, 0.0).astype(F32), sw < tw, sw <= tw),
        (jnp.where(si >= ti, 1.0, 0.0).astype(F32), sw > tw, sw >= tw),
    )
    srcs = ((rf, lwf, kf, vf, af, bf), (rb, lwb, kb, vb, ab, bb))

    probs = [tuple(x[u * n:(u + 1) * n, p * LANES:(p + 1) * LANES].astype(F32) for x in srcs[d]) + masks[d]
             for d in range(2) for p in range(WKV_PAIRS) for u in range(nsub)]
    for idx, (q, y0, gm, h, dec) in enumerate(_wkv_intra(probs, bd, eye_bd, head_a, head_a2)):
        q_ref[cur, idx] = q
        y0_ref[cur, idx] = y0
        gm_ref[cur, idx] = gm
        h_ref[cur, idx] = h
        dec_ref[cur, idx] = jnp.broadcast_to(dec, (SUBLANES, LANES))


def _wkv(r, v, nkk, lwf, kdf, bfw, lwb, kdb, bbw, nbatch, nblk):
    n = r.shape[0]
    ngrp = D_RWKV // (WKV_PAIRS * LANES)
    nprob = 2 * WKV_PAIRS * (ROWS // CHUNK)
    last = nblk - 1
    width = WKV_PAIRS * LANES

    def bwd_blk(j):
        return jnp.where(j == 0, 0, nblk - j)

    def spec(blk):
        return pl.BlockSpec((ROWS, width), lambda g, j: ((g // ngrp) * nblk + blk(j), g % ngrp))

    fin = spec(lambda j: jnp.minimum(j, last))
    bin_ = spec(lambda j: bwd_blk(jnp.minimum(j, last)))
    fout = spec(lambda j: jnp.maximum(j - 1, 0))
    bout = spec(lambda j: bwd_blk(jnp.maximum(j - 1, 0)))
    out = jax.ShapeDtypeStruct((n, D_RWKV), BF16)
    return pl.pallas_call(
        _wkv_kernel,
        grid=(nbatch * ngrp, nblk + 1),
        in_specs=[fin] * 6 + [bin_] * 6,
        out_specs=[fout, bout],
        out_shape=[out, out],
        scratch_shapes=[pltpu.VMEM((2 * WKV_PAIRS, LANES, LANES), F32),
                        pltpu.VMEM((2, nprob, CHUNK, LANES), F32), pltpu.VMEM((2, nprob, CHUNK, LANES), F32),
                        pltpu.VMEM((2, nprob, LANES, LANES), F32), pltpu.VMEM((2, nprob, LANES, LANES), F32),
                        pltpu.VMEM((2, nprob, SUBLANES, LANES), F32)],
        compiler_params=_cparams(("parallel", "arbitrary")),
        name="wkv",
    )(r, v, nkk, lwf, kdf, bfw, r, v, nkk, lwb, kdb, bbw)


def _rwkv_mix(yf_ref, yb_ref, r_ref, v_ref, kdf_ref, kdb_ref, gate_ref, rk_ref, lg_ref, lb_ref):
    ones = _head_ones().astype(BF16)
    mean_mat = (_head_ones() * (1.0 / HEAD_DIM)).astype(BF16)
    slabs = [slice(s * LANES, (s + 1) * LANES) for s in range(D_RWKV // LANES)]
    n = yf_ref.shape[0]

    def head_mean(y):
        hi, lo = _split(y, 2)
        both = _dot(jnp.concatenate([hi, lo], axis=0), mean_mat)
        return both[:n] + both[n:]

    ys = [yf_ref[:, sl].astype(F32) + yb_ref[:, sl].astype(F32) for sl in slabs]
    ycs = [y - head_mean(y) for y in ys]
    variances = [_dot((yc * yc).astype(BF16), mean_mat) for yc in ycs]
    sums = [_dot((r_ref[:, sl].astype(F32) * (kdf_ref[:, sl].astype(F32) + kdb_ref[:, sl].astype(F32))
                  * rk_ref[:, sl]).astype(BF16), ones) for sl in slabs]
    out = []
    for sl, yc, var, sm in zip(slabs, ycs, variances, sums):
        yn = yc * lax.rsqrt(var + LNX_EPS) * lg_ref[:, sl] + lb_ref[:, sl]
        bonus = sm * v_ref[:, sl].astype(F32)
        out.append(((yn + bonus) * gate_ref[:, sl].astype(F32)).astype(BF16))
    return jnp.concatenate(out, axis=1)


def _outproj_kernel(x_ref, attn_ref, yf_ref, yb_ref, r_ref, v_ref, kdf_ref, kdb_ref, gate_ref, rk_ref, lg_ref, lb_ref,
                    g1_ref, g_ref, sh_ref, sc_ref, wa_ref, wr_ref, o_ref, h_ref):
    rw = _rwkv_mix(yf_ref, yb_ref, r_ref, v_ref, kdf_ref, kdb_ref, gate_ref, rk_ref, lg_ref, lb_ref)
    mix = _dot(attn_ref[...], wa_ref[...]) + _dot(rw, wr_ref[...])
    x1 = x_ref[...] + g1_ref[0] * mix
    o_ref[...] = x1
    ms = jnp.mean(x1 * x1, axis=-1, keepdims=True)
    h = x1 * lax.rsqrt(ms + NORM_EPS) * g_ref[...]
    h_ref[...] = (h * (1.0 + sc_ref[0]) + sh_ref[0]).astype(h_ref.dtype)


def _outproj(x2, attn, yf, yb, r, v, kdf, kdb, gate, r_k, lnx_g, lnx_b, g1, g, sh, sc, wa, wr, t, nblk):
    n = x2.shape[0]
    lpb = t // ROWS
    modspec = pl.BlockSpec((1, 1, D_MODEL), lambda i: (i // lpb, 0, 0))
    rowspec = pl.BlockSpec((ROWS, D_MODEL), lambda i: (i, 0))
    seqspec = pl.BlockSpec((ROWS, D_RWKV), lambda i: ((i // lpb) * nblk + 1 + i % lpb, 0))
    pspec = pl.BlockSpec((1, D_RWKV), lambda i: (0, 0))
    return pl.pallas_call(
        _outproj_kernel,
        grid=(n // ROWS,),
        in_specs=[
            rowspec,
            pl.BlockSpec((ROWS, D_ATTN), lambda i: (i, 0)),
        ] + [seqspec] * 7 + [pspec] * 3 + [
            modspec,
            pl.BlockSpec((1, D_MODEL), lambda i: (0, 0)),
            modspec, modspec,
            pl.BlockSpec((D_ATTN, D_MODEL), lambda i: (0, 0), pipeline_mode=pl.Buffered(1)),
            pl.BlockSpec((D_RWKV, D_MODEL), lambda i: (0, 0), pipeline_mode=pl.Buffered(1)),
        ],
        out_specs=[rowspec, rowspec],
        out_shape=[jax.ShapeDtypeStruct((n, D_MODEL), F32), jax.ShapeDtypeStruct((n, D_MODEL), BF16)],
        compiler_params=_cparams(("parallel",)),
        name="outproj",
    )(x2, attn, yf, yb, r, v, kdf, kdb, gate, r_k, lnx_g, lnx_b, g1, g, sh, sc, wa, wr)


def _ffn_kernel(x_ref, h_ref, g2_ref, fg_ref, w1_ref, w3_ref, w2_ref, o_ref, acc_ref):
    j = pl.program_id(1)

    @pl.when(j == 0)
    def _():
        acc_ref[...] = jnp.zeros_like(acc_ref)

    def accumulated(rows):
        h = h_ref[rows, :]
        a = _dot(h, w1_ref[...])
        b = _dot(h, w3_ref[...])
        u = a * _sigmoid(a) * b
        return acc_ref[rows, :] + _dot(u.astype(BF16), w2_ref[...])

    last = pl.num_programs(1) - 1

    @pl.when(j < last)
    def _():
        acc_ref[...] = accumulated(slice(None))

    @pl.when(j == last)
    def _():
        half = x_ref.shape[0] // 2
        for rows in (slice(0, half), slice(half, 2 * half)):
            y = x_ref[rows, :] + g2_ref[0] * accumulated(rows)
            ms = jnp.mean(y * y, axis=-1, keepdims=True)
            o_ref[rows, :] = y * lax.rsqrt(ms + NORM_EPS) * fg_ref[...]


def _ffn(x1, h2, g2, fg, w1, w3, w2, t):
    n = x1.shape[0]
    tm, tf = FFN_TM, FFN_TF
    modspec = pl.BlockSpec((1, 1, D_MODEL), lambda i, j: (i // (t // tm), 0, 0))
    vecspec = pl.BlockSpec((1, D_MODEL), lambda i, j: (0, 0))
    return pl.pallas_call(
        _ffn_kernel,
        grid=(n // tm, D_FF // tf),
        in_specs=[
            pl.BlockSpec((tm, D_MODEL), lambda i, j: (i, 0)),
            pl.BlockSpec((tm, D_MODEL), lambda i, j: (i, 0)),
            modspec, vecspec,
            pl.BlockSpec((D_MODEL, tf), lambda i, j: (0, j)),
            pl.BlockSpec((D_MODEL, tf), lambda i, j: (0, j)),
            pl.BlockSpec((tf, D_MODEL), lambda i, j: (j, 0)),
        ],
        out_specs=pl.BlockSpec((tm, D_MODEL), lambda i, j: (i, 0)),
        out_shape=jax.ShapeDtypeStruct((n, D_MODEL), F32),
        scratch_shapes=[pltpu.VMEM((tm, D_MODEL), F32)],
        compiler_params=_cparams(("parallel", "arbitrary")),
        name="ffn",
    )(x1, h2, g2, fg, w1, w3, w2)


def _z_weights_kernel(w_ref, o_ref):
    q0, k0, v0, r0 = 0, D_ATTN, D_ATTN + Z_KV_W, D_ATTN + 2 * Z_KV_W
    lora0 = r0 + Z_RKV_W
    n_lora = 4 * D_LORA + D_GATE_LORA
    cast = lambda x: x.astype(o_ref.dtype)
    o_ref[:, Z_R:Z_R + Z_RKV_W] = cast(w_ref[:, r0:lora0])
    o_ref[:, Z_Q:Z_Q + D_ATTN] = cast(w_ref[:, q0:q0 + D_ATTN] * (HEAD_DIM ** -0.5 * LOG2E))
    o_ref[:, Z_LORA:Z_LORA + Z_LORA_W] = jnp.zeros((o_ref.shape[0], Z_LORA_W), o_ref.dtype)
    o_ref[:, Z_LORA:Z_LORA + n_lora] = cast(w_ref[:, lora0:lora0 + n_lora])
    o_ref[:, Z_KC:Z_KC + Z_KV_W] = cast(w_ref[:, k0:k0 + Z_KV_W])
    o_ref[:, Z_VC:Z_VC + Z_KV_W] = cast(w_ref[:, v0:v0 + Z_KV_W])


def _z_weights(w_in):
    rows = WPREP_ROWS
    return pl.pallas_call(
        _z_weights_kernel,
        grid=(D_MODEL // rows,),
        in_specs=[pl.BlockSpec((None, rows, w_in.shape[2]), lambda i: (0, i, 0))],
        out_specs=pl.BlockSpec((rows, NZ), lambda i: (i, 0)),
        out_shape=jax.ShapeDtypeStruct((D_MODEL, NZ), BF16),
        compiler_params=_cparams(("parallel",)),
        name="z_weights",
    )(w_in)


def _rope_tables(t):
    pos = np.arange(t)
    row = (pos // GRID_W).astype(np.float32)
    col = (pos % GRID_W).astype(np.float32)
    nf = HEAD_DIM // 4
    freqs = (np.float32(1.0) / np.float32(ROPE_BASE) ** (np.arange(nf, dtype=np.float32) / np.float32(nf))).astype(np.float32)
    d = np.arange(LANES) % HEAD_DIM
    use_row = (d // (HEAD_DIM // 2)) == 0
    first = (d % (HEAD_DIM // 2)) < nf
    ang = (np.where(use_row[None, :], row[:, None], col[:, None]) * freqs[d % nf][None, :]).astype(np.float32)
    sin = np.sin(ang)
    return jnp.asarray(np.cos(ang), F32), jnp.asarray(np.where(first[None, :], -sin, sin), F32)


def _pad_rows(w, top, total):
    return jnp.concatenate([jnp.zeros((top, w.shape[1]), w.dtype), w,
                            jnp.zeros((total - top - w.shape[0], w.shape[1]), w.dtype)], axis=0)


def kernel(x, c, ctx, c_ctx, ada_w, ada_b, norm1_g, norm2_g, w_in, attn_sink, ts_prev, ts_next, w0, w_up, a0, a_up,
           k_k, k_a, r_k, g_up, lnx_g, lnx_b, w_out, ffn_w1, ffn_w3, ffn_w2, final_norm_g):
    nbatch, t, d = x.shape
    nctx = ctx.shape[1]
    assert d == D_MODEL and nctx == ROWS and t % FFN_TM == 0 and nbatch < SUBLANES and ada_w.shape[0] == 1
    assert BLOCK == WINDOW and ATT_HEADS == N_KV_HEADS
    nblk = (nctx + t) // ROWS

    cc = jnp.concatenate([c, c_ctx[None, :], jnp.zeros((SUBLANES - nbatch - 1, d), F32)], axis=0)
    mod = _adaln(cc, ada_w[0], ada_b[0])
    mods = mod[:nbatch + 1].reshape(nbatch + 1, 6, 1, d)
    sh1, sc1, g1, sh2, sc2, g2 = (mods[:, i] for i in range(6))

    x2 = x.reshape(nbatch * t, d)
    cos_t, sin_t = _rope_tables(t)
    z = _inproj(x2, ctx.reshape(nbatch * nctx, d), norm1_g, sh1, sc1, cos_t, sin_t, _z_weights(w_in), nbatch, nblk)

    attn = _attention(z, attn_sink[0], nbatch, t, nctx)

    n_lora = ts_prev.shape[1] - Z_RKV_W
    pad_lora = lambda m: jnp.pad(m[:, Z_RKV_W:], ((0, 0), (0, Z_LORA_W - n_lora)))
    wupf = _pad_rows(w_up[0, 0], 0, LANES)
    wupb = _pad_rows(w_up[0, 1], D_LORA, LANES)
    aupf = _pad_rows(a_up[0, 0], 0, LANES)
    aupb = _pad_rows(a_up[0, 1], D_LORA, LANES)
    gup = _pad_rows(g_up[0], 0, 2 * LANES)
    r, v, lwf, lwb, kdf, kdb, nkk, bfw, bbw, gate = _rwkv_prep(
        z, ts_prev[:, :Z_RKV_W], ts_next[:, :Z_RKV_W], pad_lora(ts_prev), pad_lora(ts_next),
        w0[0], wupf, wupb, a0[0], aupf, aupb, k_k, k_a, gup, nblk)
    yf, yb = _wkv(r, v, nkk, lwf, kdf, bfw, lwb, kdb, bbw, nbatch, nblk)

    wo = w_out[0].astype(BF16)
    x1, h2 = _outproj(x2, attn, yf, yb, r, v, kdf, kdb, gate, r_k, lnx_g, lnx_b,
                      g1, norm2_g, sh2, sc2, wo[:D_ATTN], wo[D_ATTN:], t, nblk)
    out = _ffn(x1, h2, g2, final_norm_g.reshape(1, d),
               ffn_w1[0].astype(BF16), ffn_w3[0].astype(BF16), ffn_w2[0].astype(BF16), t)
    return out.reshape(nbatch, t, d)
```
